```python
import jax, jax.numpy as jnp
from jax import lax
import numpy as np

D_MODEL = 1024
BATCH = 2
SEQ = 8192
DEPTH = 4
DEC_BATCH = 32
DEC_SEQ = 1
PAST_LEN = 8192
PAGE_SIZE = 128

D_CONV = D_MODEL // 4
CONV_W = 3
D_GMLP = D_MODEL // 4
GMLP_GROUPS = 4
GMLP_GROUP_DIM = D_GMLP // GMLP_GROUPS
CHUNK = 128
N_HEADS = D_MODEL // 128
N_KV_HEADS = 2
HEADS_PER_KV = N_HEADS // N_KV_HEADS
HEAD_DIM = 64
D_ATTN = N_HEADS * HEAD_DIM
D_KV = N_KV_HEADS * HEAD_DIM
CMP_LEN = 32
CMP_STRIDE = 16
SEL_LEN = 64
TOP_N = 16
WINDOW = 512
Q_BLOCK = 128
N_BRANCH = 3
D_MIX = D_CONV + D_GMLP + D_ATTN
D_FF = ((8 * D_MODEL // 3 + 255) // 256) * 256
IN_SIZES = (D_CONV, D_CONV, D_CONV, D_GMLP, D_GMLP, D_ATTN,
            D_KV, D_KV, D_KV, D_KV, D_KV, D_KV, N_BRANCH * N_HEADS, N_BRANCH * D_MODEL)
D_IN = sum(IN_SIZES)
EPS = 1e-6
NEG = -1e30

kernel_name = 'hybrid_conv_gmlp_nsa_decoder_step'


def rms_norm(x, g):
    xf = x.astype(jnp.float32)
    y = xf * lax.rsqrt(jnp.mean(xf * xf, axis=-1, keepdims=True) + EPS)
    return (y * g.astype(jnp.float32)).astype(x.dtype)


def alibi_slopes():
    h = jnp.arange(1, N_HEADS + 1, dtype=jnp.float32)
    return jnp.exp2(-8.0 * h / N_HEADS)


def half_ffn(x, g, w_gate, w_up, w_down):
    h = rms_norm(x, g)
    return x + 0.5 * ((jax.nn.silu(h @ w_gate) * (h @ w_up)) @ w_down)


def masked_softmax(s, mask):
    p = jax.nn.softmax(jnp.where(mask, s, NEG), axis=-1)
    return jnp.where(mask, p, 0.0)


def split_in(z):
    offs = np.cumsum(IN_SIZES)[:-1].tolist()
    return jnp.split(z, offs, axis=-1)


def gather_pages(pool, page_table):
    rows = pool[page_table]
    return rows.reshape(page_table.shape[0], -1, N_KV_HEADS, HEAD_DIM)


def short_conv(b_gate, c_gate, x_in, conv_w, conv_state):
    z = c_gate * x_in
    zp = jnp.concatenate([conv_state.astype(z.dtype), z], axis=1)
    t = z.shape[1]
    y = sum(conv_w[j] * zp[:, j:j + t] for j in range(CONV_W))
    return b_gate * y, zp[:, zp.shape[1] - (CONV_W - 1):]


def chunk_gmlp(u, v, norm_g, w_s, b_s):
    n, t, _ = u.shape
    v = rms_norm(v, norm_g)
    c = min(t, CHUNK)
    w = jnp.where(jnp.tril(jnp.ones((c, c), dtype=bool)), w_s[:, :c, :c], 0.0)
    vg = v.reshape(n, t // c, c, GMLP_GROUPS, GMLP_GROUP_DIM)
    s = jnp.einsum('gts,nksgd->nktgd', w, vg) + b_s[:, :c].T[None, None, :, :, None]
    return u * s.reshape(n, t, D_GMLP), v


def compress(rows, w):
    n, seq_len = rows.shape[:2]
    n_half = seq_len // CMP_STRIDE
    h = rows[:, :n_half * CMP_STRIDE].reshape(n, n_half, CMP_STRIDE, N_KV_HEADS, HEAD_DIM)
    parts = CMP_LEN // CMP_STRIDE
    n_cmp = n_half - parts + 1
    return sum(jnp.einsum('njsgd,sgd->njgd', h[:, p:p + n_cmp], w[p * CMP_STRIDE:(p + 1) * CMP_STRIDE])
               for p in range(parts))


def nsa_attention(q, gates, q_start, kc, vc, ks, vs, kw, vw, win_start):
    f32 = jnp.float32
    n, t_len = q.shape[:2]
    seq_len = ks.shape[1]
    n_cmp = kc.shape[1]
    n_sel = -(-seq_len // SEL_LEN)
    top_n = min(TOP_N, n_sel)
    ratio = SEL_LEN // CMP_STRIDE
    lead = CMP_LEN // CMP_STRIDE - 1
    n_off = ratio + lead
    slopes = alibi_slopes().reshape(N_KV_HEADS, HEADS_PER_KV, 1)
    scale = HEAD_DIM ** -0.5

    kc32, vc32 = kc.astype(f32), vc.astype(f32)
    cmp_end = jnp.arange(n_cmp, dtype=jnp.int32) * CMP_STRIDE + (CMP_LEN - 1)
    pad = n_sel * SEL_LEN - seq_len

    def to_blocks(z):
        zp = jnp.pad(z, ((0, 0), (0, pad), (0, 0), (0, 0)))
        return zp.reshape(n, n_sel, SEL_LEN, N_KV_HEADS, HEAD_DIM).transpose(0, 3, 1, 2, 4)

    ksb, vsb = to_blocks(ks), to_blocks(vs)
    kwp = jnp.pad(kw, ((0, 0), (WINDOW, 0), (0, 0), (0, 0)))
    vwp = jnp.pad(vw, ((0, 0), (WINDOW, 0), (0, 0), (0, 0)))
    qb = Q_BLOCK if t_len % Q_BLOCK == 0 else t_len
    nb = t_len // qb
    q_blocks = q.reshape(n, nb, qb, N_KV_HEADS, HEADS_PER_KV, HEAD_DIM).transpose(1, 0, 2, 3, 4, 5)
    g_blocks = gates.reshape(n, nb, qb, N_KV_HEADS, HEADS_PER_KV, N_BRANCH).transpose(1, 0, 2, 3, 4, 5)
    starts = q_start + qb * jnp.arange(nb, dtype=jnp.int32)
    b_idx = jnp.arange(n)[:, None, None, None]
    g_idx = jnp.arange(N_KV_HEADS)[None, None, :, None]
    blk = jnp.arange(n_sel, dtype=jnp.int32)

    def one_block(args):
        q_blk, g_blk, p0 = args
        qf = q_blk.astype(f32) * scale
        t = p0 + jnp.arange(qb, dtype=jnp.int32)
        d_c = t[:, None] - cmp_end[None, :]
        s_c = jnp.einsum('nqghd,ncgd->nqghc', qf, kc32) - slopes * d_c[:, None, None, :].astype(f32)
        p_c = masked_softmax(s_c, (d_c >= 0)[:, None, None, :])
        o_c = jnp.einsum('nqghc,ncgd->nqghd', p_c, vc32)
        imp_c = jnp.pad(p_c.sum(axis=3), ((0, 0), (0, 0), (0, 0), (lead, ratio * n_sel + n_off - lead - n_cmp)))
        imp = sum(imp_c[..., o:o + ratio * n_sel:ratio] for o in range(n_off))
        cur = (t // SEL_LEN)[:, None]
        forced = (blk[None] == 0) | (blk[None] == cur) | (blk[None] == cur - 1)
        future = blk[None] * SEL_LEN > t[:, None]
        imp = jnp.where(forced[None, :, None, :], -NEG, imp)
        imp = jnp.where(future[None, :, None, :], NEG, imp)
        top_val, top_idx = lax.top_k(imp, top_n)
        n_k = top_n * SEL_LEN
        k_sel = ksb[b_idx, g_idx, top_idx].astype(f32).reshape(n, qb, N_KV_HEADS, n_k, HEAD_DIM)
        v_sel = vsb[b_idx, g_idx, top_idx].astype(f32).reshape(n, qb, N_KV_HEADS, n_k, HEAD_DIM)
        pos = top_idx[..., None] * SEL_LEN + jnp.arange(SEL_LEN, dtype=jnp.int32)
        t5 = t[None, :, None, None, None]
        ok_s = ((top_val[..., None] > 0.5 * NEG) & (pos <= t5)).reshape(n, qb, N_KV_HEADS, 1, n_k)
        d_s = (t5 - pos).reshape(n, qb, N_KV_HEADS, 1, n_k).astype(f32)
        s_s = jnp.einsum('nqghd,nqgkd->nqghk', qf, k_sel) - slopes * d_s
        p_s = masked_softmax(s_s, ok_s)
        o_s = jnp.einsum('nqghk,nqgkd->nqghd', p_s, v_sel)
        lo = p0 - win_start
        k_win = lax.dynamic_slice_in_dim(kwp, lo, WINDOW + qb, axis=1).astype(f32)
        v_win = lax.dynamic_slice_in_dim(vwp, lo, WINDOW + qb, axis=1).astype(f32)
        k_pos = p0 - WINDOW + jnp.arange(WINDOW + qb, dtype=jnp.int32)
        d_w = t[:, None] - k_pos[None, :]
        ok_w = (d_w >= 0) & (d_w <= WINDOW) & (k_pos[None, :] >= win_start)
        s_w = jnp.einsum('nqghd,nkgd->nqghk', qf, k_win) - slopes * d_w[:, None, None, :].astype(f32)
        p_w = masked_softmax(s_w, ok_w[:, None, None, :])
        o_w = jnp.einsum('nqghk,nkgd->nqghd', p_w, v_win)
        g = g_blk.astype(f32)
        o = g[..., 0:1] * o_c + g[..., 1:2] * o_s + g[..., 2:3] * o_w
        return o.astype(q.dtype)

    out = lax.map(one_block, (q_blocks, g_blocks, starts))
    return out.transpose(1, 0, 2, 3, 4, 5).reshape(n, t_len, D_ATTN)


def token_mixing(h, pos0, conv_state, cmp_k_past, cmp_v_past, sel_k_past, sel_v_past, win_k_past, win_v_past,
                 w_in, conv_w, gmlp_norm, gmlp_ws, gmlp_bs, q_norm, k_norm, cmp_wk, cmp_wv, w_branch, w_out):
    n, t, _ = h.shape
    (a_b, a_c, a_x, b_u, b_v, c_q, c_kc, c_vc, c_ks, c_vs, c_kw, c_vw, c_gate, m_gate) = split_in(h @ w_in)
    y_a, conv_new = short_conv(a_b, a_c, a_x, conv_w, conv_state)
    y_b, v_rows = chunk_gmlp(jax.nn.gelu(b_u), jax.nn.gelu(b_v), gmlp_norm, gmlp_ws, gmlp_bs)
    heads = lambda z: z.reshape(n, t, N_KV_HEADS, HEAD_DIM)
    cat = lambda past, new: jnp.concatenate([past.astype(new.dtype), new], axis=1)
    q = rms_norm(c_q.reshape(n, t, N_HEADS, HEAD_DIM), q_norm)
    kc_new, vc_new = heads(c_kc), heads(c_vc)
    ks_new, vs_new = rms_norm(heads(c_ks), k_norm[1]), heads(c_vs)
    kw_new, vw_new = rms_norm(heads(c_kw), k_norm[2]), heads(c_vw)
    kc = rms_norm(compress(cat(cmp_k_past, kc_new), cmp_wk), k_norm[0])
    vc = compress(cat(cmp_v_past, vc_new), cmp_wv)
    kw_all, vw_all = cat(win_k_past, kw_new), cat(win_v_past, vw_new)
    gates = jax.nn.sigmoid(c_gate).reshape(n, t, N_HEADS, N_BRANCH)
    y_c = nsa_attention(q, gates, pos0, kc, vc, cat(sel_k_past, ks_new), cat(sel_v_past, vs_new),
                        kw_all, vw_all, pos0 - win_k_past.shape[1])
    g = jax.nn.sigmoid(m_gate).reshape(n, t, N_BRANCH, D_MODEL)
    merged = (g[:, :, 0] * (y_a @ w_branch[:D_CONV])
              + g[:, :, 1] * (y_b @ w_branch[D_CONV:D_CONV + D_GMLP])
              + g[:, :, 2] * (y_c @ w_branch[D_CONV + D_GMLP:]))
    start = kw_all.shape[1] - min(WINDOW, kw_all.shape[1])
    return merged @ w_out, (kc_new, vc_new, ks_new, vs_new, kw_all[:, start:], vw_all[:, start:], conv_new, v_rows)


def setup_inputs(seed: int = 0) -> dict:
    key = jax.random.key(seed)
    keys = iter(jax.random.split(key, 40))

    def normal(shape, scale):
        return jax.random.normal(next(keys), shape, jnp.float32) * scale

    def gain(shape):
        return 1.0 + normal(shape, 0.02)

    n_pages = PAST_LEN // PAGE_SIZE
    n_used = DEC_BATCH * n_pages
    n_pool = n_used + max(1, n_used // 4)
    win_buf = min(WINDOW, PAST_LEN)
    page_table = jax.random.permutation(next(keys), n_pool)[:n_used].reshape(DEC_BATCH, n_pages).astype(jnp.int32)
    paged = (DEPTH, n_pool, PAGE_SIZE, N_KV_HEADS, HEAD_DIM)
    window = (DEPTH, DEC_BATCH, win_buf, N_KV_HEADS, HEAD_DIM)
    return {
        'x_prompt': normal((BATCH, SEQ, D_MODEL), 1.0),
        'x_sample': normal((DEC_BATCH, DEC_SEQ, D_MODEL), 1.0),
        'cache_cmp_k': normal(paged, 1.0),
        'cache_cmp_v': normal(paged, 1.0),
        'cache_sel_k': normal(paged, 1.0),
        'cache_sel_v': normal(paged, 1.0),
        'cache_win_k': normal(window, 1.0),
        'cache_win_v': normal(window, 1.0),
        'state_conv': normal((DEPTH, DEC_BATCH, CONV_W - 1, D_CONV), 0.5),
        'page_table': page_table,
        'ffn1_norm': gain((DEPTH, D_MODEL)),
        'ffn1_w_gate': normal((DEPTH, D_MODEL, D_FF), D_MODEL ** -0.5),
        'ffn1_w_up': normal((DEPTH, D_MODEL, D_FF), D_MODEL ** -0.5),
        'ffn1_w_down': normal((DEPTH, D_FF, D_MODEL), D_FF ** -0.5),
        'mix_norm': gain((DEPTH, D_MODEL)),
        'w_in': normal((DEPTH, D_MODEL, D_IN), D_MODEL ** -0.5),
        'conv_w': normal((DEPTH, CONV_W, D_CONV), CONV_W ** -0.5),
        'gmlp_norm': gain((DEPTH, D_GMLP)),
        'gmlp_ws': normal((DEPTH, GMLP_GROUPS, CHUNK, CHUNK), CHUNK ** -0.5),
        'gmlp_bs': gain((DEPTH, GMLP_GROUPS, CHUNK)),
        'q_norm': gain((DEPTH, HEAD_DIM)),
        'k_norm': gain((DEPTH, N_BRANCH, HEAD_DIM)),
        'cmp_wk': normal((DEPTH, CMP_LEN, N_KV_HEADS, HEAD_DIM), CMP_LEN ** -0.5),
        'cmp_wv': normal((DEPTH, CMP_LEN, N_KV_HEADS, HEAD_DIM), CMP_LEN ** -0.5),
        'w_branch': normal((DEPTH, D_MIX, D_MODEL), D_CONV ** -0.5),
        'w_out': normal((DEPTH, D_MODEL, D_MODEL), D_MODEL ** -0.5),
        'ffn2_norm': gain((DEPTH, D_MODEL)),
        'ffn2_w_gate': normal((DEPTH, D_MODEL, D_FF), D_MODEL ** -0.5),
        'ffn2_w_up': normal((DEPTH, D_MODEL, D_FF), D_MODEL ** -0.5),
        'ffn2_w_down': normal((DEPTH, D_FF, D_MODEL), D_FF ** -0.5),
    }


def reference(x_prompt, x_sample, cache_cmp_k, cache_cmp_v, cache_sel_k, cache_sel_v, cache_win_k, cache_win_v,
              state_conv, page_table, ffn1_norm, ffn1_w_gate, ffn1_w_up, ffn1_w_down, mix_norm, w_in, conv_w,
              gmlp_norm, gmlp_ws, gmlp_bs, q_norm, k_norm, cmp_wk, cmp_wv, w_branch, w_out,
              ffn2_norm, ffn2_w_gate, ffn2_w_up, ffn2_w_down):
    xp, xs = x_prompt, x_sample
    no_kv = jnp.zeros((BATCH, 0, N_KV_HEADS, HEAD_DIM), x_prompt.dtype)
    no_conv = jnp.zeros((BATCH, CONV_W - 1, D_CONV), x_prompt.dtype)
    prompt_new, sample_new = [], []
    for l in range(DEPTH):
        mix_w = (w_in[l], conv_w[l], gmlp_norm[l], gmlp_ws[l], gmlp_bs[l], q_norm[l], k_norm[l],
                 cmp_wk[l], cmp_wv[l], w_branch[l], w_out[l])
        ffn1 = (ffn1_norm[l], ffn1_w_gate[l], ffn1_w_up[l], ffn1_w_down[l])
        ffn2 = (ffn2_norm[l], ffn2_w_gate[l], ffn2_w_up[l], ffn2_w_down[l])
        xp = half_ffn(xp, *ffn1)
        m_p, st_p = token_mixing(rms_norm(xp, mix_norm[l]), 0, no_conv, no_kv, no_kv, no_kv, no_kv,
                                 no_kv, no_kv, *mix_w)
        xp = half_ffn(xp + m_p, *ffn2)
        prompt_new.append(st_p[:7])
        xs = half_ffn(xs, *ffn1)
        m_s, st_s = token_mixing(rms_norm(xs, mix_norm[l]), PAST_LEN, state_conv[l],
                                 gather_pages(cache_cmp_k[l], page_table), gather_pages(cache_cmp_v[l], page_table),
                                 gather_pages(cache_sel_k[l], page_table), gather_pages(cache_sel_v[l], page_table),
                                 cache_win_k[l], cache_win_v[l], *mix_w)
        xs = half_ffn(xs + m_s, *ffn2)
        sample_new.append(st_s)
    p_cmp_k, p_cmp_v, p_sel_k, p_sel_v, p_win_k, p_win_v, p_conv = [jnp.stack(a) for a in zip(*prompt_new)]
    s_cmp_k, s_cmp_v, s_sel_k, s_sel_v, s_win_k, s_win_v, s_conv, s_gmlp_v = [jnp.stack(a) for a in zip(*sample_new)]
    return (xp, xs, p_cmp_k, p_cmp_v, p_sel_k, p_sel_v, p_win_k, p_win_v, p_conv,
            s_cmp_k, s_cmp_v, s_sel_k, s_sel_v, s_win_k, s_win_v, s_conv, s_gmlp_v)
```

```python
import functools

import numpy as np
import jax
import jax.numpy as jnp
from jax import lax
from jax.experimental import pallas as pl
from jax.experimental.pallas import tpu as pltpu

F32 = jnp.float32
BF16 = jnp.bfloat16

HEAD_DIM = 64
N_HEADS = 8
N_KV_HEADS = 2
HEADS_PER_KV = N_HEADS // N_KV_HEADS
D_CONV = 256
CONV_W = 3
D_GMLP = 256
GMLP_GROUPS = 4
CHUNK = 128
D_ATTN = N_HEADS * HEAD_DIM
D_KV = N_KV_HEADS * HEAD_DIM
CMP_LEN = 32
CMP_STRIDE = 16
SEL_LEN = 64
TOP_N = 16
WINDOW = 512
Q_BLOCK = 128
N_BRANCH = 3
PAGE_SIZE = 128
EPS = 1e-6
NEG = -1e30
MASK_BIG = 2.0 ** 100
MAIN_COLS = 5 * 256 + D_ATTN + 6 * D_KV

LANES = 128
SUBLANES = 8
ROW_TILE = 512
FF_CHUNK = 256
KEY_CHUNK = 512
PAGES_PER_STEP = 8
VMEM_LIMIT = 56 * 1024 * 1024


def _cparams(*sem):
    return pltpu.CompilerParams(dimension_semantics=sem, vmem_limit_bytes=VMEM_LIMIT)


def _const_spec(shape):
    nd = len(shape)
    return pl.BlockSpec(shape, lambda *_: (0,) * nd, pipeline_mode=pl.Buffered(1))


def _rms(x, g):
    return x * lax.rsqrt(jnp.mean(x * x, axis=-1, keepdims=True) + EPS) * g


def _sigmoid(x):
    return 1.0 / (1.0 + jnp.exp(-x))


def _gelu_tanh(x):
    return 0.5 * x * (1.0 + jnp.tanh(0.7978845608028654 * (x + 0.044715 * (x * x * x))))


def _bdot(a, b):
    return jnp.dot(a.astype(BF16), b.astype(BF16), preferred_element_type=F32)


def _split3(x):
    hi = x.astype(BF16)
    r = x - hi.astype(F32)
    mid = r.astype(BF16)
    lo = (r - mid.astype(F32)).astype(BF16)
    return hi, mid, lo


def _exact_dot01(x, m01):
    hi, mid, lo = _split3(x)
    return (jnp.dot(hi, m01, preferred_element_type=F32) + jnp.dot(mid, m01, preferred_element_type=F32)
            + jnp.dot(lo, m01, preferred_element_type=F32))


def _head_group_ones(n):
    r = lax.broadcasted_iota(jnp.int32, (n, n), 0) // HEAD_DIM
    c = lax.broadcasted_iota(jnp.int32, (n, n), 1) // HEAD_DIM
    return jnp.where(r == c, 1.0, 0.0).astype(BF16)


def _head_rms(x, g, ones_bd):
    ssq = _exact_dot01(x * x, ones_bd)
    return x * lax.rsqrt(ssq * (1.0 / HEAD_DIM) + EPS) * g


def _swap_halves(x):
    return pltpu.roll(x, HEAD_DIM, axis=1)


def _value_ext(v, grp):
    lane = lax.broadcasted_iota(jnp.int32, v.shape, 1)
    src = v if grp == 0 else _swap_halves(v)
    return jnp.where(lane < HEAD_DIM, src, 1.0).astype(BF16)


def _pos_rows(pos):
    n = pos.shape[1]
    row = lax.broadcasted_iota(jnp.int32, (HEAD_DIM, n), 0)
    hi = (pos >> 7).astype(F32)
    lo = (pos & 127).astype(F32)
    return jnp.where(row == 0, hi, jnp.where(row == 1, lo, 0.0)).astype(BF16)


def _topk_select(imp, blk, n_iter):
    blk_f = blk.astype(F32)
    sel = jnp.zeros(imp.shape, dtype=jnp.bool_)
    for _ in range(n_iter):
        m = jnp.max(imp, axis=1, keepdims=True)
        idx = jnp.min(jnp.where(imp == m, blk_f, float(imp.shape[1])), axis=1, keepdims=True)
        pick = blk_f == idx
        sel = jnp.logical_or(sel, pick)
        imp = jnp.where(pick, -jnp.inf, imp)
    return sel


def _ffn_kernel(x_ref, g_ref, wg_ref, wu_ref, wd_ref, o_ref, acc_ref):
    x = x_ref[...]
    h = _rms(x, g_ref[...]).astype(BF16)
    d_ff = wg_ref.shape[1]
    for c in range(d_ff // FF_CHUNK):
        sl = slice(c * FF_CHUNK, (c + 1) * FF_CHUNK)
        gate = jnp.dot(h, wg_ref[:, sl], preferred_element_type=F32)
        up = jnp.dot(h, wu_ref[:, sl], preferred_element_type=F32)
        a = (gate * _sigmoid(gate) * up).astype(BF16)
        part = jnp.dot(a, wd_ref[sl, :], preferred_element_type=F32)
        if c == 0:
            acc_ref[...] = part
        else:
            acc_ref[...] += part
    o_ref[...] = x + 0.5 * acc_ref[...]


def _half_ffn(x, g, wg, wu, wd, tm):
    m, d = x.shape
    d_ff = wg.shape[1]
    return pl.pallas_call(
        _ffn_kernel,
        grid=(m // tm,),
        in_specs=[pl.BlockSpec((tm, d), lambda i: (i, 0)), _const_spec((1, d)),
                  _const_spec((d, d_ff)), _const_spec((d, d_ff)), _const_spec((d_ff, d))],
        out_specs=pl.BlockSpec((tm, d), lambda i: (i, 0)),
        out_shape=jax.ShapeDtypeStruct((m, d), F32),
        scratch_shapes=[pltpu.VMEM((tm, d), F32)],
        compiler_params=_cparams("arbitrary"),
        name="half_ffn",
    )(x, g, wg, wu, wd)


def _inproj_kernel(x_ref, g_ref, wm_ref, wcg_ref, cw_ref, gn_ref, ws_ref, bs_ref, qn_ref, kn_ref,
                   yab_ref, q_ref, gate_ref, kc_ref, vc_ref, ks_ref, vs_ref, kww_ref, vww_ref, conv_ref,
                   kst_ref, vse_ref, kwt_ref, vwe_ref, zbuf_ref, *, tm, tiles_per_seq):
    j = pl.program_id(0) % tiles_per_seq
    h = _rms(x_ref[...], g_ref[...]).astype(BF16)
    z = jnp.dot(h, wm_ref[...], preferred_element_type=F32)
    gate_ref[...] = _sigmoid(jnp.dot(h, wcg_ref[...], preferred_element_type=F32))

    a_b, a_c, a_x = z[:, 0:256], z[:, 256:512], z[:, 512:768]
    zc = a_c * a_x

    @pl.when(j == 0)
    def _():
        zbuf_ref[0:SUBLANES, :] = jnp.zeros((SUBLANES, D_CONV), F32)

    zbuf_ref[SUBLANES:SUBLANES + tm, :] = zc
    z1 = zbuf_ref[pl.ds(SUBLANES - 1, tm), :]
    z2 = zbuf_ref[pl.ds(SUBLANES - 2, tm), :]
    cw = cw_ref[...]
    y_a = a_b * (cw[0:1] * z2 + cw[1:2] * z1 + cw[2:3] * zc)
    tail = zbuf_ref[tm:tm + SUBLANES, :]
    zbuf_ref[0:SUBLANES, :] = tail
    conv_ref[...] = tail[SUBLANES - (CONV_W - 1):, :]

    u = _gelu_tanh(z[:, 768:1024])
    v = _rms(_gelu_tanh(z[:, 1024:1280]), gn_ref[...]).astype(BF16)
    tri = (lax.broadcasted_iota(jnp.int32, (CHUNK, CHUNK), 0)
           >= lax.broadcasted_iota(jnp.int32, (CHUNK, CHUNK), 1))
    wt = [jnp.where(tri, ws_ref[gi], 0.0).astype(BF16) for gi in range(GMLP_GROUPS)]
    lane_grp = lax.broadcasted_iota(jnp.int32, (CHUNK, D_GMLP), 1) // (D_GMLP // GMLP_GROUPS)
    bias = bs_ref[...]
    yb = []
    for ci in range(tm // CHUNK):
        vch = v[ci * CHUNK:(ci + 1) * CHUNK]
        s = bias
        for gi in range(GMLP_GROUPS):
            s = s + jnp.where(lane_grp == gi, jnp.dot(wt[gi], vch, preferred_element_type=F32), 0.0)
        yb.append(u[ci * CHUNK:(ci + 1) * CHUNK] * s)
    y_b = jnp.concatenate(yb, axis=0)
    yab_ref[...] = jnp.concatenate([y_a, y_b], axis=1).astype(BF16)

    ones_bd = _head_group_ones(256)
    qn = qn_ref[...]
    scale = HEAD_DIM ** -0.5
    q = jnp.concatenate([_head_rms(z[:, 1280:1536], qn, ones_bd),
                         _head_rms(z[:, 1536:1792], qn, ones_bd)], axis=1)
    q_ref[...] = (q * scale).astype(BF16)
    kc_ref[...] = z[:, 1792:1920]
    vc_ref[...] = z[:, 1920:2048]
    vs = z[:, 2176:2304]
    vw = z[:, 2432:2560]
    kn = _head_rms(jnp.concatenate([z[:, 2048:2176], z[:, 2304:2432]], axis=1), kn_ref[...], ones_bd)
    ks, kw = kn[:, 0:128], kn[:, 128:256]
    ks_ref[...] = ks
    vs_ref[...] = vs

    if tm >= WINDOW:
        kww_ref[...] = kw[tm - WINDOW:, :]
        vww_ref[...] = vw[tm - WINDOW:, :]
    else:
        first = tiles_per_seq - WINDOW // tm

        @pl.when(j >= first)
        def _():
            off = pl.multiple_of((j - first) * tm, tm)
            kww_ref[pl.ds(off, tm), :] = kw
            vww_ref[pl.ds(off, tm), :] = vw

    pos = j * tm + lax.broadcasted_iota(jnp.int32, (1, tm), 1)
    prow = _pos_rows(pos)
    kst = ks.T.astype(BF16)
    kwt = kw.T.astype(BF16)
    for grp in range(N_KV_HEADS):
        sl = slice(grp * HEAD_DIM, (grp + 1) * HEAD_DIM)
        kst_ref[grp] = jnp.concatenate([kst[sl], prow], axis=0)
        kwt_ref[grp] = jnp.concatenate([kwt[sl], prow], axis=0)
        vse_ref[grp] = _value_ext(vs, grp)
        vwe_ref[grp] = _value_ext(vw, grp)


def _in_proj_prompt(x, nb, seq, g, wm, wcg, cw, gn, ws, bs_tile, qn, kn12, tm):
    m, d = x.shape
    tps = seq // tm
    row = lambda i: (i, 0)
    rows = lambda w, dt: (pl.BlockSpec((tm, w), row), jax.ShapeDtypeStruct((m, w), dt))
    win = (pl.BlockSpec((None, WINDOW, D_KV), lambda i: (i // tps, 0, 0)),
           jax.ShapeDtypeStruct((nb, WINDOW, D_KV), F32))
    kt = (pl.BlockSpec((None, N_KV_HEADS, 2 * HEAD_DIM, tm), lambda i: (i // tps, 0, 0, i % tps)),
          jax.ShapeDtypeStruct((nb, N_KV_HEADS, 2 * HEAD_DIM, seq), BF16))
    ve = (pl.BlockSpec((None, N_KV_HEADS, tm, LANES), lambda i: (i // tps, 0, i % tps, 0)),
          jax.ShapeDtypeStruct((nb, N_KV_HEADS, seq, LANES), BF16))
    conv = (pl.BlockSpec((None, CONV_W - 1, D_CONV), lambda i: (i // tps, 0, 0)),
            jax.ShapeDtypeStruct((nb, CONV_W - 1, D_CONV), F32))
    outs = [rows(512, BF16), rows(512, BF16), rows(256, F32), rows(128, F32), rows(128, F32),
            rows(128, F32), rows(128, F32), win, win, conv, kt, ve, kt, ve]
    return pl.pallas_call(
        functools.partial(_inproj_kernel, tm=tm, tiles_per_seq=tps),
        grid=(m // tm,),
        in_specs=[pl.BlockSpec((tm, d), row), _const_spec((1, d)), _const_spec(wm.shape),
                  _const_spec(wcg.shape), _const_spec(cw.shape), _const_spec(gn.shape),
                  _const_spec(ws.shape), _const_spec(bs_tile.shape), _const_spec(qn.shape),
                  _const_spec(kn12.shape)],
        out_specs=[o[0] for o in outs],
        out_shape=[o[1] for o in outs],
        scratch_shapes=[pltpu.VMEM((tm + SUBLANES, D_CONV), F32)],
        compiler_params=_cparams("arbitrary"),
        name="in_proj_prompt",
    )(x, g, wm, wcg, cw, gn, ws, bs_tile, qn, kn12)


def _compress_rows(src_ref, w, n_half):
    p0 = jnp.zeros((n_half, D_KV), F32)
    p1 = jnp.zeros((n_half, D_KV), F32)
    for s in range(CMP_STRIDE):
        xs = src_ref[pl.ds(s, n_half, stride=CMP_STRIDE), :]
        p0 = p0 + xs * w[s:s + 1]
        p1 = p1 + xs * w[CMP_STRIDE + s:CMP_STRIDE + s + 1]
    return p0, p1


def _combine_halves(p0, p1):
    n = p0.shape[0]
    row = lax.broadcasted_iota(jnp.int32, p0.shape, 0)
    return jnp.where(row < n - 1, p0 + pltpu.roll(p1, n - 1, axis=0), 0.0)


def _compress_kernel(kc_ref, vc_ref, wk_ref, wv_ref, kn_ref, kct_ref, vce_ref, *, n_half):
    kc = _combine_halves(*_compress_rows(kc_ref, wk_ref[...], n_half))
    vc = _combine_halves(*_compress_rows(vc_ref, wv_ref[...], n_half))
    kc = _head_rms(kc, kn_ref[...], _head_group_ones(D_KV))
    kct = kc.T.astype(BF16)
    cmp_end = lax.broadcasted_iota(jnp.int32, (1, n_half), 1) * CMP_STRIDE + (CMP_LEN - 1)
    prow = _pos_rows(cmp_end)
    for grp in range(N_KV_HEADS):
        kct_ref[grp] = jnp.concatenate([kct[grp * HEAD_DIM:(grp + 1) * HEAD_DIM], prow], axis=0)
        vce_ref[grp] = _value_ext(vc, grp)


def _compress_prompt(kc, vc, nb, seq, wk, wv, kn0):
    n_half = seq // CMP_STRIDE
    return pl.pallas_call(
        functools.partial(_compress_kernel, n_half=n_half),
        grid=(nb,),
        in_specs=[pl.BlockSpec((seq, D_KV), lambda b: (b, 0)), pl.BlockSpec((seq, D_KV), lambda b: (b, 0)),
                  _const_spec(wk.shape), _const_spec(wv.shape), _const_spec(kn0.shape)],
        out_specs=[pl.BlockSpec((None, N_KV_HEADS, 2 * HEAD_DIM, n_half), lambda b: (b, 0, 0, 0)),
                   pl.BlockSpec((None, N_KV_HEADS, n_half, LANES), lambda b: (b, 0, 0, 0))],
        out_shape=[jax.ShapeDtypeStruct((nb, N_KV_HEADS, 2 * HEAD_DIM, n_half), BF16),
                   jax.ShapeDtypeStruct((nb, N_KV_HEADS, n_half, LANES), BF16)],
        compiler_params=_cparams("arbitrary"),
        name="compress_prompt",
    )(kc, vc, wk, wv, kn0)


def _attn_kernel(q_ref, gate_ref, slope_ref, kct_ref, vce_ref, kst_ref, vse_ref, kwt_ref, vwe_ref, o_ref,
                 qx_ref, m_ref, acc_ref, *, n_cmp_pad, n_blk_pad):
    qb = Q_BLOCK
    rows = HEADS_PER_KV * qb
    i = pl.program_id(2)
    p0 = i * qb

    q = q_ref[...].astype(F32)
    lane = lax.broadcasted_iota(jnp.int32, (qb, LANES), 1)
    parts = []
    for hp in range(HEADS_PER_KV):
        col = q[:, (hp // 2) * LANES:(hp // 2 + 1) * LANES]
        if hp % 2 == 1:
            col = _swap_halves(col)
        parts.append(jnp.where(lane < HEAD_DIM, col, 0.0))
    qx = (jnp.concatenate(parts, axis=0) + slope_ref[...]).astype(BF16)

    t_q = p0 + lax.broadcasted_iota(jnp.int32, (qb, 1), 0)
    t_rows = jnp.concatenate([t_q] * HEADS_PER_KV, axis=0)

    s_c = jnp.dot(qx, kct_ref[...], preferred_element_type=F32)
    cmp_end = lax.broadcasted_iota(jnp.int32, (1, n_cmp_pad), 1) * CMP_STRIDE + (CMP_LEN - 1)
    vis = cmp_end <= t_rows
    s_c = jnp.where(vis, s_c, NEG)
    e_c = jnp.where(vis, jnp.exp(s_c - jnp.max(s_c, axis=1, keepdims=True)), 0.0)
    l_c = jnp.sum(e_c, axis=1, keepdims=True)
    p_c = e_c * (1.0 / jnp.maximum(l_c, 1e-30))
    o_c = jnp.dot(p_c.astype(BF16), vce_ref[...], preferred_element_type=F32)

    psum = p_c[0:qb]
    for hp in range(1, HEADS_PER_KV):
        psum = psum + p_c[hp * qb:(hp + 1) * qb]
    cidx = lax.broadcasted_iota(jnp.int32, (n_cmp_pad, n_blk_pad), 0)
    bidx = lax.broadcasted_iota(jnp.int32, (n_cmp_pad, n_blk_pad), 1)
    ratio = SEL_LEN // CMP_STRIDE
    band = jnp.where((cidx >= ratio * bidx - 1) & (cidx <= ratio * bidx + ratio - 1)
                     & (cidx < n_cmp_pad - 1), 1.0, 0.0).astype(BF16)
    imp = _exact_dot01(psum, band)
    blk = lax.broadcasted_iota(jnp.int32, (qb, n_blk_pad), 1)
    cur = t_q >> 6
    forced = (blk == 0) | (blk == cur) | (blk == cur - 1)
    future = blk * SEL_LEN > t_q
    imp = jnp.where(forced, -NEG, imp)
    imp = jnp.where(future, NEG, imp)
    sel = _topk_select(imp, blk, TOP_N) & jnp.logical_not(future)
    selneg = jnp.where(sel, 0.0, -MASK_BIG).astype(BF16)
    qx_ref[:, 0:n_blk_pad] = jnp.concatenate([selneg] * HEADS_PER_KV, axis=0)
    qx_ref[:, n_blk_pad:] = qx

    m_ref[...] = jnp.full((rows, 1), NEG, F32)
    acc_ref[...] = jnp.zeros((rows, LANES), F32)
    erow = lax.broadcasted_iota(jnp.int32, (n_blk_pad, KEY_CHUNK), 0)
    ecol = lax.broadcasted_iota(jnp.int32, (n_blk_pad, KEY_CHUNK), 1) // SEL_LEN

    def chunk_step(c, causal):
        start = pl.multiple_of(c * KEY_CHUNK, KEY_CHUNK)
        expand = jnp.where(erow == ecol + c * (KEY_CHUNK // SEL_LEN), 1.0, 0.0).astype(BF16)
        k_ext = jnp.concatenate([expand, kst_ref[:, pl.ds(start, KEY_CHUNK)]], axis=0)
        s = jnp.dot(qx_ref[...], k_ext, preferred_element_type=F32)
        if causal:
            kpos = start + lax.broadcasted_iota(jnp.int32, (1, KEY_CHUNK), 1)
            s = jnp.where(kpos <= t_rows, s, NEG)
        m_old = m_ref[...]
        m_new = jnp.maximum(m_old, jnp.max(s, axis=1, keepdims=True))
        p = jnp.exp(s - m_new).astype(BF16)
        acc_ref[...] = (jnp.exp(m_old - m_new) * acc_ref[...]
                        + jnp.dot(p, vse_ref[pl.ds(start, KEY_CHUNK), :], preferred_element_type=F32))
        m_ref[...] = m_new

    last = p0 // KEY_CHUNK

    def body(c, carry):
        chunk_step(c, False)
        return carry

    lax.fori_loop(0, last, body, 0)
    chunk_step(last, True)
    acc_s = acc_ref[...]

    n_win = WINDOW + qb
    wstart = pl.multiple_of(jnp.maximum(p0 - WINDOW, 0), qb)
    s_w = jnp.dot(qx, kwt_ref[:, pl.ds(wstart, n_win)], preferred_element_type=F32)
    d_w = t_rows - (wstart + lax.broadcasted_iota(jnp.int32, (1, n_win), 1))
    ok_w = (d_w >= 0) & (d_w <= WINDOW)
    s_w = jnp.where(ok_w, s_w, NEG)
    e_w = jnp.where(ok_w, jnp.exp(s_w - jnp.max(s_w, axis=1, keepdims=True)), 0.0)
    acc_w = jnp.dot(e_w.astype(BF16), vwe_ref[pl.ds(wstart, n_win), :], preferred_element_type=F32)

    o_s = acc_s * (1.0 / _swap_halves(acc_s))
    o_w = acc_w * (1.0 / _swap_halves(acc_w))
    gate = gate_ref[...]
    res = []
    for hp in range(HEADS_PER_KV):
        sl = slice(hp * qb, (hp + 1) * qb)
        gc = gate[:, N_BRANCH * hp + 0:N_BRANCH * hp + 1]
        gs = gate[:, N_BRANCH * hp + 1:N_BRANCH * hp + 2]
        gw = gate[:, N_BRANCH * hp + 2:N_BRANCH * hp + 3]
        res.append(gc * o_c[sl] + gs * o_s[sl] + gw * o_w[sl])
    cols = [jnp.where(lane < HEAD_DIM, res[2 * k], _swap_halves(res[2 * k + 1])) for k in range(2)]
    o_ref[...] = jnp.concatenate(cols, axis=1).astype(o_ref.dtype)


def _attention_prompt(q, gates, slope_rows, kct, vce, kst, vse, kwt, vwe, nb, seq):
    nq = seq // Q_BLOCK
    n_cmp_pad = kct.shape[-1]
    n_blk_pad = max(LANES, seq // SEL_LEN)
    rows = HEADS_PER_KV * Q_BLOCK
    qspec = pl.BlockSpec((Q_BLOCK, HEADS_PER_KV * HEAD_DIM), lambda b, g, i: (b * nq + i, g))
    per_bg = lambda shape: pl.BlockSpec((None, None) + shape, lambda b, g, i: (b, g, 0, 0))
    return pl.pallas_call(
        functools.partial(_attn_kernel, n_cmp_pad=n_cmp_pad, n_blk_pad=n_blk_pad),
        grid=(nb, N_KV_HEADS, nq),
        in_specs=[qspec,
                  pl.BlockSpec((Q_BLOCK, LANES), lambda b, g, i: (b * nq + i, g)),
                  pl.BlockSpec((None, rows, LANES), lambda b, g, i: (g, 0, 0)),
                  per_bg((2 * HEAD_DIM, n_cmp_pad)), per_bg((n_cmp_pad, LANES)),
                  per_bg((2 * HEAD_DIM, seq)), per_bg((seq, LANES)),
                  per_bg((2 * HEAD_DIM, seq)), per_bg((seq, LANES))],
        out_specs=qspec,
        out_shape=jax.ShapeDtypeStruct(q.shape, BF16),
        scratch_shapes=[pltpu.VMEM((rows, n_blk_pad + LANES), BF16), pltpu.VMEM((rows, 1), F32),
                        pltpu.VMEM((rows, LANES), F32)],
        compiler_params=_cparams("arbitrary", "arbitrary", "arbitrary"),
        name="attention_prompt",
    )(q, gates, slope_rows, kct, vce, kst, vse, kwt, vwe)


def _merge_kernel(x_ref, yab_ref, yc_ref, g_ref, wmg_ref, wb_ref, wo_ref, o_ref):
    x = x_ref[...]
    d = x.shape[1]
    h = _rms(x, g_ref[...]).astype(BF16)
    yab = yab_ref[...]
    branches = (jnp.dot(yab[:, 0:D_CONV], wb_ref[0:D_CONV, :], preferred_element_type=F32),
                jnp.dot(yab[:, D_CONV:], wb_ref[D_CONV:D_CONV + D_GMLP, :], preferred_element_type=F32),
                jnp.dot(yc_ref[...], wb_ref[D_CONV + D_GMLP:, :], preferred_element_type=F32))
    merged = None
    for k, y in enumerate(branches):
        gk = _sigmoid(jnp.dot(h, wmg_ref[:, k * d:(k + 1) * d], preferred_element_type=F32))
        merged = gk * y if merged is None else merged + gk * y
    o_ref[...] = x + jnp.dot(merged.astype(BF16), wo_ref[...], preferred_element_type=F32)


def _merge(x, yab, yc, g, wmg, wb, wo, tm):
    m, d = x.shape
    row = lambda i: (i, 0)
    return pl.pallas_call(
        _merge_kernel,
        grid=(m // tm,),
        in_specs=[pl.BlockSpec((tm, d), row), pl.BlockSpec((tm, yab.shape[1]), row),
                  pl.BlockSpec((tm, yc.shape[1]), row), _const_spec((1, d)), _const_spec(wmg.shape),
                  _const_spec(wb.shape), _const_spec(wo.shape)],
        out_specs=pl.BlockSpec((tm, d), row),
        out_shape=jax.ShapeDtypeStruct((m, d), F32),
        compiler_params=_cparams("arbitrary"),
        name="merge_out",
    )(x, yab, yc, g, wmg, wb, wo)


def _inproj_sample_kernel(x_ref, g_ref, wm_ref, wcg_ref, cw_ref, st0_ref, st1_ref, gn_ref, ws0_ref, bs0_ref,
                          qn_ref, kn_ref, yab_ref, q_ref, gate_ref, kc_ref, vc_ref, ks_ref, vs_ref, kw_ref,
                          vw_ref, zc_ref, vrow_ref):
    h = _rms(x_ref[...], g_ref[...]).astype(BF16)
    z = jnp.dot(h, wm_ref[...], preferred_element_type=F32)
    gate_ref[...] = _sigmoid(jnp.dot(h, wcg_ref[...], preferred_element_type=F32))
    a_b, a_c, a_x = z[:, 0:256], z[:, 256:512], z[:, 512:768]
    zc = a_c * a_x
    cw = cw_ref[...]
    y_a = a_b * (cw[0:1] * st0_ref[...] + cw[1:2] * st1_ref[...] + cw[2:3] * zc)
    zc_ref[...] = zc
    u = _gelu_tanh(z[:, 768:1024])
    v = _rms(_gelu_tanh(z[:, 1024:1280]), gn_ref[...])
    vrow_ref[...] = v
    y_b = u * (ws0_ref[...] * v + bs0_ref[...])
    yab_ref[...] = jnp.concatenate([y_a, y_b], axis=1).astype(BF16)
    ones_bd = _head_group_ones(256)
    qn = qn_ref[...]
    q = jnp.concatenate([_head_rms(z[:, 1280:1536], qn, ones_bd),
                         _head_rms(z[:, 1536:1792], qn, ones_bd)], axis=1)
    q_ref[...] = q * (HEAD_DIM ** -0.5)
    kc_ref[...] = z[:, 1792:1920]
    vc_ref[...] = z[:, 1920:2048]
    vs_ref[...] = z[:, 2176:2304]
    vw_ref[...] = z[:, 2432:2560]
    kn = _head_rms(jnp.concatenate([z[:, 2048:2176], z[:, 2304:2432]], axis=1), kn_ref[...], ones_bd)
    ks_ref[...] = kn[:, 0:128]
    kw_ref[...] = kn[:, 128:256]


def _in_proj_sample(x, g, wm, wcg, cw, st0, st1, gn, ws0, bs0, qn, kn12):
    m = x.shape[0]
    ins = (x, g, wm, wcg, cw, st0, st1, gn, ws0, bs0, qn, kn12)
    sd = lambda w, dt=F32: jax.ShapeDtypeStruct((m, w), dt)
    out_shape = [sd(512, BF16), sd(512), sd(256), sd(128), sd(128), sd(128), sd(128), sd(128), sd(128),
                 sd(256), sd(256)]
    return pl.pallas_call(
        _inproj_sample_kernel,
        grid=(1,),
        in_specs=[_const_spec(a.shape) for a in ins],
        out_specs=[_const_spec(s.shape) for s in out_shape],
        out_shape=out_shape,
        compiler_params=_cparams("arbitrary"),
        name="in_proj_sample",
    )(*ins)


def _head_slopes():
    hrow = lax.broadcasted_iota(jnp.int32, (N_HEADS, 1), 0)
    return lax.bitcast_convert_type((126 - hrow) << 23, F32)


def _sample_cmp_kernel(pt_ref, *refs, n_half, t_pos):
    del pt_ref
    pp = PAGES_PER_STEP
    kpages, vpages = refs[0:pp], refs[pp:2 * pp]
    qz_ref, wk_ref, wv_ref, kn_ref = refs[2 * pp:2 * pp + 4]
    oc_ref, sel_ref = refs[2 * pp + 4:2 * pp + 6]
    p0k_ref, p1k_ref, p0v_ref, p1v_ref = refs[2 * pp + 6:]
    s = pl.program_id(1)
    halves = PAGE_SIZE // CMP_STRIDE
    wk, wv = wk_ref[...], wv_ref[...]
    for k in range(pp):
        off = pl.multiple_of((s * pp + k) * halves, halves)
        a0, a1 = _compress_rows(kpages[k], wk, halves)
        p0k_ref[pl.ds(off, halves), :] = a0
        p1k_ref[pl.ds(off, halves), :] = a1
        b0, b1 = _compress_rows(vpages[k], wv, halves)
        p0v_ref[pl.ds(off, halves), :] = b0
        p1v_ref[pl.ds(off, halves), :] = b1

    @pl.when(s == pl.num_programs(1) - 1)
    def _():
        kc = _head_rms(_combine_halves(p0k_ref[...], p1k_ref[...]), kn_ref[...], _head_group_ones(D_KV))
        vc = _combine_halves(p0v_ref[...], p1v_ref[...])
        qz = qz_ref[...].astype(BF16)
        s_c = lax.dot_general(qz, kc.astype(BF16), (((1,), (1,)), ((), ())), preferred_element_type=F32)
        cmp_end = lax.broadcasted_iota(jnp.int32, (1, n_half), 1) * CMP_STRIDE + (CMP_LEN - 1)
        d_c = t_pos - cmp_end
        vis = d_c >= 0
        s_c = jnp.where(vis, s_c - _head_slopes() * d_c.astype(F32), NEG)
        e_c = jnp.where(vis, jnp.exp(s_c - jnp.max(s_c, axis=1, keepdims=True)), 0.0)
        p_c = e_c * (1.0 / jnp.maximum(jnp.sum(e_c, axis=1, keepdims=True), 1e-30))
        oc_ref[...] = jnp.dot(p_c.astype(BF16), vc.astype(BF16), preferred_element_type=F32)

        hrow = lax.broadcasted_iota(jnp.int32, p_c.shape, 0)
        ps0 = jnp.sum(jnp.where(hrow < HEADS_PER_KV, p_c, 0.0), axis=0, keepdims=True)
        ps1 = jnp.sum(jnp.where(hrow >= HEADS_PER_KV, p_c, 0.0), axis=0, keepdims=True)
        psum = jnp.where(hrow == 0, ps0, jnp.where(hrow == 1, ps1, 0.0))
        n_blk_pad = sel_ref.shape[1]
        n_sel = t_pos // SEL_LEN + 1
        cidx = lax.broadcasted_iota(jnp.int32, (n_half, n_blk_pad), 0)
        bidx = lax.broadcasted_iota(jnp.int32, (n_half, n_blk_pad), 1)
        ratio = SEL_LEN // CMP_STRIDE
        band = jnp.where((cidx >= ratio * bidx - 1) & (cidx <= ratio * bidx + ratio - 1)
                         & (cidx < n_half - 1), 1.0, 0.0).astype(BF16)
        imp = _exact_dot01(psum, band)
        blk = lax.broadcasted_iota(jnp.int32, imp.shape, 1)
        cur = t_pos // SEL_LEN
        forced = (blk == 0) | (blk == cur) | (blk == cur - 1)
        future = blk * SEL_LEN > t_pos
        imp = jnp.where(forced, -NEG, imp)
        imp = jnp.where(future, NEG, imp)
        imp = jnp.where(blk < n_sel, imp, -jnp.inf)
        sel = _topk_select(imp, blk, min(TOP_N, n_sel)) & jnp.logical_not(future) & (blk < n_sel)
        sel_ref[...] = jnp.where(sel, 1.0, 0.0)


def _page_specs(layer, n):
    def spec(k):
        return pl.BlockSpec((None, None, PAGE_SIZE, D_KV),
                            lambda b, s, pt: (layer, pt[b, s * PAGES_PER_STEP + k], 0, 0))
    return [spec(k) for k in range(n)]


def _sample_cmp(page_table, cache_k, cache_v, layer, qz, wk, wv, kn0, t_pos):
    nb, n_pages = page_table.shape
    n_half = n_pages * PAGE_SIZE // CMP_STRIDE
    n_blk_pad = -(-(t_pos // SEL_LEN + 1) // LANES) * LANES
    pp = PAGES_PER_STEP
    per_b = lambda shape: pl.BlockSpec((None,) + shape, lambda b, s, pt: (b, 0, 0))
    const = lambda shape: pl.BlockSpec(shape, lambda b, s, pt: (0,) * len(shape))
    grid_spec = pltpu.PrefetchScalarGridSpec(
        num_scalar_prefetch=1,
        grid=(nb, n_pages // pp),
        in_specs=_page_specs(layer, pp) + _page_specs(layer, pp)
        + [per_b((N_HEADS, LANES)), const(wk.shape), const(wv.shape), const(kn0.shape)],
        out_specs=[per_b((N_HEADS, LANES)), per_b((SUBLANES, n_blk_pad))],
        scratch_shapes=[pltpu.VMEM((n_half, D_KV), F32)] * 4,
    )
    return pl.pallas_call(
        functools.partial(_sample_cmp_kernel, n_half=n_half, t_pos=t_pos),
        grid_spec=grid_spec,
        out_shape=[jax.ShapeDtypeStruct((nb, N_HEADS, LANES), F32),
                   jax.ShapeDtypeStruct((nb, SUBLANES, n_blk_pad), F32)],
        compiler_params=_cparams("arbitrary", "arbitrary"),
        name="sample_cmp",
    )(page_table, *([cache_k] * pp), *([cache_v] * pp), qz, wk, wv, kn0)


def _sample_sel_kernel(pt_ref, *refs, t_pos):
    del pt_ref
    pp = PAGES_PER_STEP
    kpages, vpages = refs[0:pp], refs[pp:2 * pp]
    qz_ref, sel_ref, kn_ref, vn_ref, o_ref, m_ref, l_ref, acc_ref = refs[2 * pp:]
    s = pl.program_id(1)
    n_keys = pp * PAGE_SIZE

    @pl.when(s == 0)
    def _():
        m_ref[...] = jnp.full(m_ref.shape, NEG, F32)
        l_ref[...] = jnp.zeros(l_ref.shape, F32)
        acc_ref[...] = jnp.zeros(acc_ref.shape, F32)

    qz = qz_ref[...].astype(BF16)
    kk = jnp.concatenate([r[...] for r in kpages], axis=0).astype(BF16)
    vv = jnp.concatenate([r[...] for r in vpages], axis=0).astype(BF16)
    sc = lax.dot_general(qz, kk, (((1,), (1,)), ((), ())), preferred_element_type=F32)
    kpos = s * n_keys + lax.broadcasted_iota(jnp.int32, (1, n_keys), 1)
    sc = sc - _head_slopes() * (t_pos - kpos).astype(F32)
    n_blk_pad = sel_ref.shape[1]
    hrow = lax.broadcasted_iota(jnp.int32, (N_HEADS, n_blk_pad), 0)
    selv = sel_ref[...]
    sel_h = jnp.where(hrow < HEADS_PER_KV, selv[0:1], selv[1:2]).astype(BF16)
    erow = lax.broadcasted_iota(jnp.int32, (n_blk_pad, n_keys), 0)
    ecol = lax.broadcasted_iota(jnp.int32, (n_blk_pad, n_keys), 1) // SEL_LEN + s * (n_keys // SEL_LEN)
    ok = jnp.dot(sel_h, jnp.where(erow == ecol, 1.0, 0.0).astype(BF16), preferred_element_type=F32) > 0.5
    sc = jnp.where(ok, sc, NEG)
    m_old = m_ref[...]
    m_new = jnp.maximum(m_old, jnp.max(sc, axis=1, keepdims=True))
    p = jnp.where(ok, jnp.exp(sc - m_new), 0.0)
    alpha = jnp.exp(m_old - m_new)
    l_ref[...] = alpha * l_ref[...] + jnp.sum(p, axis=1, keepdims=True)
    acc_ref[...] = alpha * acc_ref[...] + jnp.dot(p.astype(BF16), vv, preferred_element_type=F32)
    m_ref[...] = m_new

    @pl.when(s == pl.num_programs(1) - 1)
    def _():
        k_new = kn_ref[...].astype(BF16).astype(F32)
        s_new = jnp.sum(qz.astype(F32) * k_new, axis=1, keepdims=True)
        m_o = m_ref[...]
        m_n = jnp.maximum(m_o, s_new)
        a = jnp.exp(m_o - m_n)
        p_new = jnp.exp(s_new - m_n)
        l = a * l_ref[...] + p_new
        acc = a * acc_ref[...] + p_new.astype(BF16).astype(F32) * vn_ref[...].astype(BF16).astype(F32)
        o_ref[...] = acc * (1.0 / l)


def _sample_sel(page_table, cache_k, cache_v, layer, qz, sel, ks_new, vs_new, t_pos):
    nb, n_pages = page_table.shape
    pp = PAGES_PER_STEP
    per_b = lambda shape: pl.BlockSpec((None,) + shape, lambda b, s, pt: (b, 0, 0))
    grid_spec = pltpu.PrefetchScalarGridSpec(
        num_scalar_prefetch=1,
        grid=(nb, n_pages // pp),
        in_specs=_page_specs(layer, pp) + _page_specs(layer, pp)
        + [per_b((N_HEADS, LANES)), per_b(sel.shape[1:]), per_b((1, D_KV)), per_b((1, D_KV))],
        out_specs=per_b((N_HEADS, LANES)),
        scratch_shapes=[pltpu.VMEM((N_HEADS, 1), F32), pltpu.VMEM((N_HEADS, 1), F32),
                        pltpu.VMEM((N_HEADS, LANES), F32)],
    )
    return pl.pallas_call(
        functools.partial(_sample_sel_kernel, t_pos=t_pos),
        grid_spec=grid_spec,
        out_shape=jax.ShapeDtypeStruct((nb, N_HEADS, LANES), F32),
        compiler_params=_cparams("arbitrary", "arbitrary"),
        name="sample_sel",
    )(page_table, *([cache_k] * pp), *([cache_v] * pp), qz, sel, ks_new, vs_new)


def _sample_win_kernel(qz_ref, wk_ref, wv_ref, kn_ref, vn_ref, gate_ref, oc_ref, os_ref,
                       y_ref, wko_ref, wvo_ref):
    qz = qz_ref[...].astype(BF16)
    kwin, vwin = wk_ref[...], wv_ref[...]
    n_win = kwin.shape[0]
    sc = lax.dot_general(qz, kwin.astype(BF16), (((1,), (1,)), ((), ())), preferred_element_type=F32)
    dist = n_win - lax.broadcasted_iota(jnp.int32, (1, n_win), 1)
    sc = sc - _head_slopes() * dist.astype(F32)
    k_new, v_new = kn_ref[...], vn_ref[...]
    s_new = jnp.sum(qz.astype(F32) * k_new.astype(BF16).astype(F32), axis=1, keepdims=True)
    m = jnp.maximum(jnp.max(sc, axis=1, keepdims=True), s_new)
    p = jnp.exp(sc - m)
    p_new = jnp.exp(s_new - m)
    l = jnp.sum(p, axis=1, keepdims=True) + p_new
    acc = (jnp.dot(p.astype(BF16), vwin.astype(BF16), preferred_element_type=F32)
           + p_new.astype(BF16).astype(F32) * v_new.astype(BF16).astype(F32))
    o_w = acc * (1.0 / l)
    gate = gate_ref[...]
    y = gate[:, 0:1] * oc_ref[...] + gate[:, 1:2] * os_ref[...] + gate[:, 2:3] * o_w
    lane = lax.broadcasted_iota(jnp.int32, (1, LANES), 1)
    cols = []
    for k in range(N_HEADS // 2):
        grp = (2 * k) // HEADS_PER_KV
        even, odd = y[2 * k:2 * k + 1], y[2 * k + 1:2 * k + 2]
        low = even if grp == 0 else _swap_halves(even)
        high = odd if grp == 1 else _swap_halves(odd)
        cols.append(jnp.where(lane < HEAD_DIM, low, high))
    y_ref[...] = jnp.concatenate(cols, axis=1).astype(y_ref.dtype)
    row = lax.broadcasted_iota(jnp.int32, kwin.shape, 0)
    wko_ref[...] = jnp.where(row == n_win - 1, k_new, pltpu.roll(kwin, n_win - 1, axis=0))
    wvo_ref[...] = jnp.where(row == n_win - 1, v_new, pltpu.roll(vwin, n_win - 1, axis=0))


def _sample_win(qz, win_k, win_v, layer, kw_new, vw_new, gates_h, o_c, o_s):
    nb = qz.shape[0]
    n_win = win_k.shape[2]
    per_b = lambda shape: pl.BlockSpec((None,) + shape, lambda b: (b, 0, 0))
    cache = pl.BlockSpec((None, None, n_win, D_KV), lambda b: (layer, b, 0, 0))
    hl = (N_HEADS, LANES)
    return pl.pallas_call(
        _sample_win_kernel,
        grid=(nb,),
        in_specs=[per_b(hl), cache, cache, per_b((1, D_KV)), per_b((1, D_KV)), per_b(hl), per_b(hl), per_b(hl)],
        out_specs=[per_b((1, D_ATTN)), per_b((n_win, D_KV)), per_b((n_win, D_KV))],
        out_shape=[jax.ShapeDtypeStruct((nb, 1, D_ATTN), F32), jax.ShapeDtypeStruct((nb, n_win, D_KV), F32),
                   jax.ShapeDtypeStruct((nb, n_win, D_KV), F32)],
        compiler_params=_cparams("arbitrary"),
        name="sample_win",
    )(qz, win_k, win_v, kw_new, vw_new, gates_h, o_c, o_s)


def _slope_rows():
    out = np.zeros((N_KV_HEADS, HEADS_PER_KV * Q_BLOCK, LANES), np.float32)
    for g in range(N_KV_HEADS):
        for hp in range(HEADS_PER_KV):
            slope = 2.0 ** -(g * HEADS_PER_KV + hp + 1)
            out[g, hp * Q_BLOCK:(hp + 1) * Q_BLOCK, HEAD_DIM] = slope * 128.0
            out[g, hp * Q_BLOCK:(hp + 1) * Q_BLOCK, HEAD_DIM + 1] = slope
    return jnp.asarray(out)


def _tile_lanes(v, reps):
    return jnp.tile(v.reshape(1, -1), (1, reps))


def _heads_to_rows(q):
    n = q.shape[0]
    qh = q.reshape(n, N_HEADS, HEAD_DIM)
    z = jnp.zeros_like(qh[:, :HEADS_PER_KV])
    return jnp.concatenate([jnp.concatenate([qh[:, :HEADS_PER_KV], z], axis=-1),
                            jnp.concatenate([z, qh[:, HEADS_PER_KV:]], axis=-1)], axis=1)


def kernel(x_prompt, x_sample, cache_cmp_k, cache_cmp_v, cache_sel_k, cache_sel_v, cache_win_k, cache_win_v, state_conv, page_table, ffn1_norm, ffn1_w_gate, ffn1_w_up, ffn1_w_down, mix_norm, w_in, conv_w, gmlp_norm, gmlp_ws, gmlp_bs, q_norm, k_norm, cmp_wk, cmp_wv, w_branch, w_out, ffn2_norm, ffn2_w_gate, ffn2_w_up, ffn2_w_down):
    nb, seq, d = x_prompt.shape
    ns = x_sample.shape[0]
    depth = w_in.shape[0]
    n_pool = cache_cmp_k.shape[1]
    t_pos = page_table.shape[1] * PAGE_SIZE
    assert x_sample.shape[1] == 1 and cache_win_k.shape[2] == WINDOW
    assert seq % (CMP_STRIDE * LANES) == 0 and t_pos % (CMP_STRIDE * LANES) == 0
    tm = min(ROW_TILE, seq)

    xp = x_prompt.reshape(nb * seq, d)
    xs = x_sample.reshape(ns, d)
    paged = lambda c: c.reshape(depth, n_pool, PAGE_SIZE, D_KV)
    ck, cv, sk, sv = paged(cache_cmp_k), paged(cache_cmp_v), paged(cache_sel_k), paged(cache_sel_v)
    wink = cache_win_k.reshape(depth, ns, WINDOW, D_KV)
    winv = cache_win_v.reshape(depth, ns, WINDOW, D_KV)
    slope_rows = _slope_rows()

    prompt_new = [[] for _ in range(7)]
    sample_new = [[] for _ in range(8)]
    for l in range(depth):
        bf = lambda w: w.astype(BF16)
        w_main = bf(w_in[l, :, :MAIN_COLS])
        cg = w_in[l, :, MAIN_COLS:MAIN_COLS + N_BRANCH * N_HEADS].reshape(d, N_KV_HEADS, HEADS_PER_KV * N_BRANCH)
        w_cg = bf(jnp.pad(cg, ((0, 0), (0, 0), (0, LANES - HEADS_PER_KV * N_BRANCH))).reshape(d, N_KV_HEADS * LANES))
        w_mg = bf(w_in[l, :, MAIN_COLS + N_BRANCH * N_HEADS:])
        wb, wo = bf(w_branch[l]), bf(w_out[l])
        f1 = (ffn1_norm[l].reshape(1, d), bf(ffn1_w_gate[l]), bf(ffn1_w_up[l]), bf(ffn1_w_down[l]))
        f2 = (ffn2_norm[l].reshape(1, d), bf(ffn2_w_gate[l]), bf(ffn2_w_up[l]), bf(ffn2_w_down[l]))
        mn = mix_norm[l].reshape(1, d)
        cw = conv_w[l]
        gn = gmlp_norm[l].reshape(1, D_GMLP)
        gdim = D_GMLP // GMLP_GROUPS
        bs_tile = jnp.repeat(gmlp_bs[l].T, gdim, axis=1)
        ws0 = jnp.repeat(gmlp_ws[l, :, 0, 0], gdim).reshape(1, D_GMLP)
        bs0 = jnp.repeat(gmlp_bs[l, :, 0], gdim).reshape(1, D_GMLP)
        qn = _tile_lanes(q_norm[l], 4)
        kn0 = _tile_lanes(k_norm[l, 0], 2)
        kn12 = jnp.concatenate([_tile_lanes(k_norm[l, 1], 2), _tile_lanes(k_norm[l, 2], 2)], axis=1)
        wk = cmp_wk[l].reshape(CMP_LEN, D_KV)
        wv = cmp_wv[l].reshape(CMP_LEN, D_KV)

        xp = _half_ffn(xp, *f1, tm)
        (yab, q, gates, kc, vc, ks, vs, kww, vww, conv_new, kst, vse, kwt, vwe) = _in_proj_prompt(
            xp, nb, seq, mn, w_main, w_cg, cw, gn, gmlp_ws[l], bs_tile, qn, kn12, tm)
        kct, vce = _compress_prompt(kc, vc, nb, seq, wk, wv, kn0)
        yc = _attention_prompt(q, gates, slope_rows, kct, vce, kst, vse, kwt, vwe, nb, seq)
        xp = _merge(xp, yab, yc, mn, w_mg, wb, wo, tm)
        xp = _half_ffn(xp, *f2, tm)
        kv5 = lambda a: a.reshape(nb, -1, N_KV_HEADS, HEAD_DIM)
        for lst, a in zip(prompt_new, (kv5(kc), kv5(vc), kv5(ks), kv5(vs), kv5(kww), kv5(vww), conv_new)):
            lst.append(a)

        xs = _half_ffn(xs, *f1, ns)
        (yab_s, q_s, gates_s, kc_s, vc_s, ks_s, vs_s, kw_s, vw_s, zc_s, vrow_s) = _in_proj_sample(
            xs, mn, w_main, w_cg, cw, state_conv[l, :, 0], state_conv[l, :, 1], gn, ws0, bs0, qn, kn12)
        qz = _heads_to_rows(q_s)
        o_c, sel = _sample_cmp(page_table, ck, cv, l, qz, wk, wv, kn0, t_pos)
        o_s = _sample_sel(page_table, sk, sv, l, qz, sel, ks_s.reshape(ns, 1, D_KV), vs_s.reshape(ns, 1, D_KV), t_pos)
        gh = gates_s.reshape(ns, N_KV_HEADS, LANES)[:, :, :HEADS_PER_KV * N_BRANCH].reshape(ns, N_HEADS, N_BRANCH)
        gh = jnp.pad(gh, ((0, 0), (0, 0), (0, LANES - N_BRANCH)))
        yc_s, wk_new, wv_new = _sample_win(qz, wink, winv, l, kw_s.reshape(ns, 1, D_KV),
                                           vw_s.reshape(ns, 1, D_KV), gh, o_c, o_s)
        yc_s = yc_s.reshape(ns, D_ATTN).astype(BF16)
        xs = _merge(xs, yab_s, yc_s, mn, w_mg, wb, wo, ns)
        xs = _half_ffn(xs, *f2, ns)
        kv5s = lambda a: a.reshape(ns, -1, N_KV_HEADS, HEAD_DIM)
        conv_s = jnp.stack([state_conv[l, :, 1], zc_s], axis=1)
        for lst, a in zip(sample_new, (kv5s(kc_s), kv5s(vc_s), kv5s(ks_s), kv5s(vs_s), kv5s(wk_new), kv5s(wv_new),
                                       conv_s, vrow_s.reshape(ns, 1, D_GMLP))):
            lst.append(a)

    outs_p = [jnp.stack(a) for a in prompt_new]
    outs_s = [jnp.stack(a) for a in sample_new]
    return (xp.reshape(nb, seq, d), xs.reshape(ns, 1, d), *outs_p, *outs_s)
```

```python
import functools

import numpy as np
import jax
import jax.numpy as jnp
from jax import lax
from jax.experimental import pallas as pl
from jax.experimental.pallas import tpu as pltpu

F32 = jnp.float32
BF16 = jnp.bfloat16

HEAD_DIM = 64
N_HEADS = 8
N_KV_HEADS = 2
HEADS_PER_KV = N_HEADS // N_KV_HEADS
D_CONV = 256
CONV_W = 3
D_GMLP = 256
GMLP_GROUPS = 4
CHUNK = 128
D_ATTN = N_HEADS * HEAD_DIM
D_KV = N_KV_HEADS * HEAD_DIM
CMP_LEN = 32
CMP_STRIDE = 16
SEL_LEN = 64
TOP_N = 16
WINDOW = 512
Q_BLOCK = 128
N_BRANCH = 3
PAGE_SIZE = 128
EPS = 1e-6
NEG = -1e30
MASK_BIG = 2.0 ** 100
MAIN_COLS = 5 * 256 + D_ATTN + 6 * D_KV

LANES = 128
SUBLANES = 8
ROW_TILE = 512
FF_CHUNK = 256
KEY_CHUNK = 512
PAGES_PER_STEP = 8
VMEM_LIMIT = 56 * 1024 * 1024


def _cparams(*sem):
    return pltpu.CompilerParams(dimension_semantics=sem, vmem_limit_bytes=VMEM_LIMIT)


def _const_spec(shape):
    nd = len(shape)
    return pl.BlockSpec(shape, lambda *_: (0,) * nd, pipeline_mode=pl.Buffered(1))


def _rms(x, g):
    return x * lax.rsqrt(jnp.mean(x * x, axis=-1, keepdims=True) + EPS) * g


def _sigmoid(x):
    return 1.0 / (1.0 + jnp.exp(-x))


def _gelu_tanh(x):
    return 0.5 * x * (1.0 + jnp.tanh(0.7978845608028654 * (x + 0.044715 * (x * x * x))))


def _bdot(a, b):
    return jnp.dot(a.astype(BF16), b.astype(BF16), preferred_element_type=F32)


def _split3(x):
    hi = x.astype(BF16)
    r = x - hi.astype(F32)
    mid = r.astype(BF16)
    lo = (r - mid.astype(F32)).astype(BF16)
    return hi, mid, lo


def _exact_dot01(x, m01):
    hi, mid, lo = _split3(x)
    return (jnp.dot(hi, m01, preferred_element_type=F32) + jnp.dot(mid, m01, preferred_element_type=F32)
            + jnp.dot(lo, m01, preferred_element_type=F32))


def _head_group_ones(n):
    r = lax.broadcasted_iota(jnp.int32, (n, n), 0) // HEAD_DIM
    c = lax.broadcasted_iota(jnp.int32, (n, n), 1) // HEAD_DIM
    return jnp.where(r == c, 1.0, 0.0).astype(BF16)


def _head_rms(x, g, ones_bd):
    ssq = _exact_dot01(x * x, ones_bd)
    return x * lax.rsqrt(ssq * (1.0 / HEAD_DIM) + EPS) * g


def _swap_halves(x):
    return pltpu.roll(x, HEAD_DIM, axis=1)


def _value_ext(v, grp):
    lane = lax.broadcasted_iota(jnp.int32, v.shape, 1)
    src = v if grp == 0 else _swap_halves(v)
    return jnp.where(lane < HEAD_DIM, src, 1.0).astype(BF16)


def _pos_rows(pos):
    n = pos.shape[1]
    row = lax.broadcasted_iota(jnp.int32, (HEAD_DIM, n), 0)
    hi = (pos >> 7).astype(F32)
    lo = (pos & 127).astype(F32)
    return jnp.where(row == 0, hi, jnp.where(row == 1, lo, 0.0)).astype(BF16)


def _n_blk_pad(seq):
    return -(-(seq // SEL_LEN) // LANES) * LANES


def _topk_select_cols(imp, blk_f, n_iter):
    sel = jnp.zeros(imp.shape, dtype=jnp.bool_)
    for _ in range(n_iter):
        m = jnp.max(imp, axis=0, keepdims=True)
        idx = jnp.min(jnp.where(imp == m, blk_f, float(imp.shape[0])), axis=0, keepdims=True)
        pick = blk_f == idx
        sel = jnp.logical_or(sel, pick)
        imp = jnp.where(pick, -jnp.inf, imp)
    return sel


def _topk_select(imp, blk, n_iter):
    blk_f = blk.astype(F32)
    sel = jnp.zeros(imp.shape, dtype=jnp.bool_)
    for _ in range(n_iter):
        m = jnp.max(imp, axis=1, keepdims=True)
        idx = jnp.min(jnp.where(imp == m, blk_f, float(imp.shape[1])), axis=1, keepdims=True)
        pick = blk_f == idx
        sel = jnp.logical_or(sel, pick)
        imp = jnp.where(pick, -jnp.inf, imp)
    return sel


def _ffn_kernel(x_ref, g_ref, wg_ref, wu_ref, wd_ref, o_ref, acc_ref):
    x = x_ref[...]
    h = _rms(x, g_ref[...]).astype(BF16)
    d_ff = wg_ref.shape[1]
    for c in range(d_ff // FF_CHUNK):
        sl = slice(c * FF_CHUNK, (c + 1) * FF_CHUNK)
        gate = jnp.dot(h, wg_ref[:, sl], preferred_element_type=F32)
        up = jnp.dot(h, wu_ref[:, sl], preferred_element_type=F32)
        a = (gate * _sigmoid(gate) * up).astype(BF16)
        part = jnp.dot(a, wd_ref[sl, :], preferred_element_type=F32)
        if c == 0:
            acc_ref[...] = part
        else:
            acc_ref[...] += part
    o_ref[...] = x + 0.5 * acc_ref[...]


def _half_ffn(x, g, wg, wu, wd, tm):
    m, d = x.shape
    d_ff = wg.shape[1]
    return pl.pallas_call(
        _ffn_kernel,
        grid=(m // tm,),
        in_specs=[pl.BlockSpec((tm, d), lambda i: (i, 0)), _const_spec((1, d)),
                  _const_spec((d, d_ff)), _const_spec((d, d_ff)), _const_spec((d_ff, d))],
        out_specs=pl.BlockSpec((tm, d), lambda i: (i, 0)),
        out_shape=jax.ShapeDtypeStruct((m, d), F32),
        scratch_shapes=[pltpu.VMEM((tm, d), F32)],
        compiler_params=_cparams("arbitrary"),
        name="half_ffn",
    )(x, g, wg, wu, wd)


def _inproj_kernel(x_ref, g_ref, wm_ref, wcg_ref, cw_ref, gn_ref, ws_ref, bs_ref, qn_ref, kn_ref,
                   yab_ref, q_ref, gate_ref, kc_ref, vc_ref, ks_ref, vs_ref, kww_ref, vww_ref, conv_ref,
                   kst_ref, vse_ref, kwt_ref, vwe_ref, zbuf_ref, *, tm, tiles_per_seq):
    j = pl.program_id(0) % tiles_per_seq
    h = _rms(x_ref[...], g_ref[...]).astype(BF16)
    z = jnp.dot(h, wm_ref[...], preferred_element_type=F32)
    gate_ref[...] = _sigmoid(jnp.dot(h, wcg_ref[...], preferred_element_type=F32))

    a_b, a_c, a_x = z[:, 0:256], z[:, 256:512], z[:, 512:768]
    zc = a_c * a_x

    @pl.when(j == 0)
    def _():
        zbuf_ref[0:SUBLANES, :] = jnp.zeros((SUBLANES, D_CONV), F32)

    zbuf_ref[SUBLANES:SUBLANES + tm, :] = zc
    z1 = zbuf_ref[pl.ds(SUBLANES - 1, tm), :]
    z2 = zbuf_ref[pl.ds(SUBLANES - 2, tm), :]
    cw = cw_ref[...]
    y_a = a_b * (cw[0:1] * z2 + cw[1:2] * z1 + cw[2:3] * zc)
    tail = zbuf_ref[tm:tm + SUBLANES, :]
    zbuf_ref[0:SUBLANES, :] = tail
    conv_ref[...] = tail[SUBLANES - (CONV_W - 1):, :]

    u = _gelu_tanh(z[:, 768:1024])
    v = _rms(_gelu_tanh(z[:, 1024:1280]), gn_ref[...]).astype(BF16)
    tri = (lax.broadcasted_iota(jnp.int32, (CHUNK, CHUNK), 0)
           >= lax.broadcasted_iota(jnp.int32, (CHUNK, CHUNK), 1))
    wt = [jnp.where(tri, ws_ref[gi], 0.0).astype(BF16) for gi in range(GMLP_GROUPS)]
    lane_grp = lax.broadcasted_iota(jnp.int32, (CHUNK, D_GMLP), 1) // (D_GMLP // GMLP_GROUPS)
    bias = bs_ref[...]
    yb = []
    for ci in range(tm // CHUNK):
        vch = v[ci * CHUNK:(ci + 1) * CHUNK]
        s = bias
        for gi in range(GMLP_GROUPS):
            s = s + jnp.where(lane_grp == gi, jnp.dot(wt[gi], vch, preferred_element_type=F32), 0.0)
        yb.append(u[ci * CHUNK:(ci + 1) * CHUNK] * s)
    y_b = jnp.concatenate(yb, axis=0)
    yab_ref[...] = jnp.concatenate([y_a, y_b], axis=1).astype(BF16)

    ones_bd = _head_group_ones(256)
    qn = qn_ref[...]
    scale = HEAD_DIM ** -0.5
    q = jnp.concatenate([_head_rms(z[:, 1280:1536], qn, ones_bd),
                         _head_rms(z[:, 1536:1792], qn, ones_bd)], axis=1)
    q_ref[...] = (q * scale).astype(BF16)
    kc_ref[...] = z[:, 1792:1920]
    vc_ref[...] = z[:, 1920:2048]
    vs = z[:, 2176:2304]
    vw = z[:, 2432:2560]
    kn = _head_rms(jnp.concatenate([z[:, 2048:2176], z[:, 2304:2432]], axis=1), kn_ref[...], ones_bd)
    ks, kw = kn[:, 0:128], kn[:, 128:256]
    ks_ref[...] = ks
    vs_ref[...] = vs

    if tm >= WINDOW:
        kww_ref[...] = kw[tm - WINDOW:, :]
        vww_ref[...] = vw[tm - WINDOW:, :]
    else:
        first = tiles_per_seq - WINDOW // tm

        @pl.when(j >= first)
        def _():
            off = pl.multiple_of((j - first) * tm, tm)
            kww_ref[pl.ds(off, tm), :] = kw
            vww_ref[pl.ds(off, tm), :] = vw

    pos = j * tm + lax.broadcasted_iota(jnp.int32, (1, tm), 1)
    prow = _pos_rows(pos)
    kst = ks.T.astype(BF16)
    kwt = kw.T.astype(BF16)
    n_blk_pad = kst_ref.shape[1] - 2 * HEAD_DIM
    blk_row = lax.broadcasted_iota(jnp.int32, (n_blk_pad, tm), 0)
    erows = jnp.where(blk_row == (pos >> 6), 1.0, 0.0).astype(BF16)
    for grp in range(N_KV_HEADS):
        sl = slice(grp * HEAD_DIM, (grp + 1) * HEAD_DIM)
        kst_ref[grp] = jnp.concatenate([erows, kst[sl], prow], axis=0)
        kwt_ref[grp] = jnp.concatenate([kwt[sl], prow], axis=0)
        vse_ref[grp] = _value_ext(vs, grp)
        vwe_ref[grp] = _value_ext(vw, grp)


def _in_proj_prompt(x, nb, seq, g, wm, wcg, cw, gn, ws, bs_tile, qn, kn12, tm):
    m, d = x.shape
    tps = seq // tm
    row = lambda i: (i, 0)
    rows = lambda w, dt: (pl.BlockSpec((tm, w), row), jax.ShapeDtypeStruct((m, w), dt))
    win = (pl.BlockSpec((None, WINDOW, D_KV), lambda i: (i // tps, 0, 0)),
           jax.ShapeDtypeStruct((nb, WINDOW, D_KV), F32))
    kt = (pl.BlockSpec((None, N_KV_HEADS, 2 * HEAD_DIM, tm), lambda i: (i // tps, 0, 0, i % tps)),
          jax.ShapeDtypeStruct((nb, N_KV_HEADS, 2 * HEAD_DIM, seq), BF16))
    ve = (pl.BlockSpec((None, N_KV_HEADS, tm, LANES), lambda i: (i // tps, 0, i % tps, 0)),
          jax.ShapeDtypeStruct((nb, N_KV_HEADS, seq, LANES), BF16))
    conv = (pl.BlockSpec((None, CONV_W - 1, D_CONV), lambda i: (i // tps, 0, 0)),
            jax.ShapeDtypeStruct((nb, CONV_W - 1, D_CONV), F32))
    n_krows = _n_blk_pad(seq) + 2 * HEAD_DIM
    kt_sel = (pl.BlockSpec((None, N_KV_HEADS, n_krows, tm), lambda i: (i // tps, 0, 0, i % tps)),
              jax.ShapeDtypeStruct((nb, N_KV_HEADS, n_krows, seq), BF16))
    outs = [rows(512, BF16), rows(512, BF16), rows(256, F32), rows(128, F32), rows(128, F32),
            rows(128, F32), rows(128, F32), win, win, conv, kt_sel, ve, kt, ve]
    return pl.pallas_call(
        functools.partial(_inproj_kernel, tm=tm, tiles_per_seq=tps),
        grid=(m // tm,),
        in_specs=[pl.BlockSpec((tm, d), row), _const_spec((1, d)), _const_spec(wm.shape),
                  _const_spec(wcg.shape), _const_spec(cw.shape), _const_spec(gn.shape),
                  _const_spec(ws.shape), _const_spec(bs_tile.shape), _const_spec(qn.shape),
                  _const_spec(kn12.shape)],
        out_specs=[o[0] for o in outs],
        out_shape=[o[1] for o in outs],
        scratch_shapes=[pltpu.VMEM((tm + SUBLANES, D_CONV), F32)],
        compiler_params=_cparams("arbitrary"),
        name="in_proj_prompt",
    )(x, g, wm, wcg, cw, gn, ws, bs_tile, qn, kn12)


def _compress_rows(src_ref, w, n_half):
    p0 = jnp.zeros((n_half, D_KV), F32)
    p1 = jnp.zeros((n_half, D_KV), F32)
    for s in range(CMP_STRIDE):
        xs = src_ref[pl.ds(s, n_half, stride=CMP_STRIDE), :]
        p0 = p0 + xs * w[s:s + 1]
        p1 = p1 + xs * w[CMP_STRIDE + s:CMP_STRIDE + s + 1]
    return p0, p1


def _combine_halves(p0, p1):
    n = p0.shape[0]
    row = lax.broadcasted_iota(jnp.int32, p0.shape, 0)
    return jnp.where(row < n - 1, p0 + pltpu.roll(p1, n - 1, axis=0), 0.0)


def _compress_kernel(kc_ref, vc_ref, wk_ref, wv_ref, kn_ref, kct_ref, vce_ref, *, n_half):
    kc = _combine_halves(*_compress_rows(kc_ref, wk_ref[...], n_half))
    vc = _combine_halves(*_compress_rows(vc_ref, wv_ref[...], n_half))
    kc = _head_rms(kc, kn_ref[...], _head_group_ones(D_KV))
    kct = kc.T.astype(BF16)
    cmp_end = lax.broadcasted_iota(jnp.int32, (1, n_half), 1) * CMP_STRIDE + (CMP_LEN - 1)
    prow = _pos_rows(cmp_end)
    for grp in range(N_KV_HEADS):
        kct_ref[grp] = jnp.concatenate([kct[grp * HEAD_DIM:(grp + 1) * HEAD_DIM], prow], axis=0)
        vce_ref[grp] = _value_ext(vc, grp)


def _compress_prompt(kc, vc, nb, seq, wk, wv, kn0):
    n_half = seq // CMP_STRIDE
    return pl.pallas_call(
        functools.partial(_compress_kernel, n_half=n_half),
        grid=(nb,),
        in_specs=[pl.BlockSpec((seq, D_KV), lambda b: (b, 0)), pl.BlockSpec((seq, D_KV), lambda b: (b, 0)),
                  _const_spec(wk.shape), _const_spec(wv.shape), _const_spec(kn0.shape)],
        out_specs=[pl.BlockSpec((None, N_KV_HEADS, 2 * HEAD_DIM, n_half), lambda b: (b, 0, 0, 0)),
                   pl.BlockSpec((None, N_KV_HEADS, n_half, LANES), lambda b: (b, 0, 0, 0))],
        out_shape=[jax.ShapeDtypeStruct((nb, N_KV_HEADS, 2 * HEAD_DIM, n_half), BF16),
                   jax.ShapeDtypeStruct((nb, N_KV_HEADS, n_half, LANES), BF16)],
        compiler_params=_cparams("arbitrary"),
        name="compress_prompt",
    )(kc, vc, wk, wv, kn0)


def _attn_kernel(q_ref, gate_ref, slope_ref, kct_ref, vce_ref, kst_ref, vse_ref, kwt_ref, vwe_ref, o_ref,
                 qx_ref, m_ref, acc_ref, oc_ref, ow_ref, flag_ref, *, n_cmp_pad, n_blk_pad):
    qb = Q_BLOCK
    rows = HEADS_PER_KV * qb
    i = pl.program_id(2)
    p0 = i * qb

    q = q_ref[...].astype(F32)
    lane = lax.broadcasted_iota(jnp.int32, (qb, LANES), 1)
    parts = []
    for hp in range(HEADS_PER_KV):
        col = q[:, (hp // 2) * LANES:(hp // 2 + 1) * LANES]
        if hp % 2 == 1:
            col = _swap_halves(col)
        parts.append(jnp.where(lane < HEAD_DIM, col, 0.0))
    qx = (jnp.concatenate(parts, axis=0) + slope_ref[...]).astype(BF16)

    t_q = p0 + lax.broadcasted_iota(jnp.int32, (qb, 1), 0)
    t_rows = jnp.concatenate([t_q] * HEADS_PER_KV, axis=0)

    n_win = WINDOW + qb
    wstart = pl.multiple_of(jnp.maximum(p0 - WINDOW, 0), qb)
    s_w = jnp.dot(qx, kwt_ref[:, pl.ds(wstart, n_win)], preferred_element_type=F32)
    d_w = t_rows - (wstart + lax.broadcasted_iota(jnp.int32, (1, n_win), 1))
    ok_w = (d_w >= 0) & (d_w <= WINDOW)
    s_w = jnp.where(ok_w, s_w, NEG)
    e_w = jnp.where(ok_w, jnp.exp(s_w - jnp.max(s_w, axis=1, keepdims=True)), 0.0)
    acc_w = jnp.dot(e_w.astype(BF16), vwe_ref[pl.ds(wstart, n_win), :], preferred_element_type=F32)
    ow_ref[...] = acc_w * (1.0 / _swap_halves(acc_w))

    s_c = jnp.dot(qx, kct_ref[...], preferred_element_type=F32)
    cmp_end = lax.broadcasted_iota(jnp.int32, (1, n_cmp_pad), 1) * CMP_STRIDE + (CMP_LEN - 1)
    vis = cmp_end <= t_rows
    s_c = jnp.where(vis, s_c, NEG)
    e_c = jnp.where(vis, jnp.exp(s_c - jnp.max(s_c, axis=1, keepdims=True)), 0.0)
    l_c = jnp.sum(e_c, axis=1, keepdims=True)
    p_c = e_c * (1.0 / jnp.maximum(l_c, 1e-30))
    oc_ref[...] = jnp.dot(p_c.astype(BF16), vce_ref[...], preferred_element_type=F32)

    psum = p_c[0:qb]
    for hp in range(1, HEADS_PER_KV):
        psum = psum + p_c[hp * qb:(hp + 1) * qb]
    bidx = lax.broadcasted_iota(jnp.int32, (n_blk_pad, n_cmp_pad), 0)
    cidx = lax.broadcasted_iota(jnp.int32, (n_blk_pad, n_cmp_pad), 1)
    ratio = SEL_LEN // CMP_STRIDE
    band_t = jnp.where((cidx >= ratio * bidx - 1) & (cidx <= ratio * bidx + ratio - 1)
                       & (cidx < n_cmp_pad - 1), 1.0, 0.0).astype(BF16)
    nt = (((1,), (1,)), ((), ()))
    imp = sum(lax.dot_general(band_t, part, nt, preferred_element_type=F32)
              for part in _split3(psum))
    blk = lax.broadcasted_iota(jnp.int32, (n_blk_pad, qb), 0)
    t_lane = p0 + lax.broadcasted_iota(jnp.int32, (1, qb), 1)
    cur = t_lane >> 6
    forced = (blk == 0) | (blk == cur) | (blk == cur - 1)
    future = blk * SEL_LEN > t_lane
    imp = jnp.where(forced, -NEG, imp)
    imp = jnp.where(future, NEG, imp)
    sel_t = _topk_select_cols(imp, blk.astype(F32), TOP_N) & jnp.logical_not(future)
    blocks_per_chunk = KEY_CHUNK // SEL_LEN
    sel_f = jnp.where(sel_t, 1.0, 0.0)
    for c in range(flag_ref.shape[0]):
        used = jnp.max(sel_f[c * blocks_per_chunk:(c + 1) * blocks_per_chunk])
        flag_ref[c] = (used > 0.0).astype(jnp.int32)
    selneg = jnp.where(sel_t, 0.0, -MASK_BIG).T.astype(BF16)
    qx_ref[:, 0:n_blk_pad] = jnp.concatenate([selneg] * HEADS_PER_KV, axis=0)
    qx_ref[:, n_blk_pad:] = qx

    m_ref[...] = jnp.full((rows, 1), NEG, F32)
    acc_ref[...] = jnp.zeros((rows, LANES), F32)

    def chunk_step(c, causal):
        start = pl.multiple_of(c * KEY_CHUNK, KEY_CHUNK)
        s = jnp.dot(qx_ref[...], kst_ref[:, pl.ds(start, KEY_CHUNK)], preferred_element_type=F32)
        if causal:
            kpos = start + lax.broadcasted_iota(jnp.int32, (1, KEY_CHUNK), 1)
            s = jnp.where(kpos <= t_rows, s, NEG)
        m_old = m_ref[...]
        m_new = jnp.maximum(m_old, jnp.max(s, axis=1, keepdims=True))
        p = jnp.exp(s - m_new).astype(BF16)
        acc_ref[...] = (jnp.exp(m_old - m_new) * acc_ref[...]
                        + jnp.dot(p, vse_ref[pl.ds(start, KEY_CHUNK), :], preferred_element_type=F32))
        m_ref[...] = m_new

    last = p0 // KEY_CHUNK

    def body(c, carry):
        @pl.when(flag_ref[c] > 0)
        def _():
            chunk_step(c, False)
        return carry

    lax.fori_loop(0, last, body, 0)
    chunk_step(last, True)
    acc_s = acc_ref[...]

    o_s = acc_s * (1.0 / _swap_halves(acc_s))
    o_c = oc_ref[...]
    o_w = ow_ref[...]
    gate = gate_ref[...]
    res = []
    for hp in range(HEADS_PER_KV):
        sl = slice(hp * qb, (hp + 1) * qb)
        gc = gate[:, N_BRANCH * hp + 0:N_BRANCH * hp + 1]
        gs = gate[:, N_BRANCH * hp + 1:N_BRANCH * hp + 2]
        gw = gate[:, N_BRANCH * hp + 2:N_BRANCH * hp + 3]
        res.append(gc * o_c[sl] + gs * o_s[sl] + gw * o_w[sl])
    cols = [jnp.where(lane < HEAD_DIM, res[2 * k], _swap_halves(res[2 * k + 1])) for k in range(2)]
    o_ref[...] = jnp.concatenate(cols, axis=1).astype(o_ref.dtype)


def _attention_prompt(q, gates, slope_rows, kct, vce, kst, vse, kwt, vwe, nb, seq):
    nq = seq // Q_BLOCK
    n_cmp_pad = kct.shape[-1]
    n_blk_pad = _n_blk_pad(seq)
    rows = HEADS_PER_KV * Q_BLOCK
    qspec = pl.BlockSpec((Q_BLOCK, HEADS_PER_KV * HEAD_DIM), lambda b, g, i: (b * nq + i, g))
    per_bg = lambda shape: pl.BlockSpec((None, None) + shape, lambda b, g, i: (b, g, 0, 0))
    return pl.pallas_call(
        functools.partial(_attn_kernel, n_cmp_pad=n_cmp_pad, n_blk_pad=n_blk_pad),
        grid=(nb, N_KV_HEADS, nq),
        in_specs=[qspec,
                  pl.BlockSpec((Q_BLOCK, LANES), lambda b, g, i: (b * nq + i, g)),
                  pl.BlockSpec((None, rows, LANES), lambda b, g, i: (g, 0, 0)),
                  per_bg((2 * HEAD_DIM, n_cmp_pad)), per_bg((n_cmp_pad, LANES)),
                  per_bg((n_blk_pad + 2 * HEAD_DIM, seq)), per_bg((seq, LANES)),
                  per_bg((2 * HEAD_DIM, seq)), per_bg((seq, LANES))],
        out_specs=qspec,
        out_shape=jax.ShapeDtypeStruct(q.shape, BF16),
        scratch_shapes=[pltpu.VMEM((rows, n_blk_pad + LANES), BF16), pltpu.VMEM((rows, 1), F32),
                        pltpu.VMEM((rows, LANES), F32), pltpu.VMEM((rows, LANES), F32),
                        pltpu.VMEM((rows, LANES), F32), pltpu.SMEM((seq // KEY_CHUNK,), jnp.int32)],
        compiler_params=_cparams("arbitrary", "arbitrary", "arbitrary"),
        name="attention_prompt",
    )(q, gates, slope_rows, kct, vce, kst, vse, kwt, vwe)


def _merge_kernel(x_ref, yab_ref, yc_ref, g_ref, wmg_ref, wb_ref, wo_ref, o_ref):
    x = x_ref[...]
    d = x.shape[1]
    h = _rms(x, g_ref[...]).astype(BF16)
    yab = yab_ref[...]
    branches = (jnp.dot(yab[:, 0:D_CONV], wb_ref[0:D_CONV, :], preferred_element_type=F32),
                jnp.dot(yab[:, D_CONV:], wb_ref[D_CONV:D_CONV + D_GMLP, :], preferred_element_type=F32),
                jnp.dot(yc_ref[...], wb_ref[D_CONV + D_GMLP:, :], preferred_element_type=F32))
    merged = None
    for k, y in enumerate(branches):
        gk = _sigmoid(jnp.dot(h, wmg_ref[:, k * d:(k + 1) * d], preferred_element_type=F32))
        merged = gk * y if merged is None else merged + gk * y
    o_ref[...] = x + jnp.dot(merged.astype(BF16), wo_ref[...], preferred_element_type=F32)


def _merge(x, yab, yc, g, wmg, wb, wo, tm):
    m, d = x.shape
    row = lambda i: (i, 0)
    return pl.pallas_call(
        _merge_kernel,
        grid=(m // tm,),
        in_specs=[pl.BlockSpec((tm, d), row), pl.BlockSpec((tm, yab.shape[1]), row),
                  pl.BlockSpec((tm, yc.shape[1]), row), _const_spec((1, d)), _const_spec(wmg.shape),
                  _const_spec(wb.shape), _const_spec(wo.shape)],
        out_specs=pl.BlockSpec((tm, d), row),
        out_shape=jax.ShapeDtypeStruct((m, d), F32),
        compiler_params=_cparams("arbitrary"),
        name="merge_out",
    )(x, yab, yc, g, wmg, wb, wo)


def _inproj_sample_kernel(x_ref, g_ref, wm_ref, wcg_ref, cw_ref, st0_ref, st1_ref, gn_ref, ws0_ref, bs0_ref,
                          qn_ref, kn_ref, yab_ref, q_ref, gate_ref, kc_ref, vc_ref, ks_ref, vs_ref, kw_ref,
                          vw_ref, zc_ref, vrow_ref):
    h = _rms(x_ref[...], g_ref[...]).astype(BF16)
    z = jnp.dot(h, wm_ref[...], preferred_element_type=F32)
    gate_ref[...] = _sigmoid(jnp.dot(h, wcg_ref[...], preferred_element_type=F32))
    a_b, a_c, a_x = z[:, 0:256], z[:, 256:512], z[:, 512:768]
    zc = a_c * a_x
    cw = cw_ref[...]
    y_a = a_b * (cw[0:1] * st0_ref[...] + cw[1:2] * st1_ref[...] + cw[2:3] * zc)
    zc_ref[...] = zc
    u = _gelu_tanh(z[:, 768:1024])
    v = _rms(_gelu_tanh(z[:, 1024:1280]), gn_ref[...])
    vrow_ref[...] = v
    y_b = u * (ws0_ref[...] * v + bs0_ref[...])
    yab_ref[...] = jnp.concatenate([y_a, y_b], axis=1).astype(BF16)
    ones_bd = _head_group_ones(256)
    qn = qn_ref[...]
    q = jnp.concatenate([_head_rms(z[:, 1280:1536], qn, ones_bd),
                         _head_rms(z[:, 1536:1792], qn, ones_bd)], axis=1)
    q_ref[...] = q * (HEAD_DIM ** -0.5)
    kc_ref[...] = z[:, 1792:1920]
    vc_ref[...] = z[:, 1920:2048]
    vs_ref[...] = z[:, 2176:2304]
    vw_ref[...] = z[:, 2432:2560]
    kn = _head_rms(jnp.concatenate([z[:, 2048:2176], z[:, 2304:2432]], axis=1), kn_ref[...], ones_bd)
    ks_ref[...] = kn[:, 0:128]
    kw_ref[...] = kn[:, 128:256]


def _in_proj_sample(x, g, wm, wcg, cw, st0, st1, gn, ws0, bs0, qn, kn12):
    m = x.shape[0]
    ins = (x, g, wm, wcg, cw, st0, st1, gn, ws0, bs0, qn, kn12)
    sd = lambda w, dt=F32: jax.ShapeDtypeStruct((m, w), dt)
    out_shape = [sd(512, BF16), sd(512), sd(256), sd(128), sd(128), sd(128), sd(128), sd(128), sd(128),
                 sd(256), sd(256)]
    return pl.pallas_call(
        _inproj_sample_kernel,
        grid=(1,),
        in_specs=[_const_spec(a.shape) for a in ins],
        out_specs=[_const_spec(s.shape) for s in out_shape],
        out_shape=out_shape,
        compiler_params=_cparams("arbitrary"),
        name="in_proj_sample",
    )(*ins)


def _head_slopes():
    hrow = lax.broadcasted_iota(jnp.int32, (N_HEADS, 1), 0)
    return lax.bitcast_convert_type((126 - hrow) << 23, F32)


def _sample_cmp_kernel(pt_ref, *refs, n_half, t_pos):
    del pt_ref
    pp = PAGES_PER_STEP
    kpages, vpages = refs[0:pp], refs[pp:2 * pp]
    qz_ref, wk_ref, wv_ref, kn_ref = refs[2 * pp:2 * pp + 4]
    oc_ref, sel_ref = refs[2 * pp + 4:2 * pp + 6]
    p0k_ref, p1k_ref, p0v_ref, p1v_ref, rows_ref = refs[2 * pp + 6:]
    s = pl.program_id(1)
    halves = PAGE_SIZE // CMP_STRIDE
    wk, wv = wk_ref[...], wv_ref[...]
    for k in range(pp):
        rows_ref[2 * k] = kpages[k][...].T
        rows_ref[2 * k + 1] = vpages[k][...].T
        off = pl.multiple_of((s * pp + k) * halves, halves)
        a0, a1 = _compress_rows(rows_ref.at[2 * k], wk, halves)
        p0k_ref[pl.ds(off, halves), :] = a0
        p1k_ref[pl.ds(off, halves), :] = a1
        b0, b1 = _compress_rows(rows_ref.at[2 * k + 1], wv, halves)
        p0v_ref[pl.ds(off, halves), :] = b0
        p1v_ref[pl.ds(off, halves), :] = b1

    @pl.when(s == pl.num_programs(1) - 1)
    def _():
        kc = _head_rms(_combine_halves(p0k_ref[...], p1k_ref[...]), kn_ref[...], _head_group_ones(D_KV))
        vc = _combine_halves(p0v_ref[...], p1v_ref[...])
        qz = qz_ref[...].astype(BF16)
        s_c = lax.dot_general(qz, kc.astype(BF16), (((1,), (1,)), ((), ())), preferred_element_type=F32)
        cmp_end = lax.broadcasted_iota(jnp.int32, (1, n_half), 1) * CMP_STRIDE + (CMP_LEN - 1)
        d_c = t_pos - cmp_end
        vis = d_c >= 0
        s_c = jnp.where(vis, s_c - _head_slopes() * d_c.astype(F32), NEG)
        e_c = jnp.where(vis, jnp.exp(s_c - jnp.max(s_c, axis=1, keepdims=True)), 0.0)
        p_c = e_c * (1.0 / jnp.maximum(jnp.sum(e_c, axis=1, keepdims=True), 1e-30))
        oc_ref[...] = jnp.dot(p_c.astype(BF16), vc.astype(BF16), preferred_element_type=F32)

        hrow = lax.broadcasted_iota(jnp.int32, p_c.shape, 0)
        ps0 = jnp.sum(jnp.where(hrow < HEADS_PER_KV, p_c, 0.0), axis=0, keepdims=True)
        ps1 = jnp.sum(jnp.where(hrow >= HEADS_PER_KV, p_c, 0.0), axis=0, keepdims=True)
        psum = jnp.where(hrow == 0, ps0, jnp.where(hrow == 1, ps1, 0.0))
        n_blk_pad = sel_ref.shape[1]
        n_sel = t_pos // SEL_LEN + 1
        cidx = lax.broadcasted_iota(jnp.int32, (n_half, n_blk_pad), 0)
        bidx = lax.broadcasted_iota(jnp.int32, (n_half, n_blk_pad), 1)
        ratio = SEL_LEN // CMP_STRIDE
        band = jnp.where((cidx >= ratio * bidx - 1) & (cidx <= ratio * bidx + ratio - 1)
                         & (cidx < n_half - 1), 1.0, 0.0).astype(BF16)
        imp = _exact_dot01(psum, band)
        blk = lax.broadcasted_iota(jnp.int32, imp.shape, 1)
        cur = t_pos // SEL_LEN
        forced = (blk == 0) | (blk == cur) | (blk == cur - 1)
        future = blk * SEL_LEN > t_pos
        imp = jnp.where(forced, -NEG, imp)
        imp = jnp.where(future, NEG, imp)
        imp = jnp.where(blk < n_sel, imp, -jnp.inf)
        sel = _topk_select(imp, blk, min(TOP_N, n_sel)) & jnp.logical_not(future) & (blk < n_sel)
        sel_ref[...] = jnp.where(sel, 1.0, 0.0)


def _page_specs(layer, n):
    def spec(k):
        return pl.BlockSpec((None, None, D_KV, PAGE_SIZE),
                            lambda b, s, pt: (layer, pt[b, s * PAGES_PER_STEP + k], 0, 0))
    return [spec(k) for k in range(n)]


def _sample_cmp(page_table, cache_k, cache_v, layer, qz, wk, wv, kn0, t_pos):
    nb, n_pages = page_table.shape
    n_half = n_pages * PAGE_SIZE // CMP_STRIDE
    n_blk_pad = -(-(t_pos // SEL_LEN + 1) // LANES) * LANES
    pp = PAGES_PER_STEP
    per_b = lambda shape: pl.BlockSpec((None,) + shape, lambda b, s, pt: (b, 0, 0))
    const = lambda shape: pl.BlockSpec(shape, lambda b, s, pt: (0,) * len(shape))
    grid_spec = pltpu.PrefetchScalarGridSpec(
        num_scalar_prefetch=1,
        grid=(nb, n_pages // pp),
        in_specs=_page_specs(layer, pp) + _page_specs(layer, pp)
        + [per_b((N_HEADS, LANES)), const(wk.shape), const(wv.shape), const(kn0.shape)],
        out_specs=[per_b((N_HEADS, LANES)), per_b((SUBLANES, n_blk_pad))],
        scratch_shapes=[pltpu.VMEM((n_half, D_KV), F32)] * 4 + [pltpu.VMEM((2 * pp, PAGE_SIZE, D_KV), F32)],
    )
    return pl.pallas_call(
        functools.partial(_sample_cmp_kernel, n_half=n_half, t_pos=t_pos),
        grid_spec=grid_spec,
        out_shape=[jax.ShapeDtypeStruct((nb, N_HEADS, LANES), F32),
                   jax.ShapeDtypeStruct((nb, SUBLANES, n_blk_pad), F32)],
        compiler_params=_cparams("arbitrary", "arbitrary"),
        name="sample_cmp",
    )(page_table, *([cache_k] * pp), *([cache_v] * pp), qz, wk, wv, kn0)


def _sample_sel_kernel(pt_ref, *refs, t_pos):
    del pt_ref
    pp = PAGES_PER_STEP
    kpages, vpages = refs[0:pp], refs[pp:2 * pp]
    qz_ref, sel_ref, kn_ref, vn_ref, o_ref, m_ref, l_ref, acc_ref = refs[2 * pp:]
    s = pl.program_id(1)
    n_keys = pp * PAGE_SIZE

    @pl.when(s == 0)
    def _():
        m_ref[...] = jnp.full(m_ref.shape, NEG, F32)
        l_ref[...] = jnp.zeros(l_ref.shape, F32)
        acc_ref[...] = jnp.zeros(acc_ref.shape, F32)

    qz = qz_ref[...].astype(BF16)
    kk = jnp.concatenate([r[...] for r in kpages], axis=1).astype(BF16)
    vv = jnp.concatenate([r[...] for r in vpages], axis=1).astype(BF16)
    sc = jnp.dot(qz, kk, preferred_element_type=F32)
    kpos = s * n_keys + lax.broadcasted_iota(jnp.int32, (1, n_keys), 1)
    sc = sc - _head_slopes() * (t_pos - kpos).astype(F32)
    n_blk_pad = sel_ref.shape[1]
    hrow = lax.broadcasted_iota(jnp.int32, (N_HEADS, n_blk_pad), 0)
    selv = sel_ref[...]
    sel_h = jnp.where(hrow < HEADS_PER_KV, selv[0:1], selv[1:2]).astype(BF16)
    erow = lax.broadcasted_iota(jnp.int32, (n_blk_pad, n_keys), 0)
    ecol = lax.broadcasted_iota(jnp.int32, (n_blk_pad, n_keys), 1) // SEL_LEN + s * (n_keys // SEL_LEN)
    ok = jnp.dot(sel_h, jnp.where(erow == ecol, 1.0, 0.0).astype(BF16), preferred_element_type=F32) > 0.5
    sc = jnp.where(ok, sc, NEG)
    m_old = m_ref[...]
    m_new = jnp.maximum(m_old, jnp.max(sc, axis=1, keepdims=True))
    p = jnp.where(ok, jnp.exp(sc - m_new), 0.0)
    alpha = jnp.exp(m_old - m_new)
    l_ref[...] = alpha * l_ref[...] + jnp.sum(p, axis=1, keepdims=True)
    acc_ref[...] = alpha * acc_ref[...] + lax.dot_general(p.astype(BF16), vv, (((1,), (1,)), ((), ())),
                                                           preferred_element_type=F32)
    m_ref[...] = m_new

    @pl.when(s == pl.num_programs(1) - 1)
    def _():
        k_new = kn_ref[...].astype(BF16).astype(F32)
        s_new = jnp.sum(qz.astype(F32) * k_new, axis=1, keepdims=True)
        m_o = m_ref[...]
        m_n = jnp.maximum(m_o, s_new)
        a = jnp.exp(m_o - m_n)
        p_new = jnp.exp(s_new - m_n)
        l = a * l_ref[...] + p_new
        acc = a * acc_ref[...] + p_new.astype(BF16).astype(F32) * vn_ref[...].astype(BF16).astype(F32)
        o_ref[...] = acc * (1.0 / l)


def _sample_sel(page_table, cache_k, cache_v, layer, qz, sel, ks_new, vs_new, t_pos):
    nb, n_pages = page_table.shape
    pp = PAGES_PER_STEP
    per_b = lambda shape: pl.BlockSpec((None,) + shape, lambda b, s, pt: (b, 0, 0))
    grid_spec = pltpu.PrefetchScalarGridSpec(
        num_scalar_prefetch=1,
        grid=(nb, n_pages // pp),
        in_specs=_page_specs(layer, pp) + _page_specs(layer, pp)
        + [per_b((N_HEADS, LANES)), per_b(sel.shape[1:]), per_b((1, D_KV)), per_b((1, D_KV))],
        out_specs=per_b((N_HEADS, LANES)),
        scratch_shapes=[pltpu.VMEM((N_HEADS, 1), F32), pltpu.VMEM((N_HEADS, 1), F32),
                        pltpu.VMEM((N_HEADS, LANES), F32)],
    )
    return pl.pallas_call(
        functools.partial(_sample_sel_kernel, t_pos=t_pos),
        grid_spec=grid_spec,
        out_shape=jax.ShapeDtypeStruct((nb, N_HEADS, LANES), F32),
        compiler_params=_cparams("arbitrary", "arbitrary"),
        name="sample_sel",
    )(page_table, *([cache_k] * pp), *([cache_v] * pp), qz, sel, ks_new, vs_new)


def _sample_win_kernel(qz_ref, wk_ref, wv_ref, kn_ref, vn_ref, gate_ref, oc_ref, os_ref,
                       y_ref, wko_ref, wvo_ref):
    qz = qz_ref[...].astype(BF16)
    kwin, vwin = wk_ref[...], wv_ref[...]
    n_win = kwin.shape[1]
    nt = (((1,), (1,)), ((), ()))
    sc = jnp.dot(qz, kwin.astype(BF16), preferred_element_type=F32)
    dist = n_win - lax.broadcasted_iota(jnp.int32, (1, n_win), 1)
    sc = sc - _head_slopes() * dist.astype(F32)
    k_new, v_new = kn_ref[...], vn_ref[...]
    s_new = jnp.sum(qz.astype(F32) * k_new.astype(BF16).astype(F32), axis=1, keepdims=True)
    m = jnp.maximum(jnp.max(sc, axis=1, keepdims=True), s_new)
    p = jnp.exp(sc - m)
    p_new = jnp.exp(s_new - m)
    l = jnp.sum(p, axis=1, keepdims=True) + p_new
    acc = (lax.dot_general(p.astype(BF16), vwin.astype(BF16), nt, preferred_element_type=F32)
           + p_new.astype(BF16).astype(F32) * v_new.astype(BF16).astype(F32))
    o_w = acc * (1.0 / l)
    gate = gate_ref[...]
    y = gate[:, 0:1] * oc_ref[...] + gate[:, 1:2] * os_ref[...] + gate[:, 2:3] * o_w
    lane = lax.broadcasted_iota(jnp.int32, (1, LANES), 1)
    cols = []
    for k in range(N_HEADS // 2):
        grp = (2 * k) // HEADS_PER_KV
        even, odd = y[2 * k:2 * k + 1], y[2 * k + 1:2 * k + 2]
        low = even if grp == 0 else _swap_halves(even)
        high = odd if grp == 1 else _swap_halves(odd)
        cols.append(jnp.where(lane < HEAD_DIM, low, high))
    y_ref[...] = jnp.concatenate(cols, axis=1).astype(y_ref.dtype)
    eye = (lax.broadcasted_iota(jnp.int32, (D_KV, D_KV), 0) == lax.broadcasted_iota(jnp.int32, (D_KV, D_KV), 1))
    as_col = lambda r: jnp.sum(jnp.where(eye, r, 0.0), axis=1, keepdims=True)
    pos = lax.broadcasted_iota(jnp.int32, kwin.shape, 1)
    wko_ref[...] = jnp.where(pos == n_win - 1, as_col(k_new), pltpu.roll(kwin, n_win - 1, axis=1))
    wvo_ref[...] = jnp.where(pos == n_win - 1, as_col(v_new), pltpu.roll(vwin, n_win - 1, axis=1))


def _sample_win(qz, win_k, win_v, layer, kw_new, vw_new, gates_h, o_c, o_s):
    nb = qz.shape[0]
    n_win = win_k.shape[3]
    per_b = lambda shape: pl.BlockSpec((None,) + shape, lambda b: (b, 0, 0))
    cache = pl.BlockSpec((None, None, D_KV, n_win), lambda b: (layer, b, 0, 0))
    hl = (N_HEADS, LANES)
    return pl.pallas_call(
        _sample_win_kernel,
        grid=(nb,),
        in_specs=[per_b(hl), cache, cache, per_b((1, D_KV)), per_b((1, D_KV)), per_b(hl), per_b(hl), per_b(hl)],
        out_specs=[per_b((1, D_ATTN)), per_b((D_KV, n_win)), per_b((D_KV, n_win))],
        out_shape=[jax.ShapeDtypeStruct((nb, 1, D_ATTN), F32), jax.ShapeDtypeStruct((nb, D_KV, n_win), F32),
                   jax.ShapeDtypeStruct((nb, D_KV, n_win), F32)],
        compiler_params=_cparams("arbitrary"),
        name="sample_win",
    )(qz, win_k, win_v, kw_new, vw_new, gates_h, o_c, o_s)


def _slope_rows():
    out = np.zeros((N_KV_HEADS, HEADS_PER_KV * Q_BLOCK, LANES), np.float32)
    for g in range(N_KV_HEADS):
        for hp in range(HEADS_PER_KV):
            slope = 2.0 ** -(g * HEADS_PER_KV + hp + 1)
            out[g, hp * Q_BLOCK:(hp + 1) * Q_BLOCK, HEAD_DIM] = slope * 128.0
            out[g, hp * Q_BLOCK:(hp + 1) * Q_BLOCK, HEAD_DIM + 1] = slope
    return jnp.asarray(out)


def _tile_lanes(v, reps):
    return jnp.tile(v.reshape(1, -1), (1, reps))


def _heads_to_rows(q):
    n = q.shape[0]
    qh = q.reshape(n, N_HEADS, HEAD_DIM)
    z = jnp.zeros_like(qh[:, :HEADS_PER_KV])
    return jnp.concatenate([jnp.concatenate([qh[:, :HEADS_PER_KV], z], axis=-1),
                            jnp.concatenate([z, qh[:, HEADS_PER_KV:]], axis=-1)], axis=1)


def kernel(x_prompt, x_sample, cache_cmp_k, cache_cmp_v, cache_sel_k, cache_sel_v, cache_win_k, cache_win_v, state_conv, page_table, ffn1_norm, ffn1_w_gate, ffn1_w_up, ffn1_w_down, mix_norm, w_in, conv_w, gmlp_norm, gmlp_ws, gmlp_bs, q_norm, k_norm, cmp_wk, cmp_wv, w_branch, w_out, ffn2_norm, ffn2_w_gate, ffn2_w_up, ffn2_w_down):
    nb, seq, d = x_prompt.shape
    ns = x_sample.shape[0]
    depth = w_in.shape[0]
    n_pool = cache_cmp_k.shape[1]
    t_pos = page_table.shape[1] * PAGE_SIZE
    assert x_sample.shape[1] == 1 and cache_win_k.shape[2] == WINDOW
    assert seq % (CMP_STRIDE * LANES) == 0 and t_pos % (CMP_STRIDE * LANES) == 0
    tm = min(ROW_TILE, seq)

    xp = x_prompt.reshape(nb * seq, d)
    xs = x_sample.reshape(ns, d)
    feat_major = lambda c: jnp.transpose(c, (0, 1, 3, 4, 2)).reshape(depth, c.shape[1], D_KV, c.shape[2])
    ck, cv, sk, sv = (feat_major(c) for c in (cache_cmp_k, cache_cmp_v, cache_sel_k, cache_sel_v))
    wink, winv = feat_major(cache_win_k), feat_major(cache_win_v)
    slope_rows = _slope_rows()

    prompt_new = [[] for _ in range(7)]
    sample_new = [[] for _ in range(8)]
    for l in range(depth):
        bf = lambda w: w.astype(BF16)
        w_main = bf(w_in[l, :, :MAIN_COLS])
        cg = w_in[l, :, MAIN_COLS:MAIN_COLS + N_BRANCH * N_HEADS].reshape(d, N_KV_HEADS, HEADS_PER_KV * N_BRANCH)
        w_cg = bf(jnp.pad(cg, ((0, 0), (0, 0), (0, LANES - HEADS_PER_KV * N_BRANCH))).reshape(d, N_KV_HEADS * LANES))
        w_mg = bf(w_in[l, :, MAIN_COLS + N_BRANCH * N_HEADS:])
        wb, wo = bf(w_branch[l]), bf(w_out[l])
        f1 = (ffn1_norm[l].reshape(1, d), bf(ffn1_w_gate[l]), bf(ffn1_w_up[l]), bf(ffn1_w_down[l]))
        f2 = (ffn2_norm[l].reshape(1, d), bf(ffn2_w_gate[l]), bf(ffn2_w_up[l]), bf(ffn2_w_down[l]))
        mn = mix_norm[l].reshape(1, d)
        cw = conv_w[l]
        gn = gmlp_norm[l].reshape(1, D_GMLP)
        gdim = D_GMLP // GMLP_GROUPS
        bs_tile = jnp.repeat(gmlp_bs[l].T, gdim, axis=1)
        ws0 = jnp.repeat(gmlp_ws[l, :, 0, 0], gdim).reshape(1, D_GMLP)
        bs0 = jnp.repeat(gmlp_bs[l, :, 0], gdim).reshape(1, D_GMLP)
        qn = _tile_lanes(q_norm[l], 4)
        kn0 = _tile_lanes(k_norm[l, 0], 2)
        kn12 = jnp.concatenate([_tile_lanes(k_norm[l, 1], 2), _tile_lanes(k_norm[l, 2], 2)], axis=1)
        wk = cmp_wk[l].reshape(CMP_LEN, D_KV)
        wv = cmp_wv[l].reshape(CMP_LEN, D_KV)

        xp = _half_ffn(xp, *f1, tm)
        (yab, q, gates, kc, vc, ks, vs, kww, vww, conv_new, kst, vse, kwt, vwe) = _in_proj_prompt(
            xp, nb, seq, mn, w_main, w_cg, cw, gn, gmlp_ws[l], bs_tile, qn, kn12, tm)
        kct, vce = _compress_prompt(kc, vc, nb, seq, wk, wv, kn0)
        yc = _attention_prompt(q, gates, slope_rows, kct, vce, kst, vse, kwt, vwe, nb, seq)
        xp = _merge(xp, yab, yc, mn, w_mg, wb, wo, tm)
        xp = _half_ffn(xp, *f2, tm)
        kv5 = lambda a: a.reshape(nb, -1, N_KV_HEADS, HEAD_DIM)
        for lst, a in zip(prompt_new, (kv5(kc), kv5(vc), kv5(ks), kv5(vs), kv5(kww), kv5(vww), conv_new)):
            lst.append(a)

        xs = _half_ffn(xs, *f1, ns)
        (yab_s, q_s, gates_s, kc_s, vc_s, ks_s, vs_s, kw_s, vw_s, zc_s, vrow_s) = _in_proj_sample(
            xs, mn, w_main, w_cg, cw, state_conv[l, :, 0], state_conv[l, :, 1], gn, ws0, bs0, qn, kn12)
        qz = _heads_to_rows(q_s)
        o_c, sel = _sample_cmp(page_table, ck, cv, l, qz, wk, wv, kn0, t_pos)
        o_s = _sample_sel(page_table, sk, sv, l, qz, sel, ks_s.reshape(ns, 1, D_KV), vs_s.reshape(ns, 1, D_KV), t_pos)
        gh = gates_s.reshape(ns, N_KV_HEADS, LANES)[:, :, :HEADS_PER_KV * N_BRANCH].reshape(ns, N_HEADS, N_BRANCH)
        gh = jnp.pad(gh, ((0, 0), (0, 0), (0, LANES - N_BRANCH)))
        yc_s, wk_new, wv_new = _sample_win(qz, wink, winv, l, kw_s.reshape(ns, 1, D_KV),
                                           vw_s.reshape(ns, 1, D_KV), gh, o_c, o_s)
        yc_s = yc_s.reshape(ns, D_ATTN).astype(BF16)
        xs = _merge(xs, yab_s, yc_s, mn, w_mg, wb, wo, ns)
        xs = _half_ffn(xs, *f2, ns)
        kv5s = lambda a: a.reshape(ns, -1, N_KV_HEADS, HEAD_DIM)
        conv_s = jnp.stack([state_conv[l, :, 1], zc_s], axis=1)
        rows_major = lambda a: jnp.transpose(a.reshape(ns, N_KV_HEADS, HEAD_DIM, -1), (0, 3, 1, 2))
        for lst, a in zip(sample_new, (kv5s(kc_s), kv5s(vc_s), kv5s(ks_s), kv5s(vs_s), rows_major(wk_new),
                                       rows_major(wv_new),
                                       conv_s, vrow_s.reshape(ns, 1, D_GMLP))):
            lst.append(a)

    outs_p = [jnp.stack(a) for a in prompt_new]
    outs_s = [jnp.stack(a) for a in sample_new]
    return (xp.reshape(nb, seq, d), xs.reshape(ns, 1, d), *outs_p, *outs_s)
```

```python
import functools

import numpy as np
import jax
import jax.numpy as jnp
from jax import lax
from jax.experimental import pallas as pl
from jax.experimental.pallas import tpu as pltpu

F32 = jnp.float32
BF16 = jnp.bfloat16

HEAD_DIM = 64
N_HEADS = 8
N_KV_HEADS = 2
HEADS_PER_KV = N_HEADS // N_KV_HEADS
D_CONV = 256
CONV_W = 3
D_GMLP = 256
GMLP_GROUPS = 4
CHUNK = 128
D_ATTN = N_HEADS * HEAD_DIM
D_KV = N_KV_HEADS * HEAD_DIM
CMP_LEN = 32
CMP_STRIDE = 16
SEL_LEN = 64
TOP_N = 16
WINDOW = 512
Q_BLOCK = 128
N_BRANCH = 3
PAGE_SIZE = 128
EPS = 1e-6
NEG = -1e30
MASK_BIG = 2.0 ** 100
MAIN_COLS = 5 * 256 + D_ATTN + 6 * D_KV

LANES = 128
SUBLANES = 8
ROW_TILE = 512
FF_CHUNK = 256
KEY_CHUNK = 512
CMP_PAGES = 16
SEL_PAGES = 8
SEL_SLOTS = 32
VMEM_LIMIT = 56 * 1024 * 1024


def _cparams(*sem):
    return pltpu.CompilerParams(dimension_semantics=sem, vmem_limit_bytes=VMEM_LIMIT)


def _const_spec(shape):
    nd = len(shape)
    return pl.BlockSpec(shape, lambda *_: (0,) * nd, pipeline_mode=pl.Buffered(1))


def _rms(x, g):
    return x * lax.rsqrt(jnp.mean(x * x, axis=-1, keepdims=True) + EPS) * g


def _sigmoid(x):
    return 1.0 / (1.0 + jnp.exp(-x))


def _gelu_tanh(x):
    return 0.5 * x * (1.0 + jnp.tanh(0.7978845608028654 * (x + 0.044715 * (x * x * x))))


def _split3(x):
    hi = x.astype(BF16)
    r = x - hi.astype(F32)
    mid = r.astype(BF16)
    lo = (r - mid.astype(F32)).astype(BF16)
    return hi, mid, lo


def _exact_dot01(x, m01):
    hi, mid, lo = _split3(x)
    return (jnp.dot(hi, m01, preferred_element_type=F32) + jnp.dot(mid, m01, preferred_element_type=F32)
            + jnp.dot(lo, m01, preferred_element_type=F32))


def _head_group_ones(n):
    r = lax.broadcasted_iota(jnp.int32, (n, n), 0) // HEAD_DIM
    c = lax.broadcasted_iota(jnp.int32, (n, n), 1) // HEAD_DIM
    return jnp.where(r == c, 1.0, 0.0).astype(BF16)


def _head_rms(x, g, ones_bd):
    ssq = _exact_dot01(x * x, ones_bd)
    return x * lax.rsqrt(ssq * (1.0 / HEAD_DIM) + EPS) * g


def _swap_halves(x):
    return pltpu.roll(x, HEAD_DIM, axis=1)


def _value_ext(v, grp):
    lane = lax.broadcasted_iota(jnp.int32, v.shape, 1)
    src = v if grp == 0 else _swap_halves(v)
    return jnp.where(lane < HEAD_DIM, src, 1.0).astype(BF16)


def _pos_rows(pos):
    n = pos.shape[1]
    row = lax.broadcasted_iota(jnp.int32, (HEAD_DIM, n), 0)
    hi = (pos >> 7).astype(F32)
    lo = (pos & 127).astype(F32)
    return jnp.where(row == 0, hi, jnp.where(row == 1, lo, 0.0)).astype(BF16)


def _n_blk_pad(seq):
    return -(-(seq // SEL_LEN) // LANES) * LANES


def _fold_lane_tiles(x, op):
    t = x[:, 0:LANES]
    for k in range(1, x.shape[1] // LANES):
        t = op(t, x[:, k * LANES:(k + 1) * LANES])
    return t


def _row_max(x):
    t = _fold_lane_tiles(x, jnp.maximum)
    return jnp.broadcast_to(jnp.max(t, axis=1, keepdims=True), t.shape)


def _row_sum(x):
    t = _fold_lane_tiles(x, jnp.add)
    return jnp.broadcast_to(jnp.sum(t, axis=1, keepdims=True), t.shape)


def _rep(m, like):
    return jnp.concatenate([m] * (like.shape[1] // LANES), axis=1)


def _topk_select_cols(imp, blk_f, n_iter):
    sel = jnp.zeros(imp.shape, dtype=jnp.bool_)
    for _ in range(n_iter):
        m = jnp.max(imp, axis=0, keepdims=True)
        idx = jnp.min(jnp.where(imp == m, blk_f, float(imp.shape[0])), axis=0, keepdims=True)
        pick = blk_f == idx
        sel = jnp.logical_or(sel, pick)
        imp = jnp.where(pick, -jnp.inf, imp)
    return sel


def _ffn_kernel(x_ref, g_ref, wg_ref, wu_ref, wd_ref, o_ref, acc_ref):
    x = x_ref[...]
    h = _rms(x, g_ref[...]).astype(BF16)
    d_ff = wg_ref.shape[1]
    for c in range(d_ff // FF_CHUNK):
        sl = slice(c * FF_CHUNK, (c + 1) * FF_CHUNK)
        gate = jnp.dot(h, wg_ref[:, sl], preferred_element_type=F32)
        up = jnp.dot(h, wu_ref[:, sl], preferred_element_type=F32)
        a = (gate * _sigmoid(gate) * up).astype(BF16)
        part = jnp.dot(a, wd_ref[sl, :], preferred_element_type=F32)
        if c == 0:
            acc_ref[...] = part
        else:
            acc_ref[...] += part
    o_ref[...] = x + 0.5 * acc_ref[...]


def _half_ffn(x, g, wg, wu, wd, tm):
    m, d = x.shape
    d_ff = wg.shape[1]
    return pl.pallas_call(
        _ffn_kernel,
        grid=(m // tm,),
        in_specs=[pl.BlockSpec((tm, d), lambda i: (i, 0)), _const_spec((1, d)),
                  _const_spec((d, d_ff)), _const_spec((d, d_ff)), _const_spec((d_ff, d))],
        out_specs=pl.BlockSpec((tm, d), lambda i: (i, 0)),
        out_shape=jax.ShapeDtypeStruct((m, d), F32),
        scratch_shapes=[pltpu.VMEM((tm, d), F32)],
        compiler_params=_cparams("arbitrary"),
        name="half_ffn",
    )(x, g, wg, wu, wd)


def _inproj_kernel(x_ref, g_ref, wm_ref, wcg_ref, cw_ref, gn_ref, ws_ref, bs_ref, qn_ref, kn_ref,
                   yab_ref, q_ref, gate_ref, kc_ref, vc_ref, ks_ref, vs_ref, kww_ref, vww_ref, conv_ref,
                   kst_ref, vse_ref, kwt_ref, vwe_ref, zbuf_ref, *, tm, tiles_per_seq):
    j = pl.program_id(0) % tiles_per_seq
    h = _rms(x_ref[...], g_ref[...]).astype(BF16)
    z = jnp.dot(h, wm_ref[...], preferred_element_type=F32)
    gate_ref[...] = _sigmoid(jnp.dot(h, wcg_ref[...], preferred_element_type=F32))

    a_b, a_c, a_x = z[:, 0:256], z[:, 256:512], z[:, 512:768]
    zc = a_c * a_x

    @pl.when(j == 0)
    def _():
        zbuf_ref[0:SUBLANES, :] = jnp.zeros((SUBLANES, D_CONV), F32)

    zbuf_ref[SUBLANES:SUBLANES + tm, :] = zc
    z1 = zbuf_ref[pl.ds(SUBLANES - 1, tm), :]
    z2 = zbuf_ref[pl.ds(SUBLANES - 2, tm), :]
    cw = cw_ref[...]
    y_a = a_b * (cw[0:1] * z2 + cw[1:2] * z1 + cw[2:3] * zc)
    tail = zbuf_ref[tm:tm + SUBLANES, :]
    zbuf_ref[0:SUBLANES, :] = tail
    conv_ref[...] = tail[SUBLANES - (CONV_W - 1):, :]

    u = _gelu_tanh(z[:, 768:1024])
    v = _rms(_gelu_tanh(z[:, 1024:1280]), gn_ref[...]).astype(BF16)
    tri = (lax.broadcasted_iota(jnp.int32, (CHUNK, CHUNK), 0)
           >= lax.broadcasted_iota(jnp.int32, (CHUNK, CHUNK), 1))
    wt = [jnp.where(tri, ws_ref[gi], 0.0).astype(BF16) for gi in range(GMLP_GROUPS)]
    lane_grp = lax.broadcasted_iota(jnp.int32, (CHUNK, D_GMLP), 1) // (D_GMLP // GMLP_GROUPS)
    bias = bs_ref[...]
    yb = []
    for ci in range(tm // CHUNK):
        vch = v[ci * CHUNK:(ci + 1) * CHUNK]
        s = bias
        for gi in range(GMLP_GROUPS):
            s = s + jnp.where(lane_grp == gi, jnp.dot(wt[gi], vch, preferred_element_type=F32), 0.0)
        yb.append(u[ci * CHUNK:(ci + 1) * CHUNK] * s)
    y_b = jnp.concatenate(yb, axis=0)
    yab_ref[...] = jnp.concatenate([y_a, y_b], axis=1).astype(BF16)

    ones_bd = _head_group_ones(256)
    qn = qn_ref[...]
    scale = HEAD_DIM ** -0.5
    q = jnp.concatenate([_head_rms(z[:, 1280:1536], qn, ones_bd),
                         _head_rms(z[:, 1536:1792], qn, ones_bd)], axis=1)
    q_ref[...] = (q * scale).astype(BF16)
    kc_ref[...] = z[:, 1792:1920]
    vc_ref[...] = z[:, 1920:2048]
    vs = z[:, 2176:2304]
    vw = z[:, 2432:2560]
    kn = _head_rms(jnp.concatenate([z[:, 2048:2176], z[:, 2304:2432]], axis=1), kn_ref[...], ones_bd)
    ks, kw = kn[:, 0:128], kn[:, 128:256]
    ks_ref[...] = ks
    vs_ref[...] = vs

    if tm >= WINDOW:
        kww_ref[...] = kw[tm - WINDOW:, :]
        vww_ref[...] = vw[tm - WINDOW:, :]
    else:
        first = tiles_per_seq - WINDOW // tm

        @pl.when(j >= first)
        def _():
            off = pl.multiple_of((j - first) * tm, tm)
            kww_ref[pl.ds(off, tm), :] = kw
            vww_ref[pl.ds(off, tm), :] = vw

    pos = j * tm + lax.broadcasted_iota(jnp.int32, (1, tm), 1)
    prow = _pos_rows(pos)
    kst = ks.T.astype(BF16)
    kwt = kw.T.astype(BF16)
    n_blk_pad = kst_ref.shape[1] - 2 * HEAD_DIM
    blk_row = lax.broadcasted_iota(jnp.int32, (n_blk_pad, tm), 0)
    erows = jnp.where(blk_row == (pos >> 6), 1.0, 0.0).astype(BF16)
    for grp in range(N_KV_HEADS):
        sl = slice(grp * HEAD_DIM, (grp + 1) * HEAD_DIM)
        kst_ref[grp] = jnp.concatenate([erows, kst[sl], prow], axis=0)
        kwt_ref[grp] = jnp.concatenate([kwt[sl], prow], axis=0)
        vse_ref[grp] = _value_ext(vs, grp)
        vwe_ref[grp] = _value_ext(vw, grp)


def _in_proj_prompt(x, nb, seq, g, wm, wcg, cw, gn, ws, bs_tile, qn, kn12, tm):
    m, d = x.shape
    tps = seq // tm
    row = lambda i: (i, 0)
    rows = lambda w, dt: (pl.BlockSpec((tm, w), row), jax.ShapeDtypeStruct((m, w), dt))
    win = (pl.BlockSpec((None, WINDOW, D_KV), lambda i: (i // tps, 0, 0)),
           jax.ShapeDtypeStruct((nb, WINDOW, D_KV), F32))
    kt = (pl.BlockSpec((None, N_KV_HEADS, 2 * HEAD_DIM, tm), lambda i: (i // tps, 0, 0, i % tps)),
          jax.ShapeDtypeStruct((nb, N_KV_HEADS, 2 * HEAD_DIM, seq), BF16))
    ve = (pl.BlockSpec((None, N_KV_HEADS, tm, LANES), lambda i: (i // tps, 0, i % tps, 0)),
          jax.ShapeDtypeStruct((nb, N_KV_HEADS, seq, LANES), BF16))
    conv = (pl.BlockSpec((None, CONV_W - 1, D_CONV), lambda i: (i // tps, 0, 0)),
            jax.ShapeDtypeStruct((nb, CONV_W - 1, D_CONV), F32))
    n_krows = _n_blk_pad(seq) + 2 * HEAD_DIM
    kt_sel = (pl.BlockSpec((None, N_KV_HEADS, n_krows, tm), lambda i: (i // tps, 0, 0, i % tps)),
              jax.ShapeDtypeStruct((nb, N_KV_HEADS, n_krows, seq), BF16))
    outs = [rows(512, BF16), rows(512, BF16), rows(256, F32), rows(128, F32), rows(128, F32),
            rows(128, F32), rows(128, F32), win, win, conv, kt_sel, ve, kt, ve]
    return pl.pallas_call(
        functools.partial(_inproj_kernel, tm=tm, tiles_per_seq=tps),
        grid=(m // tm,),
        in_specs=[pl.BlockSpec((tm, d), row), _const_spec((1, d)), _const_spec(wm.shape),
                  _const_spec(wcg.shape), _const_spec(cw.shape), _const_spec(gn.shape),
                  _const_spec(ws.shape), _const_spec(bs_tile.shape), _const_spec(qn.shape),
                  _const_spec(kn12.shape)],
        out_specs=[o[0] for o in outs],
        out_shape=[o[1] for o in outs],
        scratch_shapes=[pltpu.VMEM((tm + SUBLANES, D_CONV), F32)],
        compiler_params=_cparams("arbitrary"),
        name="in_proj_prompt",
    )(x, g, wm, wcg, cw, gn, ws, bs_tile, qn, kn12)


def _compress_rows(src_ref, w, n_half):
    p0 = jnp.zeros((n_half, D_KV), F32)
    p1 = jnp.zeros((n_half, D_KV), F32)
    for s in range(CMP_STRIDE):
        xs = src_ref[pl.ds(s, n_half, stride=CMP_STRIDE), :]
        p0 = p0 + xs * w[s:s + 1]
        p1 = p1 + xs * w[CMP_STRIDE + s:CMP_STRIDE + s + 1]
    return p0, p1


def _combine_halves(p0, p1):
    n = p0.shape[0]
    row = lax.broadcasted_iota(jnp.int32, p0.shape, 0)
    return jnp.where(row < n - 1, p0 + pltpu.roll(p1, n - 1, axis=0), 0.0)


def _compress_kernel(kc_ref, vc_ref, wk_ref, wv_ref, kn_ref, kct_ref, vce_ref, *, n_half):
    kc = _combine_halves(*_compress_rows(kc_ref, wk_ref[...], n_half))
    vc = _combine_halves(*_compress_rows(vc_ref, wv_ref[...], n_half))
    kc = _head_rms(kc, kn_ref[...], _head_group_ones(D_KV))
    kct = kc.T.astype(BF16)
    cmp_end = lax.broadcasted_iota(jnp.int32, (1, n_half), 1) * CMP_STRIDE + (CMP_LEN - 1)
    prow = _pos_rows(cmp_end)
    for grp in range(N_KV_HEADS):
        kct_ref[grp] = jnp.concatenate([kct[grp * HEAD_DIM:(grp + 1) * HEAD_DIM], prow], axis=0)
        vce_ref[grp] = _value_ext(vc, grp)


def _compress_prompt(kc, vc, nb, seq, wk, wv, kn0):
    n_half = seq // CMP_STRIDE
    return pl.pallas_call(
        functools.partial(_compress_kernel, n_half=n_half),
        grid=(nb,),
        in_specs=[pl.BlockSpec((seq, D_KV), lambda b: (b, 0)), pl.BlockSpec((seq, D_KV), lambda b: (b, 0)),
                  _const_spec(wk.shape), _const_spec(wv.shape), _const_spec(kn0.shape)],
        out_specs=[pl.BlockSpec((None, N_KV_HEADS, 2 * HEAD_DIM, n_half), lambda b: (b, 0, 0, 0)),
                   pl.BlockSpec((None, N_KV_HEADS, n_half, LANES), lambda b: (b, 0, 0, 0))],
        out_shape=[jax.ShapeDtypeStruct((nb, N_KV_HEADS, 2 * HEAD_DIM, n_half), BF16),
                   jax.ShapeDtypeStruct((nb, N_KV_HEADS, n_half, LANES), BF16)],
        compiler_params=_cparams("arbitrary"),
        name="compress_prompt",
    )(kc, vc, wk, wv, kn0)


def _attn_kernel(q_ref, gate_ref, slope_ref, kct_ref, vce_ref, kst_ref, vse_ref, kwt_ref, vwe_ref, o_ref,
                 qx_ref, m_ref, acc_ref, oc_ref, ow_ref, flag_ref, *, n_cmp_pad, n_blk_pad):
    qb = Q_BLOCK
    rows = HEADS_PER_KV * qb
    i = pl.program_id(2)
    p0 = i * qb

    q = q_ref[...].astype(F32)
    lane = lax.broadcasted_iota(jnp.int32, (qb, LANES), 1)
    parts = []
    for hp in range(HEADS_PER_KV):
        col = q[:, (hp // 2) * LANES:(hp // 2 + 1) * LANES]
        if hp % 2 == 1:
            col = _swap_halves(col)
        parts.append(jnp.where(lane < HEAD_DIM, col, 0.0))
    qx = (jnp.concatenate(parts, axis=0) + slope_ref[...]).astype(BF16)

    t_q = p0 + lax.broadcasted_iota(jnp.int32, (qb, 1), 0)
    t_rows = jnp.concatenate([t_q] * HEADS_PER_KV, axis=0)

    n_win = WINDOW + qb
    wstart = pl.multiple_of(jnp.maximum(p0 - WINDOW, 0), qb)
    s_w = jnp.dot(qx, kwt_ref[:, pl.ds(wstart, n_win)], preferred_element_type=F32)
    d_w = t_rows - (wstart + lax.broadcasted_iota(jnp.int32, (1, n_win), 1))
    ok_w = (d_w >= 0) & (d_w <= WINDOW)
    s_w = jnp.where(ok_w, s_w, NEG)
    e_w = jnp.where(ok_w, jnp.exp(s_w - _rep(_row_max(s_w), s_w)), 0.0)
    acc_w = jnp.dot(e_w.astype(BF16), vwe_ref[pl.ds(wstart, n_win), :], preferred_element_type=F32)
    ow_ref[...] = acc_w * (1.0 / _swap_halves(acc_w))

    s_c = jnp.dot(qx, kct_ref[...], preferred_element_type=F32)
    cmp_end = lax.broadcasted_iota(jnp.int32, (1, n_cmp_pad), 1) * CMP_STRIDE + (CMP_LEN - 1)
    vis = cmp_end <= t_rows
    s_c = jnp.where(vis, s_c, NEG)
    e_c = jnp.where(vis, jnp.exp(s_c - _rep(_row_max(s_c), s_c)), 0.0)
    p_c = e_c * _rep(1.0 / jnp.maximum(_row_sum(e_c), 1e-30), e_c)
    oc_ref[...] = jnp.dot(p_c.astype(BF16), vce_ref[...], preferred_element_type=F32)

    psum = p_c[0:qb]
    for hp in range(1, HEADS_PER_KV):
        psum = psum + p_c[hp * qb:(hp + 1) * qb]
    bidx = lax.broadcasted_iota(jnp.int32, (n_blk_pad, n_cmp_pad), 0)
    cidx = lax.broadcasted_iota(jnp.int32, (n_blk_pad, n_cmp_pad), 1)
    ratio = SEL_LEN // CMP_STRIDE
    band_t = jnp.where((cidx >= ratio * bidx - 1) & (cidx <= ratio * bidx + ratio - 1)
                       & (cidx < n_cmp_pad - 1), 1.0, 0.0).astype(BF16)
    nt = (((1,), (1,)), ((), ()))
    imp = sum(lax.dot_general(band_t, part, nt, preferred_element_type=F32)
              for part in _split3(psum))
    blk = lax.broadcasted_iota(jnp.int32, (n_blk_pad, qb), 0)
    t_lane = p0 + lax.broadcasted_iota(jnp.int32, (1, qb), 1)
    cur = t_lane >> 6
    forced = (blk == 0) | (blk == cur) | (blk == cur - 1)
    future = blk * SEL_LEN > t_lane
    imp = jnp.where(forced, -NEG, imp)
    imp = jnp.where(future, NEG, imp)
    sel_t = _topk_select_cols(imp, blk.astype(F32), TOP_N) & jnp.logical_not(future)
    blocks_per_chunk = KEY_CHUNK // SEL_LEN
    sel_f = jnp.where(sel_t, 1.0, 0.0)
    for c in range(flag_ref.shape[0]):
        used = jnp.max(sel_f[c * blocks_per_chunk:(c + 1) * blocks_per_chunk])
        flag_ref[c] = (used > 0.0).astype(jnp.int32)
    selneg = jnp.where(sel_t, 0.0, -MASK_BIG).T.astype(BF16)
    qx_ref[:, 0:n_blk_pad] = jnp.concatenate([selneg] * HEADS_PER_KV, axis=0)
    qx_ref[:, n_blk_pad:] = qx

    m_ref[...] = jnp.full((rows, LANES), NEG, F32)
    acc_ref[...] = jnp.zeros((rows, LANES), F32)

    def chunk_step(c, causal):
        start = pl.multiple_of(c * KEY_CHUNK, KEY_CHUNK)
        k_ext = kst_ref[:, pl.ds(start, KEY_CHUNK)]
        v_ext = vse_ref[pl.ds(start, KEY_CHUNK), :]
        s = jnp.dot(qx_ref[...], k_ext, preferred_element_type=F32)
        if causal:
            kpos = start + lax.broadcasted_iota(jnp.int32, (1, KEY_CHUNK), 1)
            s = jnp.where(kpos <= t_rows, s, NEG)
        m_old = m_ref[...]
        m_new = jnp.maximum(m_old, _row_max(s))
        p = jnp.exp(s - _rep(m_new, s)).astype(BF16)
        acc_ref[...] = jnp.exp(m_old - m_new) * acc_ref[...] + jnp.dot(p, v_ext, preferred_element_type=F32)
        m_ref[...] = m_new

    last = p0 // KEY_CHUNK

    def body(c, carry):
        @pl.when(flag_ref[c] > 0)
        def _():
            chunk_step(c, False)
        return carry

    lax.fori_loop(0, last, body, 0)
    chunk_step(last, True)
    acc_s = acc_ref[...]

    o_s = acc_s * (1.0 / _swap_halves(acc_s))
    o_c = oc_ref[...]
    o_w = ow_ref[...]
    gate = gate_ref[...]
    res = []
    for hp in range(HEADS_PER_KV):
        sl = slice(hp * qb, (hp + 1) * qb)
        gc = gate[:, N_BRANCH * hp + 0:N_BRANCH * hp + 1]
        gs = gate[:, N_BRANCH * hp + 1:N_BRANCH * hp + 2]
        gw = gate[:, N_BRANCH * hp + 2:N_BRANCH * hp + 3]
        res.append(gc * o_c[sl] + gs * o_s[sl] + gw * o_w[sl])
    cols = [jnp.where(lane < HEAD_DIM, res[2 * k], _swap_halves(res[2 * k + 1])) for k in range(2)]
    o_ref[...] = jnp.concatenate(cols, axis=1).astype(o_ref.dtype)


def _attention_prompt(q, gates, slope_rows, kct, vce, kst, vse, kwt, vwe, nb, seq):
    nq = seq // Q_BLOCK
    n_cmp_pad = kct.shape[-1]
    n_blk_pad = _n_blk_pad(seq)
    rows = HEADS_PER_KV * Q_BLOCK
    qspec = pl.BlockSpec((Q_BLOCK, HEADS_PER_KV * HEAD_DIM), lambda b, g, i: (b * nq + i, g))
    per_bg = lambda shape: pl.BlockSpec((None, None) + shape, lambda b, g, i: (b, g, 0, 0))
    return pl.pallas_call(
        functools.partial(_attn_kernel, n_cmp_pad=n_cmp_pad, n_blk_pad=n_blk_pad),
        grid=(nb, N_KV_HEADS, nq),
        in_specs=[qspec,
                  pl.BlockSpec((Q_BLOCK, LANES), lambda b, g, i: (b * nq + i, g)),
                  pl.BlockSpec((None, rows, LANES), lambda b, g, i: (g, 0, 0)),
                  per_bg((2 * HEAD_DIM, n_cmp_pad)), per_bg((n_cmp_pad, LANES)),
                  per_bg((n_blk_pad + 2 * HEAD_DIM, seq)), per_bg((seq, LANES)),
                  per_bg((2 * HEAD_DIM, seq)), per_bg((seq, LANES))],
        out_specs=qspec,
        out_shape=jax.ShapeDtypeStruct(q.shape, BF16),
        scratch_shapes=[pltpu.VMEM((rows, n_blk_pad + LANES), BF16), pltpu.VMEM((rows, LANES), F32),
                        pltpu.VMEM((rows, LANES), F32), pltpu.VMEM((rows, LANES), F32),
                        pltpu.VMEM((rows, LANES), F32), pltpu.SMEM((seq // KEY_CHUNK,), jnp.int32)],
        compiler_params=_cparams("arbitrary", "arbitrary", "arbitrary"),
        name="attention_prompt",
    )(q, gates, slope_rows, kct, vce, kst, vse, kwt, vwe)


def _merge_kernel(x_ref, yab_ref, yc_ref, g_ref, wmg_ref, wb_ref, wo_ref, o_ref):
    x = x_ref[...]
    d = x.shape[1]
    h = _rms(x, g_ref[...]).astype(BF16)
    yab = yab_ref[...]
    branches = (jnp.dot(yab[:, 0:D_CONV], wb_ref[0:D_CONV, :], preferred_element_type=F32),
                jnp.dot(yab[:, D_CONV:], wb_ref[D_CONV:D_CONV + D_GMLP, :], preferred_element_type=F32),
                jnp.dot(yc_ref[...], wb_ref[D_CONV + D_GMLP:, :], preferred_element_type=F32))
    merged = None
    for k, y in enumerate(branches):
        gk = _sigmoid(jnp.dot(h, wmg_ref[:, k * d:(k + 1) * d], preferred_element_type=F32))
        merged = gk * y if merged is None else merged + gk * y
    o_ref[...] = x + jnp.dot(merged.astype(BF16), wo_ref[...], preferred_element_type=F32)


def _merge(x, yab, yc, g, wmg, wb, wo, tm):
    m, d = x.shape
    row = lambda i: (i, 0)
    return pl.pallas_call(
        _merge_kernel,
        grid=(m // tm,),
        in_specs=[pl.BlockSpec((tm, d), row), pl.BlockSpec((tm, yab.shape[1]), row),
                  pl.BlockSpec((tm, yc.shape[1]), row), _const_spec((1, d)), _const_spec(wmg.shape),
                  _const_spec(wb.shape), _const_spec(wo.shape)],
        out_specs=pl.BlockSpec((tm, d), row),
        out_shape=jax.ShapeDtypeStruct((m, d), F32),
        compiler_params=_cparams("arbitrary"),
        name="merge_out",
    )(x, yab, yc, g, wmg, wb, wo)


def _inproj_sample_kernel(x_ref, g_ref, wm_ref, wcg_ref, cw_ref, st0_ref, st1_ref, gn_ref, ws0_ref, bs0_ref,
                          qn_ref, kn_ref, yab_ref, q_ref, gate_ref, kc_ref, vc_ref, ks_ref, vs_ref, kw_ref,
                          vw_ref, zc_ref, vrow_ref):
    h = _rms(x_ref[...], g_ref[...]).astype(BF16)
    z = jnp.dot(h, wm_ref[...], preferred_element_type=F32)
    gate_ref[...] = _sigmoid(jnp.dot(h, wcg_ref[...], preferred_element_type=F32))
    a_b, a_c, a_x = z[:, 0:256], z[:, 256:512], z[:, 512:768]
    zc = a_c * a_x
    cw = cw_ref[...]
    y_a = a_b * (cw[0:1] * st0_ref[...] + cw[1:2] * st1_ref[...] + cw[2:3] * zc)
    zc_ref[...] = zc
    u = _gelu_tanh(z[:, 768:1024])
    v = _rms(_gelu_tanh(z[:, 1024:1280]), gn_ref[...])
    vrow_ref[...] = v
    y_b = u * (ws0_ref[...] * v + bs0_ref[...])
    yab_ref[...] = jnp.concatenate([y_a, y_b], axis=1).astype(BF16)
    ones_bd = _head_group_ones(256)
    qn = qn_ref[...]
    q = jnp.concatenate([_head_rms(z[:, 1280:1536], qn, ones_bd),
                         _head_rms(z[:, 1536:1792], qn, ones_bd)], axis=1)
    q_ref[...] = q * (HEAD_DIM ** -0.5)
    kc_ref[...] = z[:, 1792:1920]
    vc_ref[...] = z[:, 1920:2048]
    vs_ref[...] = z[:, 2176:2304]
    vw_ref[...] = z[:, 2432:2560]
    kn = _head_rms(jnp.concatenate([z[:, 2048:2176], z[:, 2304:2432]], axis=1), kn_ref[...], ones_bd)
    ks_ref[...] = kn[:, 0:128]
    kw_ref[...] = kn[:, 128:256]


def _in_proj_sample(x, g, wm, wcg, cw, st0, st1, gn, ws0, bs0, qn, kn12):
    m = x.shape[0]
    ins = (x, g, wm, wcg, cw, st0, st1, gn, ws0, bs0, qn, kn12)
    sd = lambda w, dt=F32: jax.ShapeDtypeStruct((m, w), dt)
    out_shape = [sd(512, BF16), sd(512), sd(256), sd(128), sd(128), sd(128), sd(128), sd(128), sd(128),
                 sd(256), sd(256)]
    return pl.pallas_call(
        _inproj_sample_kernel,
        grid=(1,),
        in_specs=[_const_spec(a.shape) for a in ins],
        out_specs=[_const_spec(s.shape) for s in out_shape],
        out_shape=out_shape,
        compiler_params=_cparams("arbitrary"),
        name="in_proj_sample",
    )(*ins)


def _head_slopes():
    hrow = lax.broadcasted_iota(jnp.int32, (N_HEADS, 1), 0)
    return lax.bitcast_convert_type((126 - hrow) << 23, F32)


def _sample_cmp_kernel(pt_ref, *refs, n_half, t_pos):
    del pt_ref
    pp = CMP_PAGES
    kpages, vpages = refs[0:pp], refs[pp:2 * pp]
    qz_ref, wt_ref, seg_ref, kn_ref = refs[2 * pp:2 * pp + 4]
    oc_ref, sel_ref = refs[2 * pp + 4:2 * pp + 6]
    p0k_ref, p1k_ref, p0v_ref, p1v_ref = refs[2 * pp + 6:]
    s = pl.program_id(1)
    halves = pp * PAGE_SIZE // CMP_STRIDE
    off = pl.multiple_of(s * halves, halves)
    seg = seg_ref[...]

    def half_sums(pages, w):
        lhs = jnp.concatenate([(r[...] * w).astype(BF16) for r in pages], axis=1)
        return jnp.dot(lhs, seg, preferred_element_type=F32)

    p0k_ref[:, pl.ds(off, halves)] = half_sums(kpages, wt_ref[0])
    p1k_ref[:, pl.ds(off, halves)] = half_sums(kpages, wt_ref[1])
    p0v_ref[:, pl.ds(off, halves)] = half_sums(vpages, wt_ref[2])
    p1v_ref[:, pl.ds(off, halves)] = half_sums(vpages, wt_ref[3])

    @pl.when(s == pl.num_programs(1) - 1)
    def _():
        col = lax.broadcasted_iota(jnp.int32, (D_KV, n_half), 1)
        combine = lambda p0, p1: jnp.where(col < n_half - 1, p0 + pltpu.roll(p1, n_half - 1, axis=1), 0.0)
        kc = combine(p0k_ref[...], p1k_ref[...])
        vc = combine(p0v_ref[...], p1v_ref[...])
        frow = lax.broadcasted_iota(jnp.int32, (D_KV, n_half), 0)
        sq = kc * kc
        ss0 = jnp.sum(jnp.where(frow < HEAD_DIM, sq, 0.0), axis=0, keepdims=True)
        ss1 = jnp.sum(jnp.where(frow >= HEAD_DIM, sq, 0.0), axis=0, keepdims=True)
        inv = lax.rsqrt(jnp.where(frow < HEAD_DIM, ss0, ss1) * (1.0 / HEAD_DIM) + EPS)
        kc = kc * inv * kn_ref[...]
        qz = qz_ref[...].astype(BF16)
        s_c = jnp.dot(qz, kc.astype(BF16), preferred_element_type=F32)
        cmp_end = lax.broadcasted_iota(jnp.int32, (1, n_half), 1) * CMP_STRIDE + (CMP_LEN - 1)
        d_c = t_pos - cmp_end
        vis = d_c >= 0
        s_c = jnp.where(vis, s_c - _head_slopes() * d_c.astype(F32), NEG)
        e_c = jnp.where(vis, jnp.exp(s_c - jnp.max(s_c, axis=1, keepdims=True)), 0.0)
        p_c = e_c * (1.0 / jnp.maximum(jnp.sum(e_c, axis=1, keepdims=True), 1e-30))
        nt = (((1,), (1,)), ((), ()))
        oc_ref[...] = lax.dot_general(p_c.astype(BF16), vc.astype(BF16), nt, preferred_element_type=F32)

        hrow = lax.broadcasted_iota(jnp.int32, p_c.shape, 0)
        ps0 = jnp.sum(jnp.where(hrow < HEADS_PER_KV, p_c, 0.0), axis=0, keepdims=True)
        ps1 = jnp.sum(jnp.where(hrow >= HEADS_PER_KV, p_c, 0.0), axis=0, keepdims=True)
        prow = lax.broadcasted_iota(jnp.int32, (LANES, n_half), 0)
        psum = jnp.where(prow == 0, ps0, jnp.where(prow == 1, ps1, 0.0))
        n_blk_pad = sel_ref.shape[1]
        n_sel = t_pos // SEL_LEN + 1
        bidx = lax.broadcasted_iota(jnp.int32, (n_blk_pad, n_half), 0)
        cidx = lax.broadcasted_iota(jnp.int32, (n_blk_pad, n_half), 1)
        ratio = SEL_LEN // CMP_STRIDE
        band_t = jnp.where((cidx >= ratio * bidx - 1) & (cidx <= ratio * bidx + ratio - 1)
                           & (cidx < n_half - 1), 1.0, 0.0).astype(BF16)
        imp = sum(lax.dot_general(band_t, part, nt, preferred_element_type=F32) for part in _split3(psum))
        blk = lax.broadcasted_iota(jnp.int32, imp.shape, 0)
        cur = t_pos // SEL_LEN
        forced = (blk == 0) | (blk == cur) | (blk == cur - 1)
        future = blk * SEL_LEN > t_pos
        imp = jnp.where(forced, -NEG, imp)
        imp = jnp.where(future, NEG, imp)
        imp = jnp.where(blk < n_sel, imp, -jnp.inf)
        sel_t = (_topk_select_cols(imp, blk.astype(F32), min(TOP_N, n_sel))
                 & jnp.logical_not(future) & (blk < n_sel))
        sel_ref[...] = jnp.where(sel_t, 1.0, 0.0).T[0:SUBLANES]


def _page_specs(layer, n, table_col):
    def spec(k):
        return pl.BlockSpec((None, None, D_KV, PAGE_SIZE),
                            lambda b, s, *tabs: (layer, table_col(tabs, b, s * n + k), 0, 0))
    return [spec(k) for k in range(n)]


def _sample_cmp(page_table, cache_k, cache_v, layer, qz, wt, seg, kn0_col, t_pos):
    nb, n_pages = page_table.shape
    n_half = n_pages * PAGE_SIZE // CMP_STRIDE
    n_blk_pad = -(-(t_pos // SEL_LEN + 1) // LANES) * LANES
    pp = CMP_PAGES
    per_b = lambda shape: pl.BlockSpec((None,) + shape, lambda b, s, pt: (b, 0, 0))
    const = lambda shape: pl.BlockSpec(shape, lambda b, s, pt: (0,) * len(shape))
    pages = lambda: _page_specs(layer, pp, lambda tabs, b, j: tabs[0][b, j])
    grid_spec = pltpu.PrefetchScalarGridSpec(
        num_scalar_prefetch=1,
        grid=(nb, n_pages // pp),
        in_specs=pages() + pages()
        + [per_b((N_HEADS, LANES)), const(wt.shape), const(seg.shape), const(kn0_col.shape)],
        out_specs=[per_b((N_HEADS, LANES)), per_b((SUBLANES, n_blk_pad))],
        scratch_shapes=[pltpu.VMEM((D_KV, n_half), F32)] * 4,
    )
    return pl.pallas_call(
        functools.partial(_sample_cmp_kernel, n_half=n_half, t_pos=t_pos),
        grid_spec=grid_spec,
        out_shape=[jax.ShapeDtypeStruct((nb, N_HEADS, LANES), F32),
                   jax.ShapeDtypeStruct((nb, SUBLANES, n_blk_pad), F32)],
        compiler_params=_cparams("arbitrary", "arbitrary"),
        name="sample_cmp",
    )(page_table, *([cache_k] * pp), *([cache_v] * pp), qz, wt, seg, kn0_col)


def _sample_sel_kernel(phys_ref, lp_ref, *refs, t_pos):
    del phys_ref
    pp = SEL_PAGES
    kpages, vpages = refs[0:pp], refs[pp:2 * pp]
    qz_ref, sel_ref, kn_ref, vn_ref, o_ref, m_ref, l_ref, acc_ref = refs[2 * pp:]
    b, s = pl.program_id(0), pl.program_id(1)
    n_keys = pp * PAGE_SIZE

    @pl.when(s == 0)
    def _():
        m_ref[...] = jnp.full(m_ref.shape, NEG, F32)
        l_ref[...] = jnp.zeros(l_ref.shape, F32)
        acc_ref[...] = jnp.zeros(acc_ref.shape, F32)

    lane = lax.broadcasted_iota(jnp.int32, (1, PAGE_SIZE), 1)
    lps = [lp_ref[b, s * pp + k] for k in range(pp)]
    kpos = jnp.concatenate([lp * PAGE_SIZE + lane for lp in lps], axis=1)
    kblk = jnp.concatenate([lp * (PAGE_SIZE // SEL_LEN) + lane // SEL_LEN for lp in lps], axis=1)
    qz = qz_ref[...].astype(BF16)
    kk = jnp.concatenate([r[...] for r in kpages], axis=1).astype(BF16)
    vv = jnp.concatenate([r[...] for r in vpages], axis=1).astype(BF16)
    sc = jnp.dot(qz, kk, preferred_element_type=F32)
    sc = sc - _head_slopes() * (t_pos - kpos).astype(F32)
    n_blk_pad = sel_ref.shape[1]
    hrow = lax.broadcasted_iota(jnp.int32, (N_HEADS, n_blk_pad), 0)
    selv = sel_ref[...]
    sel_h = jnp.where(hrow < HEADS_PER_KV, selv[0:1], selv[1:2]).astype(BF16)
    erow = lax.broadcasted_iota(jnp.int32, (n_blk_pad, n_keys), 0)
    ok = jnp.dot(sel_h, jnp.where(erow == kblk, 1.0, 0.0).astype(BF16), preferred_element_type=F32) > 0.5
    sc = jnp.where(ok, sc, NEG)
    m_old = m_ref[...]
    m_new = jnp.maximum(m_old, jnp.max(sc, axis=1, keepdims=True))
    p = jnp.where(ok, jnp.exp(sc - m_new), 0.0)
    alpha = jnp.exp(m_old - m_new)
    l_ref[...] = alpha * l_ref[...] + jnp.sum(p, axis=1, keepdims=True)
    acc_ref[...] = alpha * acc_ref[...] + lax.dot_general(p.astype(BF16), vv, (((1,), (1,)), ((), ())),
                                                           preferred_element_type=F32)
    m_ref[...] = m_new

    @pl.when(s == pl.num_programs(1) - 1)
    def _():
        k_new = kn_ref[...].astype(BF16).astype(F32)
        s_new = jnp.sum(qz.astype(F32) * k_new, axis=1, keepdims=True)
        m_o = m_ref[...]
        m_n = jnp.maximum(m_o, s_new)
        a = jnp.exp(m_o - m_n)
        p_new = jnp.exp(s_new - m_n)
        l = a * l_ref[...] + p_new
        acc = a * acc_ref[...] + p_new.astype(BF16).astype(F32) * vn_ref[...].astype(BF16).astype(F32)
        o_ref[...] = acc * (1.0 / l)


def _needed_pages(sel, page_table):
    nb, n_pages = page_table.shape
    per_page = PAGE_SIZE // SEL_LEN
    flags = sel[:, :N_KV_HEADS, :n_pages * per_page] > 0.5
    need = flags.reshape(nb, N_KV_HEADS, n_pages, per_page).any(axis=(1, 3))
    n_slots = min(SEL_SLOTS, n_pages)
    order = jnp.argsort(jnp.logical_not(need), axis=1, stable=True)[:, :n_slots]
    count = need.sum(axis=1, keepdims=True)
    valid = jnp.arange(n_slots, dtype=jnp.int32)[None, :] < count
    logical = jnp.where(valid, order, -1).astype(jnp.int32)
    phys = jnp.take_along_axis(page_table, jnp.maximum(logical, 0), axis=1).astype(jnp.int32)
    return phys, logical


def _sample_sel(phys, logical, cache_k, cache_v, layer, qz, sel, ks_new, vs_new, t_pos):
    nb = phys.shape[0]
    pp = SEL_PAGES
    per_b = lambda shape: pl.BlockSpec((None,) + shape, lambda b, s, ph, lp: (b, 0, 0))
    pages = lambda: _page_specs(layer, pp, lambda tabs, b, j: tabs[0][b, j])
    grid_spec = pltpu.PrefetchScalarGridSpec(
        num_scalar_prefetch=2,
        grid=(nb, phys.shape[1] // pp),
        in_specs=pages() + pages()
        + [per_b((N_HEADS, LANES)), per_b(sel.shape[1:]), per_b((1, D_KV)), per_b((1, D_KV))],
        out_specs=per_b((N_HEADS, LANES)),
        scratch_shapes=[pltpu.VMEM((N_HEADS, 1), F32), pltpu.VMEM((N_HEADS, 1), F32),
                        pltpu.VMEM((N_HEADS, LANES), F32)],
    )
    return pl.pallas_call(
        functools.partial(_sample_sel_kernel, t_pos=t_pos),
        grid_spec=grid_spec,
        out_shape=jax.ShapeDtypeStruct((nb, N_HEADS, LANES), F32),
        compiler_params=_cparams("arbitrary", "arbitrary"),
        name="sample_sel",
    )(phys, logical, *([cache_k] * pp), *([cache_v] * pp), qz, sel, ks_new, vs_new)


def _sample_win_kernel(qz_ref, wk_ref, wv_ref, kn_ref, vn_ref, gate_ref, oc_ref, os_ref,
                       y_ref, wko_ref, wvo_ref):
    qz = qz_ref[...].astype(BF16)
    kwin, vwin = wk_ref[...], wv_ref[...]
    n_win = kwin.shape[1]
    nt = (((1,), (1,)), ((), ()))
    sc = jnp.dot(qz, kwin.astype(BF16), preferred_element_type=F32)
    dist = n_win - lax.broadcasted_iota(jnp.int32, (1, n_win), 1)
    sc = sc - _head_slopes() * dist.astype(F32)
    k_new, v_new = kn_ref[...], vn_ref[...]
    s_new = jnp.sum(qz.astype(F32) * k_new.astype(BF16).astype(F32), axis=1, keepdims=True)
    m = jnp.maximum(jnp.max(sc, axis=1, keepdims=True), s_new)
    p = jnp.exp(sc - m)
    p_new = jnp.exp(s_new - m)
    l = jnp.sum(p, axis=1, keepdims=True) + p_new
    acc = (lax.dot_general(p.astype(BF16), vwin.astype(BF16), nt, preferred_element_type=F32)
           + p_new.astype(BF16).astype(F32) * v_new.astype(BF16).astype(F32))
    o_w = acc * (1.0 / l)
    gate = gate_ref[...]
    y = gate[:, 0:1] * oc_ref[...] + gate[:, 1:2] * os_ref[...] + gate[:, 2:3] * o_w
    lane = lax.broadcasted_iota(jnp.int32, (1, LANES), 1)
    cols = []
    for k in range(N_HEADS // 2):
        grp = (2 * k) // HEADS_PER_KV
        even, odd = y[2 * k:2 * k + 1], y[2 * k + 1:2 * k + 2]
        low = even if grp == 0 else _swap_halves(even)
        high = odd if grp == 1 else _swap_halves(odd)
        cols.append(jnp.where(lane < HEAD_DIM, low, high))
    y_ref[...] = jnp.concatenate(cols, axis=1).astype(y_ref.dtype)
    eye = (lax.broadcasted_iota(jnp.int32, (D_KV, D_KV), 0) == lax.broadcasted_iota(jnp.int32, (D_KV, D_KV), 1))
    as_col = lambda r: jnp.sum(jnp.where(eye, r, 0.0), axis=1, keepdims=True)
    pos = lax.broadcasted_iota(jnp.int32, kwin.shape, 1)
    wko_ref[...] = jnp.where(pos == n_win - 1, as_col(k_new), pltpu.roll(kwin, n_win - 1, axis=1))
    wvo_ref[...] = jnp.where(pos == n_win - 1, as_col(v_new), pltpu.roll(vwin, n_win - 1, axis=1))


def _sample_win(qz, win_k, win_v, layer, kw_new, vw_new, gates_h, o_c, o_s):
    nb = qz.shape[0]
    n_win = win_k.shape[3]
    per_b = lambda shape: pl.BlockSpec((None,) + shape, lambda b: (b, 0, 0))
    cache = pl.BlockSpec((None, None, D_KV, n_win), lambda b: (layer, b, 0, 0))
    hl = (N_HEADS, LANES)
    return pl.pallas_call(
        _sample_win_kernel,
        grid=(nb,),
        in_specs=[per_b(hl), cache, cache, per_b((1, D_KV)), per_b((1, D_KV)), per_b(hl), per_b(hl), per_b(hl)],
        out_specs=[per_b((1, D_ATTN)), per_b((D_KV, n_win)), per_b((D_KV, n_win))],
        out_shape=[jax.ShapeDtypeStruct((nb, 1, D_ATTN), F32), jax.ShapeDtypeStruct((nb, D_KV, n_win), F32),
                   jax.ShapeDtypeStruct((nb, D_KV, n_win), F32)],
        compiler_params=_cparams("arbitrary"),
        name="sample_win",
    )(qz, win_k, win_v, kw_new, vw_new, gates_h, o_c, o_s)


def _slope_rows():
    out = np.zeros((N_KV_HEADS, HEADS_PER_KV * Q_BLOCK, LANES), np.float32)
    for g in range(N_KV_HEADS):
        for hp in range(HEADS_PER_KV):
            slope = 2.0 ** -(g * HEADS_PER_KV + hp + 1)
            out[g, hp * Q_BLOCK:(hp + 1) * Q_BLOCK, HEAD_DIM] = slope * 128.0
            out[g, hp * Q_BLOCK:(hp + 1) * Q_BLOCK, HEAD_DIM + 1] = slope
    return jnp.asarray(out)


def _tile_lanes(v, reps):
    return jnp.tile(v.reshape(1, -1), (1, reps))


def _heads_to_rows(q):
    n = q.shape[0]
    qh = q.reshape(n, N_HEADS, HEAD_DIM)
    z = jnp.zeros_like(qh[:, :HEADS_PER_KV])
    return jnp.concatenate([jnp.concatenate([qh[:, :HEADS_PER_KV], z], axis=-1),
                            jnp.concatenate([z, qh[:, HEADS_PER_KV:]], axis=-1)], axis=1)


def kernel(x_prompt, x_sample, cache_cmp_k, cache_cmp_v, cache_sel_k, cache_sel_v, cache_win_k, cache_win_v, state_conv, page_table, ffn1_norm, ffn1_w_gate, ffn1_w_up, ffn1_w_down, mix_norm, w_in, conv_w, gmlp_norm, gmlp_ws, gmlp_bs, q_norm, k_norm, cmp_wk, cmp_wv, w_branch, w_out, ffn2_norm, ffn2_w_gate, ffn2_w_up, ffn2_w_down):
    nb, seq, d = x_prompt.shape
    ns = x_sample.shape[0]
    depth = w_in.shape[0]
    t_pos = page_table.shape[1] * PAGE_SIZE
    assert x_sample.shape[1] == 1 and cache_win_k.shape[2] == WINDOW
    assert seq % (CMP_STRIDE * LANES) == 0 and t_pos % (CMP_STRIDE * LANES) == 0
    assert page_table.shape[1] % CMP_PAGES == 0
    tm = min(ROW_TILE, seq)

    xp = x_prompt.reshape(nb * seq, d)
    xs = x_sample.reshape(ns, d)
    feat_major = lambda c: jnp.transpose(c, (0, 1, 3, 4, 2)).reshape(depth, c.shape[1], D_KV, c.shape[2])
    ck, cv, sk, sv = (feat_major(c) for c in (cache_cmp_k, cache_cmp_v, cache_sel_k, cache_sel_v))
    wink, winv = feat_major(cache_win_k), feat_major(cache_win_v)
    slope_rows = _slope_rows()
    seg_rows = CMP_PAGES * PAGE_SIZE
    seg = jnp.asarray(np.arange(seg_rows)[:, None] // CMP_STRIDE == np.arange(seg_rows // CMP_STRIDE)[None, :], BF16)

    prompt_new = [[] for _ in range(7)]
    sample_new = [[] for _ in range(8)]
    for l in range(depth):
        bf = lambda w: w.astype(BF16)
        w_main = bf(w_in[l, :, :MAIN_COLS])
        cg = w_in[l, :, MAIN_COLS:MAIN_COLS + N_BRANCH * N_HEADS].reshape(d, N_KV_HEADS, HEADS_PER_KV * N_BRANCH)
        w_cg = bf(jnp.pad(cg, ((0, 0), (0, 0), (0, LANES - HEADS_PER_KV * N_BRANCH))).reshape(d, N_KV_HEADS * LANES))
        w_mg = bf(w_in[l, :, MAIN_COLS + N_BRANCH * N_HEADS:])
        wb, wo = bf(w_branch[l]), bf(w_out[l])
        f1 = (ffn1_norm[l].reshape(1, d), bf(ffn1_w_gate[l]), bf(ffn1_w_up[l]), bf(ffn1_w_down[l]))
        f2 = (ffn2_norm[l].reshape(1, d), bf(ffn2_w_gate[l]), bf(ffn2_w_up[l]), bf(ffn2_w_down[l]))
        mn = mix_norm[l].reshape(1, d)
        cw = conv_w[l]
        gn = gmlp_norm[l].reshape(1, D_GMLP)
        gdim = D_GMLP // GMLP_GROUPS
        bs_tile = jnp.repeat(gmlp_bs[l].T, gdim, axis=1)
        ws0 = jnp.repeat(gmlp_ws[l, :, 0, 0], gdim).reshape(1, D_GMLP)
        bs0 = jnp.repeat(gmlp_bs[l, :, 0], gdim).reshape(1, D_GMLP)
        qn = _tile_lanes(q_norm[l], 4)
        kn0 = _tile_lanes(k_norm[l, 0], 2)
        kn12 = jnp.concatenate([_tile_lanes(k_norm[l, 1], 2), _tile_lanes(k_norm[l, 2], 2)], axis=1)
        wk = cmp_wk[l].reshape(CMP_LEN, D_KV)
        wv = cmp_wv[l].reshape(CMP_LEN, D_KV)
        taps = lambda w: jnp.tile(w.T, (1, PAGE_SIZE // CMP_STRIDE))
        wt = jnp.stack([taps(wk[:CMP_STRIDE]), taps(wk[CMP_STRIDE:]), taps(wv[:CMP_STRIDE]), taps(wv[CMP_STRIDE:])])

        xp = _half_ffn(xp, *f1, tm)
        (yab, q, gates, kc, vc, ks, vs, kww, vww, conv_new, kst, vse, kwt, vwe) = _in_proj_prompt(
            xp, nb, seq, mn, w_main, w_cg, cw, gn, gmlp_ws[l], bs_tile, qn, kn12, tm)
        kct, vce = _compress_prompt(kc, vc, nb, seq, wk, wv, kn0)
        yc = _attention_prompt(q, gates, slope_rows, kct, vce, kst, vse, kwt, vwe, nb, seq)
        xp = _merge(xp, yab, yc, mn, w_mg, wb, wo, tm)
        xp = _half_ffn(xp, *f2, tm)
        kv5 = lambda a: a.reshape(nb, -1, N_KV_HEADS, HEAD_DIM)
        for lst, a in zip(prompt_new, (kv5(kc), kv5(vc), kv5(ks), kv5(vs), kv5(kww), kv5(vww), conv_new)):
            lst.append(a)

        xs = _half_ffn(xs, *f1, ns)
        (yab_s, q_s, gates_s, kc_s, vc_s, ks_s, vs_s, kw_s, vw_s, zc_s, vrow_s) = _in_proj_sample(
            xs, mn, w_main, w_cg, cw, state_conv[l, :, 0], state_conv[l, :, 1], gn, ws0, bs0, qn, kn12)
        qz = _heads_to_rows(q_s)
        o_c, sel = _sample_cmp(page_table, ck, cv, l, qz, wt, seg, kn0.reshape(D_KV, 1), t_pos)
        phys, logical = _needed_pages(sel, page_table)
        o_s = _sample_sel(phys, logical, sk, sv, l, qz, sel, ks_s.reshape(ns, 1, D_KV),
                          vs_s.reshape(ns, 1, D_KV), t_pos)
        gh = gates_s.reshape(ns, N_KV_HEADS, LANES)[:, :, :HEADS_PER_KV * N_BRANCH].reshape(ns, N_HEADS, N_BRANCH)
        gh = jnp.pad(gh, ((0, 0), (0, 0), (0, LANES - N_BRANCH)))
        yc_s, wk_new, wv_new = _sample_win(qz, wink, winv, l, kw_s.reshape(ns, 1, D_KV),
                                           vw_s.reshape(ns, 1, D_KV), gh, o_c, o_s)
        yc_s = yc_s.reshape(ns, D_ATTN).astype(BF16)
        xs = _merge(xs, yab_s, yc_s, mn, w_mg, wb, wo, ns)
        xs = _half_ffn(xs, *f2, ns)
        kv5s = lambda a: a.reshape(ns, -1, N_KV_HEADS, HEAD_DIM)
        conv_s = jnp.stack([state_conv[l, :, 1], zc_s], axis=1)
        rows_major = lambda a: jnp.transpose(a.reshape(ns, N_KV_HEADS, HEAD_DIM, -1), (0, 3, 1, 2))
        for lst, a in zip(sample_new, (kv5s(kc_s), kv5s(vc_s), kv5s(ks_s), kv5s(vs_s), rows_major(wk_new),
                                       rows_major(wv_new), conv_s, vrow_s.reshape(ns, 1, D_GMLP))):
            lst.append(a)

    outs_p = [jnp.stack(a) for a in prompt_new]
    outs_s = [jnp.stack(a) for a in sample_new]
    return (xp.reshape(nb, seq, d), xs.reshape(ns, 1, d), *outs_p, *outs_s)
```

```python
import functools

import numpy as np
import jax
import jax.numpy as jnp
from jax import lax
from jax.experimental import pallas as pl
from jax.experimental.pallas import tpu as pltpu

F32 = jnp.float32
BF16 = jnp.bfloat16

HEAD_DIM = 64
N_HEADS = 8
N_KV_HEADS = 2
HEADS_PER_KV = N_HEADS // N_KV_HEADS
D_CONV = 256
CONV_W = 3
D_GMLP = 256
GMLP_GROUPS = 4
CHUNK = 128
D_ATTN = N_HEADS * HEAD_DIM
D_KV = N_KV_HEADS * HEAD_DIM
CMP_LEN = 32
CMP_STRIDE = 16
SEL_LEN = 64
TOP_N = 16
WINDOW = 512
Q_BLOCK = 128
N_BRANCH = 3
PAGE_SIZE = 128
EPS = 1e-6
NEG = -1e30
MASK_BIG = 2.0 ** 100
MAIN_COLS = 5 * 256 + D_ATTN + 6 * D_KV

LANES = 128
SUBLANES = 8
ROW_TILE = 512
FF_CHUNK = 256
KEY_CHUNK = 512
CMP_PAGES = 16
SEL_PAGES = 8
SEL_SLOTS = 32
VMEM_LIMIT = 56 * 1024 * 1024


def _cparams(*sem):
    return pltpu.CompilerParams(dimension_semantics=sem, vmem_limit_bytes=VMEM_LIMIT)


def _const_spec(shape):
    nd = len(shape)
    return pl.BlockSpec(shape, lambda *_: (0,) * nd, pipeline_mode=pl.Buffered(1))


def _rms(x, g):
    return x * lax.rsqrt(jnp.mean(x * x, axis=-1, keepdims=True) + EPS) * g


def _sigmoid(x):
    return 1.0 / (1.0 + jnp.exp(-x))


def _gelu_tanh(x):
    return 0.5 * x * (1.0 + jnp.tanh(0.7978845608028654 * (x + 0.044715 * (x * x * x))))


def _split3(x):
    hi = x.astype(BF16)
    r = x - hi.astype(F32)
    mid = r.astype(BF16)
    lo = (r - mid.astype(F32)).astype(BF16)
    return hi, mid, lo


def _exact_dot01(x, m01):
    hi, mid, lo = _split3(x)
    return (jnp.dot(hi, m01, preferred_element_type=F32) + jnp.dot(mid, m01, preferred_element_type=F32)
            + jnp.dot(lo, m01, preferred_element_type=F32))


def _head_group_ones(n):
    r = lax.broadcasted_iota(jnp.int32, (n, n), 0) // HEAD_DIM
    c = lax.broadcasted_iota(jnp.int32, (n, n), 1) // HEAD_DIM
    return jnp.where(r == c, 1.0, 0.0).astype(BF16)


def _head_rms(x, g, ones_bd):
    ssq = _exact_dot01(x * x, ones_bd)
    return x * lax.rsqrt(ssq * (1.0 / HEAD_DIM) + EPS) * g


def _swap_halves(x):
    return pltpu.roll(x, HEAD_DIM, axis=1)


def _value_ext(v, grp):
    lane = lax.broadcasted_iota(jnp.int32, v.shape, 1)
    src = v if grp == 0 else _swap_halves(v)
    return jnp.where(lane < HEAD_DIM, src, 1.0).astype(BF16)


def _pos_rows(pos):
    n = pos.shape[1]
    row = lax.broadcasted_iota(jnp.int32, (HEAD_DIM, n), 0)
    hi = (pos >> 7).astype(F32)
    lo = (pos & 127).astype(F32)
    return jnp.where(row == 0, hi, jnp.where(row == 1, lo, 0.0)).astype(BF16)


def _n_blk_pad(seq):
    return -(-(seq // SEL_LEN) // LANES) * LANES


def _fold_lane_tiles(x, op):
    t = x[:, 0:LANES]
    for k in range(1, x.shape[1] // LANES):
        t = op(t, x[:, k * LANES:(k + 1) * LANES])
    return t


def _row_max(x):
    t = _fold_lane_tiles(x, jnp.maximum)
    return jnp.broadcast_to(jnp.max(t, axis=1, keepdims=True), t.shape)


def _row_sum(x):
    t = _fold_lane_tiles(x, jnp.add)
    return jnp.broadcast_to(jnp.sum(t, axis=1, keepdims=True), t.shape)


def _rep(m, like):
    return jnp.concatenate([m] * (like.shape[1] // LANES), axis=1)


def _topk_select_cols(imp, blk_f, n_iter):
    sel = jnp.zeros(imp.shape, dtype=jnp.bool_)
    for _ in range(n_iter):
        m = jnp.max(imp, axis=0, keepdims=True)
        idx = jnp.min(jnp.where(imp == m, blk_f, float(imp.shape[0])), axis=0, keepdims=True)
        pick = blk_f == idx
        sel = jnp.logical_or(sel, pick)
        imp = jnp.where(pick, -jnp.inf, imp)
    return sel


def _ffn_kernel(x_ref, g_ref, wg_ref, wu_ref, wd_ref, o_ref, acc_ref):
    x = x_ref[...]
    h = _rms(x, g_ref[...]).astype(BF16)
    d_ff = wg_ref.shape[1]
    for c in range(d_ff // FF_CHUNK):
        sl = slice(c * FF_CHUNK, (c + 1) * FF_CHUNK)
        gate = jnp.dot(h, wg_ref[:, sl], preferred_element_type=F32)
        up = jnp.dot(h, wu_ref[:, sl], preferred_element_type=F32)
        a = (gate * _sigmoid(gate) * up).astype(BF16)
        part = jnp.dot(a, wd_ref[sl, :], preferred_element_type=F32)
        if c == 0:
            acc_ref[...] = part
        else:
            acc_ref[...] += part
    o_ref[...] = x + 0.5 * acc_ref[...]


def _half_ffn(x, g, wg, wu, wd, tm):
    m, d = x.shape
    d_ff = wg.shape[1]
    return pl.pallas_call(
        _ffn_kernel,
        grid=(m // tm,),
        in_specs=[pl.BlockSpec((tm, d), lambda i: (i, 0)), _const_spec((1, d)),
                  _const_spec((d, d_ff)), _const_spec((d, d_ff)), _const_spec((d_ff, d))],
        out_specs=pl.BlockSpec((tm, d), lambda i: (i, 0)),
        out_shape=jax.ShapeDtypeStruct((m, d), F32),
        scratch_shapes=[pltpu.VMEM((tm, d), F32)],
        compiler_params=_cparams("arbitrary"),
        name="half_ffn",
    )(x, g, wg, wu, wd)


def _inproj_kernel(x_ref, g_ref, wm_ref, wcg_ref, cw_ref, gn_ref, ws_ref, bs_ref, qn_ref, kn_ref,
                   yab_ref, q_ref, gate_ref, kc_ref, vc_ref, ks_ref, vs_ref, kww_ref, vww_ref, conv_ref,
                   kst_ref, vse_ref, kwt_ref, vwe_ref, zbuf_ref, *, tm, tiles_per_seq):
    j = pl.program_id(0) % tiles_per_seq
    h = _rms(x_ref[...], g_ref[...]).astype(BF16)
    z = jnp.dot(h, wm_ref[...], preferred_element_type=F32)
    gate_ref[...] = _sigmoid(jnp.dot(h, wcg_ref[...], preferred_element_type=F32))

    a_b, a_c, a_x = z[:, 0:256], z[:, 256:512], z[:, 512:768]
    zc = a_c * a_x

    @pl.when(j == 0)
    def _():
        zbuf_ref[0:SUBLANES, :] = jnp.zeros((SUBLANES, D_CONV), F32)

    zbuf_ref[SUBLANES:SUBLANES + tm, :] = zc
    z1 = zbuf_ref[pl.ds(SUBLANES - 1, tm), :]
    z2 = zbuf_ref[pl.ds(SUBLANES - 2, tm), :]
    cw = cw_ref[...]
    y_a = a_b * (cw[0:1] * z2 + cw[1:2] * z1 + cw[2:3] * zc)
    tail = zbuf_ref[tm:tm + SUBLANES, :]
    zbuf_ref[0:SUBLANES, :] = tail
    conv_ref[...] = tail[SUBLANES - (CONV_W - 1):, :]

    u = _gelu_tanh(z[:, 768:1024])
    v = _rms(_gelu_tanh(z[:, 1024:1280]), gn_ref[...]).astype(BF16)
    tri = (lax.broadcasted_iota(jnp.int32, (CHUNK, CHUNK), 0)
           >= lax.broadcasted_iota(jnp.int32, (CHUNK, CHUNK), 1))
    wt = [jnp.where(tri, ws_ref[gi], 0.0).astype(BF16) for gi in range(GMLP_GROUPS)]
    lane_grp = lax.broadcasted_iota(jnp.int32, (CHUNK, D_GMLP), 1) // (D_GMLP // GMLP_GROUPS)
    bias = bs_ref[...]
    yb = []
    for ci in range(tm // CHUNK):
        vch = v[ci * CHUNK:(ci + 1) * CHUNK]
        s = bias
        for gi in range(GMLP_GROUPS):
            s = s + jnp.where(lane_grp == gi, jnp.dot(wt[gi], vch, preferred_element_type=F32), 0.0)
        yb.append(u[ci * CHUNK:(ci + 1) * CHUNK] * s)
    y_b = jnp.concatenate(yb, axis=0)
    yab_ref[...] = jnp.concatenate([y_a, y_b], axis=1).astype(BF16)

    ones_bd = _head_group_ones(256)
    qn = qn_ref[...]
    scale = HEAD_DIM ** -0.5
    q = jnp.concatenate([_head_rms(z[:, 1280:1536], qn, ones_bd),
                         _head_rms(z[:, 1536:1792], qn, ones_bd)], axis=1)
    q_ref[...] = (q * scale).astype(BF16)
    kc_ref[...] = z[:, 1792:1920]
    vc_ref[...] = z[:, 1920:2048]
    vs = z[:, 2176:2304]
    vw = z[:, 2432:2560]
    kn = _head_rms(jnp.concatenate([z[:, 2048:2176], z[:, 2304:2432]], axis=1), kn_ref[...], ones_bd)
    ks, kw = kn[:, 0:128], kn[:, 128:256]
    ks_ref[...] = ks
    vs_ref[...] = vs

    if tm >= WINDOW:
        kww_ref[...] = kw[tm - WINDOW:, :]
        vww_ref[...] = vw[tm - WINDOW:, :]
    else:
        first = tiles_per_seq - WINDOW // tm

        @pl.when(j >= first)
        def _():
            off = pl.multiple_of((j - first) * tm, tm)
            kww_ref[pl.ds(off, tm), :] = kw
            vww_ref[pl.ds(off, tm), :] = vw

    pos = j * tm + lax.broadcasted_iota(jnp.int32, (1, tm), 1)
    prow = _pos_rows(pos)
    kst = ks.T.astype(BF16)
    kwt = kw.T.astype(BF16)
    n_blk_pad = kst_ref.shape[1] - 2 * HEAD_DIM
    blk_row = lax.broadcasted_iota(jnp.int32, (n_blk_pad, tm), 0)
    erows = jnp.where(blk_row == (pos >> 6), 1.0, 0.0).astype(BF16)
    for grp in range(N_KV_HEADS):
        sl = slice(grp * HEAD_DIM, (grp + 1) * HEAD_DIM)
        kst_ref[grp] = jnp.concatenate([erows, kst[sl], prow], axis=0)
        kwt_ref[grp] = jnp.concatenate([kwt[sl], prow], axis=0)
        vse_ref[grp] = _value_ext(vs, grp)
        vwe_ref[grp] = _value_ext(vw, grp)


def _in_proj_prompt(x, nb, seq, g, wm, wcg, cw, gn, ws, bs_tile, qn, kn12, tm):
    m, d = x.shape
    tps = seq // tm
    row = lambda i: (i, 0)
    rows = lambda w, dt: (pl.BlockSpec((tm, w), row), jax.ShapeDtypeStruct((m, w), dt))
    win = (pl.BlockSpec((None, WINDOW, D_KV), lambda i: (i // tps, 0, 0)),
           jax.ShapeDtypeStruct((nb, WINDOW, D_KV), F32))
    kt = (pl.BlockSpec((None, N_KV_HEADS, 2 * HEAD_DIM, tm), lambda i: (i // tps, 0, 0, i % tps)),
          jax.ShapeDtypeStruct((nb, N_KV_HEADS, 2 * HEAD_DIM, seq), BF16))
    ve = (pl.BlockSpec((None, N_KV_HEADS, tm, LANES), lambda i: (i // tps, 0, i % tps, 0)),
          jax.ShapeDtypeStruct((nb, N_KV_HEADS, seq, LANES), BF16))
    conv = (pl.BlockSpec((None, CONV_W - 1, D_CONV), lambda i: (i // tps, 0, 0)),
            jax.ShapeDtypeStruct((nb, CONV_W - 1, D_CONV), F32))
    n_krows = _n_blk_pad(seq) + 2 * HEAD_DIM
    kt_sel = (pl.BlockSpec((None, N_KV_HEADS, n_krows, tm), lambda i: (i // tps, 0, 0, i % tps)),
              jax.ShapeDtypeStruct((nb, N_KV_HEADS, n_krows, seq), BF16))
    outs = [rows(512, BF16), rows(512, BF16), rows(256, F32), rows(128, F32), rows(128, F32),
            rows(128, F32), rows(128, F32), win, win, conv, kt_sel, ve, kt, ve]
    return pl.pallas_call(
        functools.partial(_inproj_kernel, tm=tm, tiles_per_seq=tps),
        grid=(m // tm,),
        in_specs=[pl.BlockSpec((tm, d), row), _const_spec((1, d)), _const_spec(wm.shape),
                  _const_spec(wcg.shape), _const_spec(cw.shape), _const_spec(gn.shape),
                  _const_spec(ws.shape), _const_spec(bs_tile.shape), _const_spec(qn.shape),
                  _const_spec(kn12.shape)],
        out_specs=[o[0] for o in outs],
        out_shape=[o[1] for o in outs],
        scratch_shapes=[pltpu.VMEM((tm + SUBLANES, D_CONV), F32)],
        compiler_params=_cparams("arbitrary"),
        name="in_proj_prompt",
    )(x, g, wm, wcg, cw, gn, ws, bs_tile, qn, kn12)


def _compress_rows(src_ref, w, n_half):
    p0 = jnp.zeros((n_half, D_KV), F32)
    p1 = jnp.zeros((n_half, D_KV), F32)
    for s in range(CMP_STRIDE):
        xs = src_ref[pl.ds(s, n_half, stride=CMP_STRIDE), :]
        p0 = p0 + xs * w[s:s + 1]
        p1 = p1 + xs * w[CMP_STRIDE + s:CMP_STRIDE + s + 1]
    return p0, p1


def _combine_halves(p0, p1):
    n = p0.shape[0]
    row = lax.broadcasted_iota(jnp.int32, p0.shape, 0)
    return jnp.where(row < n - 1, p0 + pltpu.roll(p1, n - 1, axis=0), 0.0)


def _compress_kernel(kc_ref, vc_ref, wk_ref, wv_ref, kn_ref, kct_ref, vce_ref, *, n_half):
    kc = _combine_halves(*_compress_rows(kc_ref, wk_ref[...], n_half))
    vc = _combine_halves(*_compress_rows(vc_ref, wv_ref[...], n_half))
    kc = _head_rms(kc, kn_ref[...], _head_group_ones(D_KV))
    kct = kc.T.astype(BF16)
    cmp_end = lax.broadcasted_iota(jnp.int32, (1, n_half), 1) * CMP_STRIDE + (CMP_LEN - 1)
    prow = _pos_rows(cmp_end)
    for grp in range(N_KV_HEADS):
        kct_ref[grp] = jnp.concatenate([kct[grp * HEAD_DIM:(grp + 1) * HEAD_DIM], prow], axis=0)
        vce_ref[grp] = _value_ext(vc, grp)


def _compress_prompt(kc, vc, nb, seq, wk, wv, kn0):
    n_half = seq // CMP_STRIDE
    return pl.pallas_call(
        functools.partial(_compress_kernel, n_half=n_half),
        grid=(nb,),
        in_specs=[pl.BlockSpec((seq, D_KV), lambda b: (b, 0)), pl.BlockSpec((seq, D_KV), lambda b: (b, 0)),
                  _const_spec(wk.shape), _const_spec(wv.shape), _const_spec(kn0.shape)],
        out_specs=[pl.BlockSpec((None, N_KV_HEADS, 2 * HEAD_DIM, n_half), lambda b: (b, 0, 0, 0)),
                   pl.BlockSpec((None, N_KV_HEADS, n_half, LANES), lambda b: (b, 0, 0, 0))],
        out_shape=[jax.ShapeDtypeStruct((nb, N_KV_HEADS, 2 * HEAD_DIM, n_half), BF16),
                   jax.ShapeDtypeStruct((nb, N_KV_HEADS, n_half, LANES), BF16)],
        compiler_params=_cparams("arbitrary"),
        name="compress_prompt",
    )(kc, vc, wk, wv, kn0)


def _attn_kernel(q_ref, gate_ref, slope_ref, wb_ref, kct_ref, vce_ref, kst_ref, vse_ref, kwt_ref, vwe_ref, o_ref,
                 qx_ref, m_ref, acc_ref, oc_ref, ow_ref, list_ref, *, n_cmp_pad, n_blk_pad):
    qb = Q_BLOCK
    rows = HEADS_PER_KV * qb
    i = pl.program_id(2)
    p0 = i * qb

    q = q_ref[...].astype(F32)
    lane = lax.broadcasted_iota(jnp.int32, (qb, LANES), 1)
    parts = []
    for hp in range(HEADS_PER_KV):
        col = q[:, (hp // 2) * LANES:(hp // 2 + 1) * LANES]
        if hp % 2 == 1:
            col = _swap_halves(col)
        parts.append(jnp.where(lane < HEAD_DIM, col, 0.0))
    qx = (jnp.concatenate(parts, axis=0) + slope_ref[...]).astype(BF16)

    t_q = p0 + lax.broadcasted_iota(jnp.int32, (qb, 1), 0)
    t_rows = jnp.concatenate([t_q] * HEADS_PER_KV, axis=0)

    s_c = jnp.dot(qx, kct_ref[...], preferred_element_type=F32)
    n_win = WINDOW + qb
    wstart = pl.multiple_of(jnp.maximum(p0 - WINDOW, 0), qb)
    s_w = (jnp.dot(qx, kwt_ref[:, pl.ds(wstart, n_win)], preferred_element_type=F32)
           + jnp.concatenate([wb_ref[...]] * HEADS_PER_KV, axis=0))

    cmp_end = lax.broadcasted_iota(jnp.int32, (1, n_cmp_pad), 1) * CMP_STRIDE + (CMP_LEN - 1)
    vis = cmp_end <= t_rows
    s_c = jnp.where(vis, s_c, NEG)
    e_c = jnp.where(vis, jnp.exp(s_c - _rep(_row_max(s_c), s_c)), 0.0)
    p_c = e_c * _rep(1.0 / jnp.maximum(_row_sum(e_c), 1e-30), e_c)
    oc_ref[...] = jnp.dot(p_c.astype(BF16), vce_ref[...], preferred_element_type=F32)

    psum = p_c[0:qb]
    for hp in range(1, HEADS_PER_KV):
        psum = psum + p_c[hp * qb:(hp + 1) * qb]
    bidx = lax.broadcasted_iota(jnp.int32, (n_blk_pad, n_cmp_pad), 0)
    cidx = lax.broadcasted_iota(jnp.int32, (n_blk_pad, n_cmp_pad), 1)
    ratio = SEL_LEN // CMP_STRIDE
    band_t = jnp.where((cidx >= ratio * bidx - 1) & (cidx <= ratio * bidx + ratio - 1)
                       & (cidx < n_cmp_pad - 1), 1.0, 0.0).astype(BF16)
    nt = (((1,), (1,)), ((), ()))
    imp = sum(lax.dot_general(band_t, part, nt, preferred_element_type=F32)
              for part in _split3(psum))
    e_w = jnp.exp(s_w - _rep(_row_max(s_w), s_w))
    acc_w = jnp.dot(e_w.astype(BF16), vwe_ref[pl.ds(wstart, n_win), :], preferred_element_type=F32)
    ow_ref[...] = acc_w * (1.0 / _swap_halves(acc_w))

    blk = lax.broadcasted_iota(jnp.int32, (n_blk_pad, qb), 0)
    t_lane = p0 + lax.broadcasted_iota(jnp.int32, (1, qb), 1)
    cur = t_lane >> 6
    forced = (blk == 0) | (blk == cur) | (blk == cur - 1)
    future = blk * SEL_LEN > t_lane
    imp = jnp.where(forced, -jnp.inf, imp)
    imp = jnp.where(future, NEG, imp)
    sel_t = (forced | _topk_select_cols(imp, blk.astype(F32), TOP_N - 3)) & jnp.logical_not(future)
    selneg = jnp.where(sel_t, 0.0, -MASK_BIG).T.astype(BF16)

    last = p0 // KEY_CHUNK
    blocks_per_chunk = KEY_CHUNK // SEL_LEN
    sel_f = jnp.where(sel_t, 1.0, 0.0)
    n_list = jnp.int32(0)
    for c in range(list_ref.shape[0] - 1):
        used = jnp.max(sel_f[c * blocks_per_chunk:(c + 1) * blocks_per_chunk])
        active = jnp.logical_and(used > 0.0, c < last)
        list_ref[n_list] = jnp.where(active, c, last)
        n_list = n_list + active.astype(jnp.int32)
    list_ref[n_list] = last
    qx_ref[:, 0:n_blk_pad] = jnp.concatenate([selneg] * HEADS_PER_KV, axis=0)
    qx_ref[:, n_blk_pad:] = qx

    m_ref[...] = jnp.full((rows, LANES), NEG, F32)
    acc_ref[...] = jnp.zeros((rows, LANES), F32)

    def scores(c):
        start = pl.multiple_of(c * KEY_CHUNK, KEY_CHUNK)
        return jnp.dot(qx_ref[...], kst_ref[:, pl.ds(start, KEY_CHUNK)], preferred_element_type=F32)

    def softmax_update(s, c):
        start = pl.multiple_of(c * KEY_CHUNK, KEY_CHUNK)
        m_old = m_ref[...]
        m_new = jnp.maximum(m_old, _row_max(s))
        p = jnp.exp(s - _rep(m_new, s)).astype(BF16)
        acc_ref[...] = (jnp.exp(m_old - m_new) * acc_ref[...]
                        + jnp.dot(p, vse_ref[pl.ds(start, KEY_CHUNK), :], preferred_element_type=F32))
        m_ref[...] = m_new

    kpos = last * KEY_CHUNK + lax.broadcasted_iota(jnp.int32, (1, KEY_CHUNK), 1)
    causal = lambda s: jnp.where(kpos <= t_rows, s, NEG)

    def pair(ca, cb, last_is_diagonal):
        s_a, s_b = scores(ca), scores(cb)
        softmax_update(s_a, ca)
        softmax_update(causal(s_b) if last_is_diagonal else s_b, cb)

    def body(j, carry):
        pair(list_ref[2 * j], list_ref[2 * j + 1], False)
        return carry

    lax.fori_loop(0, n_list // 2, body, 0)

    @pl.when(n_list % 2 == 1)
    def _():
        pair(list_ref[n_list - 1], last, True)

    @pl.when(n_list % 2 == 0)
    def _():
        softmax_update(causal(scores(last)), last)

    acc_s = acc_ref[...]

    o_s = acc_s * (1.0 / _swap_halves(acc_s))
    o_c = oc_ref[...]
    o_w = ow_ref[...]
    gate = gate_ref[...]
    res = []
    for hp in range(HEADS_PER_KV):
        sl = slice(hp * qb, (hp + 1) * qb)
        gc = gate[:, N_BRANCH * hp + 0:N_BRANCH * hp + 1]
        gs = gate[:, N_BRANCH * hp + 1:N_BRANCH * hp + 2]
        gw = gate[:, N_BRANCH * hp + 2:N_BRANCH * hp + 3]
        res.append(gc * o_c[sl] + gs * o_s[sl] + gw * o_w[sl])
    cols = [jnp.where(lane < HEAD_DIM, res[2 * k], _swap_halves(res[2 * k + 1])) for k in range(2)]
    o_ref[...] = jnp.concatenate(cols, axis=1).astype(o_ref.dtype)


def _window_bias():
    r = np.arange(Q_BLOCK)[:, None]
    j = np.arange(WINDOW + Q_BLOCK)[None, :]
    early = [j <= Q_BLOCK * v + r for v in range(WINDOW // Q_BLOCK)]
    steady = (j >= r) & (j <= r + WINDOW)
    return jnp.asarray(np.where(np.stack(early + [steady]), 0.0, NEG), F32)


def _attention_prompt(q, gates, slope_rows, wbias, kct, vce, kst, vse, kwt, vwe, nb, seq):
    nq = seq // Q_BLOCK
    n_cmp_pad = kct.shape[-1]
    n_blk_pad = _n_blk_pad(seq)
    rows = HEADS_PER_KV * Q_BLOCK
    n_var = wbias.shape[0]
    qspec = pl.BlockSpec((Q_BLOCK, HEADS_PER_KV * HEAD_DIM), lambda b, g, i: (b * nq + i, g))
    per_bg = lambda shape: pl.BlockSpec((None, None) + shape, lambda b, g, i: (b, g, 0, 0))
    return pl.pallas_call(
        functools.partial(_attn_kernel, n_cmp_pad=n_cmp_pad, n_blk_pad=n_blk_pad),
        grid=(nb, N_KV_HEADS, nq),
        in_specs=[qspec,
                  pl.BlockSpec((Q_BLOCK, LANES), lambda b, g, i: (b * nq + i, g)),
                  pl.BlockSpec((None, rows, LANES), lambda b, g, i: (g, 0, 0)),
                  pl.BlockSpec((None,) + wbias.shape[1:], lambda b, g, i: (jnp.minimum(i, n_var - 1), 0, 0)),
                  per_bg((2 * HEAD_DIM, n_cmp_pad)), per_bg((n_cmp_pad, LANES)),
                  per_bg((n_blk_pad + 2 * HEAD_DIM, seq)), per_bg((seq, LANES)),
                  per_bg((2 * HEAD_DIM, seq)), per_bg((seq, LANES))],
        out_specs=qspec,
        out_shape=jax.ShapeDtypeStruct(q.shape, BF16),
        scratch_shapes=[pltpu.VMEM((rows, n_blk_pad + LANES), BF16), pltpu.VMEM((rows, LANES), F32),
                        pltpu.VMEM((rows, LANES), F32), pltpu.VMEM((rows, LANES), F32),
                        pltpu.VMEM((rows, LANES), F32), pltpu.SMEM((seq // KEY_CHUNK + 1,), jnp.int32)],
        compiler_params=_cparams("arbitrary", "arbitrary", "arbitrary"),
        name="attention_prompt",
    )(q, gates, slope_rows, wbias, kct, vce, kst, vse, kwt, vwe)


def _merge_kernel(x_ref, yab_ref, yc_ref, g_ref, wmg_ref, wb_ref, wo_ref, o_ref):
    x = x_ref[...]
    d = x.shape[1]
    h = _rms(x, g_ref[...]).astype(BF16)
    yab = yab_ref[...]
    branches = (jnp.dot(yab[:, 0:D_CONV], wb_ref[0:D_CONV, :], preferred_element_type=F32),
                jnp.dot(yab[:, D_CONV:], wb_ref[D_CONV:D_CONV + D_GMLP, :], preferred_element_type=F32),
                jnp.dot(yc_ref[...], wb_ref[D_CONV + D_GMLP:, :], preferred_element_type=F32))
    merged = None
    for k, y in enumerate(branches):
        gk = _sigmoid(jnp.dot(h, wmg_ref[:, k * d:(k + 1) * d], preferred_element_type=F32))
        merged = gk * y if merged is None else merged + gk * y
    o_ref[...] = x + jnp.dot(merged.astype(BF16), wo_ref[...], preferred_element_type=F32)


def _merge(x, yab, yc, g, wmg, wb, wo, tm):
    m, d = x.shape
    row = lambda i: (i, 0)
    return pl.pallas_call(
        _merge_kernel,
        grid=(m // tm,),
        in_specs=[pl.BlockSpec((tm, d), row), pl.BlockSpec((tm, yab.shape[1]), row),
                  pl.BlockSpec((tm, yc.shape[1]), row), _const_spec((1, d)), _const_spec(wmg.shape),
                  _const_spec(wb.shape), _const_spec(wo.shape)],
        out_specs=pl.BlockSpec((tm, d), row),
        out_shape=jax.ShapeDtypeStruct((m, d), F32),
        compiler_params=_cparams("arbitrary"),
        name="merge_out",
    )(x, yab, yc, g, wmg, wb, wo)


def _inproj_sample_kernel(x_ref, g_ref, wm_ref, wcg_ref, cw_ref, st0_ref, st1_ref, gn_ref, ws0_ref, bs0_ref,
                          qn_ref, kn_ref, yab_ref, q_ref, gate_ref, kc_ref, vc_ref, ks_ref, vs_ref, kw_ref,
                          vw_ref, zc_ref, vrow_ref):
    h = _rms(x_ref[...], g_ref[...]).astype(BF16)
    z = jnp.dot(h, wm_ref[...], preferred_element_type=F32)
    gate_ref[...] = _sigmoid(jnp.dot(h, wcg_ref[...], preferred_element_type=F32))
    a_b, a_c, a_x = z[:, 0:256], z[:, 256:512], z[:, 512:768]
    zc = a_c * a_x
    cw = cw_ref[...]
    y_a = a_b * (cw[0:1] * st0_ref[...] + cw[1:2] * st1_ref[...] + cw[2:3] * zc)
    zc_ref[...] = zc
    u = _gelu_tanh(z[:, 768:1024])
    v = _rms(_gelu_tanh(z[:, 1024:1280]), gn_ref[...])
    vrow_ref[...] = v
    y_b = u * (ws0_ref[...] * v + bs0_ref[...])
    yab_ref[...] = jnp.concatenate([y_a, y_b], axis=1).astype(BF16)
    ones_bd = _head_group_ones(256)
    qn = qn_ref[...]
    q = jnp.concatenate([_head_rms(z[:, 1280:1536], qn, ones_bd),
                         _head_rms(z[:, 1536:1792], qn, ones_bd)], axis=1)
    q_ref[...] = q * (HEAD_DIM ** -0.5)
    kc_ref[...] = z[:, 1792:1920]
    vc_ref[...] = z[:, 1920:2048]
    vs_ref[...] = z[:, 2176:2304]
    vw_ref[...] = z[:, 2432:2560]
    kn = _head_rms(jnp.concatenate([z[:, 2048:2176], z[:, 2304:2432]], axis=1), kn_ref[...], ones_bd)
    ks_ref[...] = kn[:, 0:128]
    kw_ref[...] = kn[:, 128:256]


def _in_proj_sample(x, g, wm, wcg, cw, st0, st1, gn, ws0, bs0, qn, kn12):
    m = x.shape[0]
    ins = (x, g, wm, wcg, cw, st0, st1, gn, ws0, bs0, qn, kn12)
    sd = lambda w, dt=F32: jax.ShapeDtypeStruct((m, w), dt)
    out_shape = [sd(512, BF16), sd(512), sd(256), sd(128), sd(128), sd(128), sd(128), sd(128), sd(128),
                 sd(256), sd(256)]
    return pl.pallas_call(
        _inproj_sample_kernel,
        grid=(1,),
        in_specs=[_const_spec(a.shape) for a in ins],
        out_specs=[_const_spec(s.shape) for s in out_shape],
        out_shape=out_shape,
        compiler_params=_cparams("arbitrary"),
        name="in_proj_sample",
    )(*ins)


def _head_slopes():
    hrow = lax.broadcasted_iota(jnp.int32, (N_HEADS, 1), 0)
    return lax.bitcast_convert_type((126 - hrow) << 23, F32)


def _sample_cmp_kernel(pt_ref, *refs, n_half, t_pos):
    del pt_ref
    pp = CMP_PAGES
    kpages, vpages = refs[0:pp], refs[pp:2 * pp]
    qz_ref, wt_ref, seg_ref, kn_ref = refs[2 * pp:2 * pp + 4]
    oc_ref, sel_ref = refs[2 * pp + 4:2 * pp + 6]
    p0k_ref, p1k_ref, p0v_ref, p1v_ref = refs[2 * pp + 6:]
    s = pl.program_id(1)
    halves = pp * PAGE_SIZE // CMP_STRIDE
    off = pl.multiple_of(s * halves, halves)
    seg = seg_ref[...]

    def half_sums(pages, w):
        lhs = jnp.concatenate([(r[...] * w).astype(BF16) for r in pages], axis=1)
        return jnp.dot(lhs, seg, preferred_element_type=F32)

    p0k_ref[:, pl.ds(off, halves)] = half_sums(kpages, wt_ref[0])
    p1k_ref[:, pl.ds(off, halves)] = half_sums(kpages, wt_ref[1])
    p0v_ref[:, pl.ds(off, halves)] = half_sums(vpages, wt_ref[2])
    p1v_ref[:, pl.ds(off, halves)] = half_sums(vpages, wt_ref[3])

    @pl.when(s == pl.num_programs(1) - 1)
    def _():
        col = lax.broadcasted_iota(jnp.int32, (D_KV, n_half), 1)
        combine = lambda p0, p1: jnp.where(col < n_half - 1, p0 + pltpu.roll(p1, n_half - 1, axis=1), 0.0)
        kc = combine(p0k_ref[...], p1k_ref[...])
        vc = combine(p0v_ref[...], p1v_ref[...])
        frow = lax.broadcasted_iota(jnp.int32, (D_KV, n_half), 0)
        sq = kc * kc
        ss0 = jnp.sum(jnp.where(frow < HEAD_DIM, sq, 0.0), axis=0, keepdims=True)
        ss1 = jnp.sum(jnp.where(frow >= HEAD_DIM, sq, 0.0), axis=0, keepdims=True)
        inv = lax.rsqrt(jnp.where(frow < HEAD_DIM, ss0, ss1) * (1.0 / HEAD_DIM) + EPS)
        kc = kc * inv * kn_ref[...]
        qz = qz_ref[...].astype(BF16)
        s_c = jnp.dot(qz, kc.astype(BF16), preferred_element_type=F32)
        cmp_end = lax.broadcasted_iota(jnp.int32, (1, n_half), 1) * CMP_STRIDE + (CMP_LEN - 1)
        d_c = t_pos - cmp_end
        vis = d_c >= 0
        s_c = jnp.where(vis, s_c - _head_slopes() * d_c.astype(F32), NEG)
        e_c = jnp.where(vis, jnp.exp(s_c - jnp.max(s_c, axis=1, keepdims=True)), 0.0)
        p_c = e_c * (1.0 / jnp.maximum(jnp.sum(e_c, axis=1, keepdims=True), 1e-30))
        nt = (((1,), (1,)), ((), ()))
        oc_ref[...] = lax.dot_general(p_c.astype(BF16), vc.astype(BF16), nt, preferred_element_type=F32)

        hrow = lax.broadcasted_iota(jnp.int32, p_c.shape, 0)
        ps0 = jnp.sum(jnp.where(hrow < HEADS_PER_KV, p_c, 0.0), axis=0, keepdims=True)
        ps1 = jnp.sum(jnp.where(hrow >= HEADS_PER_KV, p_c, 0.0), axis=0, keepdims=True)
        prow = lax.broadcasted_iota(jnp.int32, (LANES, n_half), 0)
        psum = jnp.where(prow == 0, ps0, jnp.where(prow == 1, ps1, 0.0))
        n_blk_pad = sel_ref.shape[1]
        n_sel = t_pos // SEL_LEN + 1
        bidx = lax.broadcasted_iota(jnp.int32, (n_blk_pad, n_half), 0)
        cidx = lax.broadcasted_iota(jnp.int32, (n_blk_pad, n_half), 1)
        ratio = SEL_LEN // CMP_STRIDE
        band_t = jnp.where((cidx >= ratio * bidx - 1) & (cidx <= ratio * bidx + ratio - 1)
                           & (cidx < n_half - 1), 1.0, 0.0).astype(BF16)
        imp = sum(lax.dot_general(band_t, part, nt, preferred_element_type=F32) for part in _split3(psum))
        blk = lax.broadcasted_iota(jnp.int32, imp.shape, 0)
        cur = t_pos // SEL_LEN
        forced = (blk == 0) | (blk == cur) | (blk == cur - 1)
        future = blk * SEL_LEN > t_pos
        imp = jnp.where(forced, -NEG, imp)
        imp = jnp.where(future, NEG, imp)
        imp = jnp.where(blk < n_sel, imp, -jnp.inf)
        sel_t = (_topk_select_cols(imp, blk.astype(F32), min(TOP_N, n_sel))
                 & jnp.logical_not(future) & (blk < n_sel))
        sel_ref[...] = jnp.where(sel_t, 1.0, 0.0).T[0:SUBLANES]


def _page_specs(layer, n, table_col):
    def spec(k):
        return pl.BlockSpec((None, None, D_KV, PAGE_SIZE),
                            lambda b, s, *tabs: (layer, table_col(tabs, b, s * n + k), 0, 0))
    return [spec(k) for k in range(n)]


def _sample_cmp(page_table, cache_k, cache_v, layer, qz, wt, seg, kn0_col, t_pos):
    nb, n_pages = page_table.shape
    n_half = n_pages * PAGE_SIZE // CMP_STRIDE
    n_blk_pad = -(-(t_pos // SEL_LEN + 1) // LANES) * LANES
    pp = CMP_PAGES
    per_b = lambda shape: pl.BlockSpec((None,) + shape, lambda b, s, pt: (b, 0, 0))
    const = lambda shape: pl.BlockSpec(shape, lambda b, s, pt: (0,) * len(shape))
    pages = lambda: _page_specs(layer, pp, lambda tabs, b, j: tabs[0][b, j])
    grid_spec = pltpu.PrefetchScalarGridSpec(
        num_scalar_prefetch=1,
        grid=(nb, n_pages // pp),
        in_specs=pages() + pages()
        + [per_b((N_HEADS, LANES)), const(wt.shape), const(seg.shape), const(kn0_col.shape)],
        out_specs=[per_b((N_HEADS, LANES)), per_b((SUBLANES, n_blk_pad))],
        scratch_shapes=[pltpu.VMEM((D_KV, n_half), F32)] * 4,
    )
    return pl.pallas_call(
        functools.partial(_sample_cmp_kernel, n_half=n_half, t_pos=t_pos),
        grid_spec=grid_spec,
        out_shape=[jax.ShapeDtypeStruct((nb, N_HEADS, LANES), F32),
                   jax.ShapeDtypeStruct((nb, SUBLANES, n_blk_pad), F32)],
        compiler_params=_cparams("arbitrary", "arbitrary"),
        name="sample_cmp",
    )(page_table, *([cache_k] * pp), *([cache_v] * pp), qz, wt, seg, kn0_col)


def _sample_sel_kernel(phys_ref, lp_ref, *refs, t_pos):
    del phys_ref
    pp = SEL_PAGES
    kpages, vpages = refs[0:pp], refs[pp:2 * pp]
    qz_ref, sel_ref, kn_ref, vn_ref, o_ref, m_ref, l_ref, acc_ref = refs[2 * pp:]
    b, s = pl.program_id(0), pl.program_id(1)
    n_keys = pp * PAGE_SIZE

    @pl.when(s == 0)
    def _():
        m_ref[...] = jnp.full(m_ref.shape, NEG, F32)
        l_ref[...] = jnp.zeros(l_ref.shape, F32)
        acc_ref[...] = jnp.zeros(acc_ref.shape, F32)

    lane = lax.broadcasted_iota(jnp.int32, (1, PAGE_SIZE), 1)
    lps = [lp_ref[b, s * pp + k] for k in range(pp)]
    kpos = jnp.concatenate([lp * PAGE_SIZE + lane for lp in lps], axis=1)
    kblk = jnp.concatenate([lp * (PAGE_SIZE // SEL_LEN) + lane // SEL_LEN for lp in lps], axis=1)
    qz = qz_ref[...].astype(BF16)
    kk = jnp.concatenate([r[...] for r in kpages], axis=1).astype(BF16)
    vv = jnp.concatenate([r[...] for r in vpages], axis=1).astype(BF16)
    sc = jnp.dot(qz, kk, preferred_element_type=F32)
    sc = sc - _head_slopes() * (t_pos - kpos).astype(F32)
    n_blk_pad = sel_ref.shape[1]
    hrow = lax.broadcasted_iota(jnp.int32, (N_HEADS, n_blk_pad), 0)
    selv = sel_ref[...]
    sel_h = jnp.where(hrow < HEADS_PER_KV, selv[0:1], selv[1:2]).astype(BF16)
    erow = lax.broadcasted_iota(jnp.int32, (n_blk_pad, n_keys), 0)
    ok = jnp.dot(sel_h, jnp.where(erow == kblk, 1.0, 0.0).astype(BF16), preferred_element_type=F32) > 0.5
    sc = jnp.where(ok, sc, NEG)
    m_old = m_ref[...]
    m_new = jnp.maximum(m_old, jnp.max(sc, axis=1, keepdims=True))
    p = jnp.where(ok, jnp.exp(sc - m_new), 0.0)
    alpha = jnp.exp(m_old - m_new)
    l_ref[...] = alpha * l_ref[...] + jnp.sum(p, axis=1, keepdims=True)
    acc_ref[...] = alpha * acc_ref[...] + lax.dot_general(p.astype(BF16), vv, (((1,), (1,)), ((), ())),
                                                           preferred_element_type=F32)
    m_ref[...] = m_new

    @pl.when(s == pl.num_programs(1) - 1)
    def _():
        k_new = kn_ref[...].astype(BF16).astype(F32)
        s_new = jnp.sum(qz.astype(F32) * k_new, axis=1, keepdims=True)
        m_o = m_ref[...]
        m_n = jnp.maximum(m_o, s_new)
        a = jnp.exp(m_o - m_n)
        p_new = jnp.exp(s_new - m_n)
        l = a * l_ref[...] + p_new
        acc = a * acc_ref[...] + p_new.astype(BF16).astype(F32) * vn_ref[...].astype(BF16).astype(F32)
        o_ref[...] = acc * (1.0 / l)


def _needed_pages(sel, page_table):
    nb, n_pages = page_table.shape
    per_page = PAGE_SIZE // SEL_LEN
    flags = sel[:, :N_KV_HEADS, :n_pages * per_page] > 0.5
    need = flags.reshape(nb, N_KV_HEADS, n_pages, per_page).any(axis=(1, 3))
    n_slots = min(SEL_SLOTS, n_pages)
    order = jnp.argsort(jnp.logical_not(need), axis=1, stable=True)[:, :n_slots]
    count = need.sum(axis=1, keepdims=True)
    valid = jnp.arange(n_slots, dtype=jnp.int32)[None, :] < count
    logical = jnp.where(valid, order, -1).astype(jnp.int32)
    phys = jnp.take_along_axis(page_table, jnp.maximum(logical, 0), axis=1).astype(jnp.int32)
    return phys, logical


def _sample_sel(phys, logical, cache_k, cache_v, layer, qz, sel, ks_new, vs_new, t_pos):
    nb = phys.shape[0]
    pp = SEL_PAGES
    per_b = lambda shape: pl.BlockSpec((None,) + shape, lambda b, s, ph, lp: (b, 0, 0))
    pages = lambda: _page_specs(layer, pp, lambda tabs, b, j: tabs[0][b, j])
    grid_spec = pltpu.PrefetchScalarGridSpec(
        num_scalar_prefetch=2,
        grid=(nb, phys.shape[1] // pp),
        in_specs=pages() + pages()
        + [per_b((N_HEADS, LANES)), per_b(sel.shape[1:]), per_b((1, D_KV)), per_b((1, D_KV))],
        out_specs=per_b((N_HEADS, LANES)),
        scratch_shapes=[pltpu.VMEM((N_HEADS, 1), F32), pltpu.VMEM((N_HEADS, 1), F32),
                        pltpu.VMEM((N_HEADS, LANES), F32)],
    )
    return pl.pallas_call(
        functools.partial(_sample_sel_kernel, t_pos=t_pos),
        grid_spec=grid_spec,
        out_shape=jax.ShapeDtypeStruct((nb, N_HEADS, LANES), F32),
        compiler_params=_cparams("arbitrary", "arbitrary"),
        name="sample_sel",
    )(phys, logical, *([cache_k] * pp), *([cache_v] * pp), qz, sel, ks_new, vs_new)


def _sample_win_kernel(qz_ref, wk_ref, wv_ref, kn_ref, vn_ref, gate_ref, oc_ref, os_ref,
                       y_ref, wko_ref, wvo_ref):
    qz = qz_ref[...].astype(BF16)
    kwin, vwin = wk_ref[...], wv_ref[...]
    n_win = kwin.shape[1]
    nt = (((1,), (1,)), ((), ()))
    sc = jnp.dot(qz, kwin.astype(BF16), preferred_element_type=F32)
    dist = n_win - lax.broadcasted_iota(jnp.int32, (1, n_win), 1)
    sc = sc - _head_slopes() * dist.astype(F32)
    k_new, v_new = kn_ref[...], vn_ref[...]
    s_new = jnp.sum(qz.astype(F32) * k_new.astype(BF16).astype(F32), axis=1, keepdims=True)
    m = jnp.maximum(jnp.max(sc, axis=1, keepdims=True), s_new)
    p = jnp.exp(sc - m)
    p_new = jnp.exp(s_new - m)
    l = jnp.sum(p, axis=1, keepdims=True) + p_new
    acc = (lax.dot_general(p.astype(BF16), vwin.astype(BF16), nt, preferred_element_type=F32)
           + p_new.astype(BF16).astype(F32) * v_new.astype(BF16).astype(F32))
    o_w = acc * (1.0 / l)
    gate = gate_ref[...]
    y = gate[:, 0:1] * oc_ref[...] + gate[:, 1:2] * os_ref[...] + gate[:, 2:3] * o_w
    lane = lax.broadcasted_iota(jnp.int32, (1, LANES), 1)
    cols = []
    for k in range(N_HEADS // 2):
        grp = (2 * k) // HEADS_PER_KV
        even, odd = y[2 * k:2 * k + 1], y[2 * k + 1:2 * k + 2]
        low = even if grp == 0 else _swap_halves(even)
        high = odd if grp == 1 else _swap_halves(odd)
        cols.append(jnp.where(lane < HEAD_DIM, low, high))
    y_ref[...] = jnp.concatenate(cols, axis=1).astype(y_ref.dtype)
    eye = (lax.broadcasted_iota(jnp.int32, (D_KV, D_KV), 0) == lax.broadcasted_iota(jnp.int32, (D_KV, D_KV), 1))
    as_col = lambda r: jnp.sum(jnp.where(eye, r, 0.0), axis=1, keepdims=True)
    pos = lax.broadcasted_iota(jnp.int32, kwin.shape, 1)
    wko_ref[...] = jnp.where(pos == n_win - 1, as_col(k_new), pltpu.roll(kwin, n_win - 1, axis=1))
    wvo_ref[...] = jnp.where(pos == n_win - 1, as_col(v_new), pltpu.roll(vwin, n_win - 1, axis=1))


def _sample_win(qz, win_k, win_v, layer, kw_new, vw_new, gates_h, o_c, o_s):
    nb = qz.shape[0]
    n_win = win_k.shape[3]
    per_b = lambda shape: pl.BlockSpec((None,) + shape, lambda b: (b, 0, 0))
    cache = pl.BlockSpec((None, None, D_KV, n_win), lambda b: (layer, b, 0, 0))
    hl = (N_HEADS, LANES)
    return pl.pallas_call(
        _sample_win_kernel,
        grid=(nb,),
        in_specs=[per_b(hl), cache, cache, per_b((1, D_KV)), per_b((1, D_KV)), per_b(hl), per_b(hl), per_b(hl)],
        out_specs=[per_b((1, D_ATTN)), per_b((D_KV, n_win)), per_b((D_KV, n_win))],
        out_shape=[jax.ShapeDtypeStruct((nb, 1, D_ATTN), F32), jax.ShapeDtypeStruct((nb, D_KV, n_win), F32),
                   jax.ShapeDtypeStruct((nb, D_KV, n_win), F32)],
        compiler_params=_cparams("arbitrary"),
        name="sample_win",
    )(qz, win_k, win_v, kw_new, vw_new, gates_h, o_c, o_s)


def _slope_rows():
    out = np.zeros((N_KV_HEADS, HEADS_PER_KV * Q_BLOCK, LANES), np.float32)
    for g in range(N_KV_HEADS):
        for hp in range(HEADS_PER_KV):
            slope = 2.0 ** -(g * HEADS_PER_KV + hp + 1)
            out[g, hp * Q_BLOCK:(hp + 1) * Q_BLOCK, HEAD_DIM] = slope * 128.0
            out[g, hp * Q_BLOCK:(hp + 1) * Q_BLOCK, HEAD_DIM + 1] = slope
    return jnp.asarray(out)


def _tile_lanes(v, reps):
    return jnp.tile(v.reshape(1, -1), (1, reps))


def _heads_to_rows(q):
    n = q.shape[0]
    qh = q.reshape(n, N_HEADS, HEAD_DIM)
    z = jnp.zeros_like(qh[:, :HEADS_PER_KV])
    return jnp.concatenate([jnp.concatenate([qh[:, :HEADS_PER_KV], z], axis=-1),
                            jnp.concatenate([z, qh[:, HEADS_PER_KV:]], axis=-1)], axis=1)


def kernel(x_prompt, x_sample, cache_cmp_k, cache_cmp_v, cache_sel_k, cache_sel_v, cache_win_k, cache_win_v, state_conv, page_table, ffn1_norm, ffn1_w_gate, ffn1_w_up, ffn1_w_down, mix_norm, w_in, conv_w, gmlp_norm, gmlp_ws, gmlp_bs, q_norm, k_norm, cmp_wk, cmp_wv, w_branch, w_out, ffn2_norm, ffn2_w_gate, ffn2_w_up, ffn2_w_down):
    nb, seq, d = x_prompt.shape
    ns = x_sample.shape[0]
    depth = w_in.shape[0]
    t_pos = page_table.shape[1] * PAGE_SIZE
    assert x_sample.shape[1] == 1 and cache_win_k.shape[2] == WINDOW
    assert seq % (CMP_STRIDE * LANES) == 0 and t_pos % (CMP_STRIDE * LANES) == 0
    assert page_table.shape[1] % CMP_PAGES == 0
    tm = min(ROW_TILE, seq)

    xp = x_prompt.reshape(nb * seq, d)
    xs = x_sample.reshape(ns, d)
    feat_major = lambda c: jnp.transpose(c, (0, 1, 3, 4, 2)).reshape(depth, c.shape[1], D_KV, c.shape[2])
    ck, cv, sk, sv = (feat_major(c) for c in (cache_cmp_k, cache_cmp_v, cache_sel_k, cache_sel_v))
    wink, winv = feat_major(cache_win_k), feat_major(cache_win_v)
    slope_rows = _slope_rows()
    wbias = _window_bias()
    seg_rows = CMP_PAGES * PAGE_SIZE
    seg = jnp.asarray(np.arange(seg_rows)[:, None] // CMP_STRIDE == np.arange(seg_rows // CMP_STRIDE)[None, :], BF16)

    prompt_new = [[] for _ in range(7)]
    sample_new = [[] for _ in range(8)]
    for l in range(depth):
        bf = lambda w: w.astype(BF16)
        w_main = bf(w_in[l, :, :MAIN_COLS])
        cg = w_in[l, :, MAIN_COLS:MAIN_COLS + N_BRANCH * N_HEADS].reshape(d, N_KV_HEADS, HEADS_PER_KV * N_BRANCH)
        w_cg = bf(jnp.pad(cg, ((0, 0), (0, 0), (0, LANES - HEADS_PER_KV * N_BRANCH))).reshape(d, N_KV_HEADS * LANES))
        w_mg = bf(w_in[l, :, MAIN_COLS + N_BRANCH * N_HEADS:])
        wb, wo = bf(w_branch[l]), bf(w_out[l])
        f1 = (ffn1_norm[l].reshape(1, d), bf(ffn1_w_gate[l]), bf(ffn1_w_up[l]), bf(ffn1_w_down[l]))
        f2 = (ffn2_norm[l].reshape(1, d), bf(ffn2_w_gate[l]), bf(ffn2_w_up[l]), bf(ffn2_w_down[l]))
        mn = mix_norm[l].reshape(1, d)
        cw = conv_w[l]
        gn = gmlp_norm[l].reshape(1, D_GMLP)
        gdim = D_GMLP // GMLP_GROUPS
        bs_tile = jnp.repeat(gmlp_bs[l].T, gdim, axis=1)
        ws0 = jnp.repeat(gmlp_ws[l, :, 0, 0], gdim).reshape(1, D_GMLP)
        bs0 = jnp.repeat(gmlp_bs[l, :, 0], gdim).reshape(1, D_GMLP)
        qn = _tile_lanes(q_norm[l], 4)
        kn0 = _tile_lanes(k_norm[l, 0], 2)
        kn12 = jnp.concatenate([_tile_lanes(k_norm[l, 1], 2), _tile_lanes(k_norm[l, 2], 2)], axis=1)
        wk = cmp_wk[l].reshape(CMP_LEN, D_KV)
        wv = cmp_wv[l].reshape(CMP_LEN, D_KV)
        taps = lambda w: jnp.tile(w.T, (1, PAGE_SIZE // CMP_STRIDE))
        wt = jnp.stack([taps(wk[:CMP_STRIDE]), taps(wk[CMP_STRIDE:]), taps(wv[:CMP_STRIDE]), taps(wv[CMP_STRIDE:])])

        xp = _half_ffn(xp, *f1, tm)
        (yab, q, gates, kc, vc, ks, vs, kww, vww, conv_new, kst, vse, kwt, vwe) = _in_proj_prompt(
            xp, nb, seq, mn, w_main, w_cg, cw, gn, gmlp_ws[l], bs_tile, qn, kn12, tm)
        kct, vce = _compress_prompt(kc, vc, nb, seq, wk, wv, kn0)
        yc = _attention_prompt(q, gates, slope_rows, wbias, kct, vce, kst, vse, kwt, vwe, nb, seq)
        xp = _merge(xp, yab, yc, mn, w_mg, wb, wo, tm)
        xp = _half_ffn(xp, *f2, tm)
        kv5 = lambda a: a.reshape(nb, -1, N_KV_HEADS, HEAD_DIM)
        for lst, a in zip(prompt_new, (kv5(kc), kv5(vc), kv5(ks), kv5(vs), kv5(kww), kv5(vww), conv_new)):
            lst.append(a)

        xs = _half_ffn(xs, *f1, ns)
        (yab_s, q_s, gates_s, kc_s, vc_s, ks_s, vs_s, kw_s, vw_s, zc_s, vrow_s) = _in_proj_sample(
            xs, mn, w_main, w_cg, cw, state_conv[l, :, 0], state_conv[l, :, 1], gn, ws0, bs0, qn, kn12)
        qz = _heads_to_rows(q_s)
        o_c, sel = _sample_cmp(page_table, ck, cv, l, qz, wt, seg, kn0.reshape(D_KV, 1), t_pos)
        phys, logical = _needed_pages(sel, page_table)
        o_s = _sample_sel(phys, logical, sk, sv, l, qz, sel, ks_s.reshape(ns, 1, D_KV),
                          vs_s.reshape(ns, 1, D_KV), t_pos)
        gh = gates_s.reshape(ns, N_KV_HEADS, LANES)[:, :, :HEADS_PER_KV * N_BRANCH].reshape(ns, N_HEADS, N_BRANCH)
        gh = jnp.pad(gh, ((0, 0), (0, 0), (0, LANES - N_BRANCH)))
        yc_s, wk_new, wv_new = _sample_win(qz, wink, winv, l, kw_s.reshape(ns, 1, D_KV),
                                           vw_s.reshape(ns, 1, D_KV), gh, o_c, o_s)
        yc_s = yc_s.reshape(ns, D_ATTN).astype(BF16)
        xs = _merge(xs, yab_s, yc_s, mn, w_mg, wb, wo, ns)
        xs = _half_ffn(xs, *f2, ns)
        kv5s = lambda a: a.reshape(ns, -1, N_KV_HEADS, HEAD_DIM)
        conv_s = jnp.stack([state_conv[l, :, 1], zc_s], axis=1)
        rows_major = lambda a: jnp.transpose(a.reshape(ns, N_KV_HEADS, HEAD_DIM, -1), (0, 3, 1, 2))
        for lst, a in zip(sample_new, (kv5s(kc_s), kv5s(vc_s), kv5s(ks_s), kv5s(vs_s), rows_major(wk_new),
                                       rows_major(wv_new), conv_s, vrow_s.reshape(ns, 1, D_GMLP))):
            lst.append(a)

    outs_p = [jnp.stack(a) for a in prompt_new]
    outs_s = [jnp.stack(a) for a in sample_new]
    return (xp.reshape(nb, seq, d), xs.reshape(ns, 1, d), *outs_p, *outs_s)
```

```python
import functools

import numpy as np
import jax
import jax.numpy as jnp
from jax import lax
from jax.experimental import pallas as pl
from jax.experimental.pallas import tpu as pltpu

F32 = jnp.float32
BF16 = jnp.bfloat16

HEAD_DIM = 64
N_HEADS = 8
N_KV_HEADS = 2
HEADS_PER_KV = N_HEADS // N_KV_HEADS
D_CONV = 256
CONV_W = 3
D_GMLP = 256
GMLP_GROUPS = 4
CHUNK = 128
D_ATTN = N_HEADS * HEAD_DIM
D_KV = N_KV_HEADS * HEAD_DIM
CMP_LEN = 32
CMP_STRIDE = 16
SEL_LEN = 64
TOP_N = 16
WINDOW = 512
Q_BLOCK = 128
N_BRANCH = 3
PAGE_SIZE = 128
EPS = 1e-6
NEG = -1e30
MASK_BIG = 2.0 ** 100
MAIN_COLS = 5 * 256 + D_ATTN + 6 * D_KV

LANES = 128
SUBLANES = 8
ROW_TILE = 512
FF_CHUNK = 256
KEY_CHUNK = 512
CMP_PAGES = 16
SEL_PAGES = 8
SEL_SLOTS = 32
VMEM_LIMIT = 56 * 1024 * 1024


def _cparams(*sem):
    return pltpu.CompilerParams(dimension_semantics=sem, vmem_limit_bytes=VMEM_LIMIT)


def _const_spec(shape):
    nd = len(shape)
    return pl.BlockSpec(shape, lambda *_: (0,) * nd, pipeline_mode=pl.Buffered(1))


def _rms(x, g):
    return x * lax.rsqrt(jnp.mean(x * x, axis=-1, keepdims=True) + EPS) * g


def _sigmoid(x):
    return 1.0 / (1.0 + jnp.exp(-x))


def _gelu_tanh(x):
    return 0.5 * x * (1.0 + jnp.tanh(0.7978845608028654 * (x + 0.044715 * (x * x * x))))


def _split3(x):
    hi = x.astype(BF16)
    r = x - hi.astype(F32)
    mid = r.astype(BF16)
    lo = (r - mid.astype(F32)).astype(BF16)
    return hi, mid, lo


def _exact_dot01(x, m01):
    hi, mid, lo = _split3(x)
    return (jnp.dot(hi, m01, preferred_element_type=F32) + jnp.dot(mid, m01, preferred_element_type=F32)
            + jnp.dot(lo, m01, preferred_element_type=F32))


def _head_group_ones(n):
    r = lax.broadcasted_iota(jnp.int32, (n, n), 0) // HEAD_DIM
    c = lax.broadcasted_iota(jnp.int32, (n, n), 1) // HEAD_DIM
    return jnp.where(r == c, 1.0, 0.0).astype(BF16)


def _head_rms(x, g, ones_bd):
    ssq = _exact_dot01(x * x, ones_bd)
    return x * lax.rsqrt(ssq * (1.0 / HEAD_DIM) + EPS) * g


def _swap_halves(x):
    return pltpu.roll(x, HEAD_DIM, axis=1)


def _value_ext(v, grp):
    lane = lax.broadcasted_iota(jnp.int32, v.shape, 1)
    src = v if grp == 0 else _swap_halves(v)
    return jnp.where(lane < HEAD_DIM, src, 1.0).astype(BF16)


def _pos_rows(pos):
    n = pos.shape[1]
    row = lax.broadcasted_iota(jnp.int32, (HEAD_DIM, n), 0)
    hi = (pos >> 7).astype(F32)
    lo = (pos & 127).astype(F32)
    return jnp.where(row == 0, hi, jnp.where(row == 1, lo, 0.0)).astype(BF16)


def _n_blk_pad(seq):
    return -(-(seq // SEL_LEN) // LANES) * LANES


def _fold_lane_tiles(x, op):
    t = x[:, 0:LANES]
    for k in range(1, x.shape[1] // LANES):
        t = op(t, x[:, k * LANES:(k + 1) * LANES])
    return t


def _row_max(x):
    t = _fold_lane_tiles(x, jnp.maximum)
    return jnp.broadcast_to(jnp.max(t, axis=1, keepdims=True), t.shape)


def _row_sum(x):
    t = _fold_lane_tiles(x, jnp.add)
    return jnp.broadcast_to(jnp.sum(t, axis=1, keepdims=True), t.shape)


def _rep(m, like):
    return jnp.concatenate([m] * (like.shape[1] // LANES), axis=1)


def _topk_select_cols(imp, blk_f, n_iter):
    sel = jnp.zeros(imp.shape, dtype=jnp.bool_)
    for _ in range(n_iter):
        m = jnp.max(imp, axis=0, keepdims=True)
        idx = jnp.min(jnp.where(imp == m, blk_f, float(imp.shape[0])), axis=0, keepdims=True)
        pick = blk_f == idx
        sel = jnp.logical_or(sel, pick)
        imp = jnp.where(pick, -jnp.inf, imp)
    return sel


def _ffn_kernel(x_ref, g_ref, wg_ref, wu_ref, wd_ref, o_ref, acc_ref):
    x = x_ref[...]
    h = _rms(x, g_ref[...]).astype(BF16)
    d_ff = wg_ref.shape[1]
    for c in range(d_ff // FF_CHUNK):
        sl = slice(c * FF_CHUNK, (c + 1) * FF_CHUNK)
        gate = jnp.dot(h, wg_ref[:, sl], preferred_element_type=F32)
        up = jnp.dot(h, wu_ref[:, sl], preferred_element_type=F32)
        a = (gate * _sigmoid(gate) * up).astype(BF16)
        part = jnp.dot(a, wd_ref[sl, :], preferred_element_type=F32)
        if c == 0:
            acc_ref[...] = part
        else:
            acc_ref[...] += part
    o_ref[...] = x + 0.5 * acc_ref[...]


def _half_ffn(x, g, wg, wu, wd, tm):
    m, d = x.shape
    d_ff = wg.shape[1]
    return pl.pallas_call(
        _ffn_kernel,
        grid=(m // tm,),
        in_specs=[pl.BlockSpec((tm, d), lambda i: (i, 0)), _const_spec((1, d)),
                  _const_spec((d, d_ff)), _const_spec((d, d_ff)), _const_spec((d_ff, d))],
        out_specs=pl.BlockSpec((tm, d), lambda i: (i, 0)),
        out_shape=jax.ShapeDtypeStruct((m, d), F32),
        scratch_shapes=[pltpu.VMEM((tm, d), F32)],
        compiler_params=_cparams("arbitrary"),
        name="half_ffn",
    )(x, g, wg, wu, wd)


def _inproj_kernel(x_ref, g_ref, wm_ref, wcg_ref, cw_ref, gn_ref, ws_ref, bs_ref, qn_ref, kn_ref,
                   yab_ref, q_ref, gate_ref, kc_ref, vc_ref, ks_ref, vs_ref, kww_ref, vww_ref, conv_ref,
                   kst_ref, vse_ref, kwt_ref, vwe_ref, zbuf_ref, *, tm, tiles_per_seq):
    j = pl.program_id(0) % tiles_per_seq
    h = _rms(x_ref[...], g_ref[...]).astype(BF16)
    z = jnp.dot(h, wm_ref[...], preferred_element_type=F32)
    gate_ref[...] = _sigmoid(jnp.dot(h, wcg_ref[...], preferred_element_type=F32))

    a_b, a_c, a_x = z[:, 0:256], z[:, 256:512], z[:, 512:768]
    zc = a_c * a_x

    @pl.when(j == 0)
    def _():
        zbuf_ref[0:SUBLANES, :] = jnp.zeros((SUBLANES, D_CONV), F32)

    zbuf_ref[SUBLANES:SUBLANES + tm, :] = zc
    z1 = zbuf_ref[pl.ds(SUBLANES - 1, tm), :]
    z2 = zbuf_ref[pl.ds(SUBLANES - 2, tm), :]
    cw = cw_ref[...]
    y_a = a_b * (cw[0:1] * z2 + cw[1:2] * z1 + cw[2:3] * zc)
    tail = zbuf_ref[tm:tm + SUBLANES, :]
    zbuf_ref[0:SUBLANES, :] = tail
    conv_ref[...] = tail[SUBLANES - (CONV_W - 1):, :]

    u = _gelu_tanh(z[:, 768:1024])
    v = _rms(_gelu_tanh(z[:, 1024:1280]), gn_ref[...]).astype(BF16)
    tri = (lax.broadcasted_iota(jnp.int32, (CHUNK, CHUNK), 0)
           >= lax.broadcasted_iota(jnp.int32, (CHUNK, CHUNK), 1))
    wt = [jnp.where(tri, ws_ref[gi], 0.0).astype(BF16) for gi in range(GMLP_GROUPS)]
    lane_grp = lax.broadcasted_iota(jnp.int32, (CHUNK, D_GMLP), 1) // (D_GMLP // GMLP_GROUPS)
    bias = bs_ref[...]
    yb = []
    for ci in range(tm // CHUNK):
        vch = v[ci * CHUNK:(ci + 1) * CHUNK]
        s = bias
        for gi in range(GMLP_GROUPS):
            s = s + jnp.where(lane_grp == gi, jnp.dot(wt[gi], vch, preferred_element_type=F32), 0.0)
        yb.append(u[ci * CHUNK:(ci + 1) * CHUNK] * s)
    y_b = jnp.concatenate(yb, axis=0)
    yab_ref[...] = jnp.concatenate([y_a, y_b], axis=1).astype(BF16)

    ones_bd = _head_group_ones(256)
    qn = qn_ref[...]
    scale = HEAD_DIM ** -0.5
    q = jnp.concatenate([_head_rms(z[:, 1280:1536], qn, ones_bd),
                         _head_rms(z[:, 1536:1792], qn, ones_bd)], axis=1)
    q_ref[...] = (q * scale).astype(BF16)
    vs = z[:, 2176:2304]
    vw = z[:, 2432:2560]
    kn = _head_rms(jnp.concatenate([z[:, 2048:2176], z[:, 2304:2432]], axis=1), kn_ref[...], ones_bd)
    ks, kw = kn[:, 0:128], kn[:, 128:256]
    ks_t, kw_t, vw_t = ks.T, kw.T, vw.T
    kc_ref[...] = z[:, 1792:1920].T
    vc_ref[...] = z[:, 1920:2048].T
    ks_ref[...] = ks_t
    vs_ref[...] = vs.T

    if tm >= WINDOW:
        kww_ref[...] = kw_t[:, tm - WINDOW:]
        vww_ref[...] = vw_t[:, tm - WINDOW:]
    else:
        first = tiles_per_seq - WINDOW // tm

        @pl.when(j >= first)
        def _():
            off = pl.multiple_of((j - first) * tm, tm)
            kww_ref[:, pl.ds(off, tm)] = kw_t
            vww_ref[:, pl.ds(off, tm)] = vw_t

    pos = j * tm + lax.broadcasted_iota(jnp.int32, (1, tm), 1)
    prow = _pos_rows(pos)
    kst = ks_t.astype(BF16)
    kwt = kw_t.astype(BF16)
    n_blk_pad = kst_ref.shape[1] - 2 * HEAD_DIM
    blk_row = lax.broadcasted_iota(jnp.int32, (n_blk_pad, tm), 0)
    erows = jnp.where(blk_row == (pos >> 6), 1.0, 0.0).astype(BF16)
    for grp in range(N_KV_HEADS):
        sl = slice(grp * HEAD_DIM, (grp + 1) * HEAD_DIM)
        kst_ref[grp] = jnp.concatenate([erows, kst[sl], prow], axis=0)
        kwt_ref[grp] = jnp.concatenate([kwt[sl], prow], axis=0)
        vse_ref[grp] = _value_ext(vs, grp)
        vwe_ref[grp] = _value_ext(vw, grp)


def _in_proj_prompt(x, nb, seq, g, wm, wcg, cw, gn, ws, bs_tile, qn, kn12, tm):
    m, d = x.shape
    tps = seq // tm
    row = lambda i: (i, 0)
    rows = lambda w, dt: (pl.BlockSpec((tm, w), row), jax.ShapeDtypeStruct((m, w), dt))
    win = (pl.BlockSpec((None, D_KV, WINDOW), lambda i: (i // tps, 0, 0)),
           jax.ShapeDtypeStruct((nb, D_KV, WINDOW), F32))
    feat = (pl.BlockSpec((None, D_KV, tm), lambda i: (i // tps, 0, i % tps)),
            jax.ShapeDtypeStruct((nb, D_KV, seq), F32))
    kt = (pl.BlockSpec((None, N_KV_HEADS, 2 * HEAD_DIM, tm), lambda i: (i // tps, 0, 0, i % tps)),
          jax.ShapeDtypeStruct((nb, N_KV_HEADS, 2 * HEAD_DIM, seq), BF16))
    ve = (pl.BlockSpec((None, N_KV_HEADS, tm, LANES), lambda i: (i // tps, 0, i % tps, 0)),
          jax.ShapeDtypeStruct((nb, N_KV_HEADS, seq, LANES), BF16))
    conv = (pl.BlockSpec((None, CONV_W - 1, D_CONV), lambda i: (i // tps, 0, 0)),
            jax.ShapeDtypeStruct((nb, CONV_W - 1, D_CONV), F32))
    n_krows = _n_blk_pad(seq) + 2 * HEAD_DIM
    kt_sel = (pl.BlockSpec((None, N_KV_HEADS, n_krows, tm), lambda i: (i // tps, 0, 0, i % tps)),
              jax.ShapeDtypeStruct((nb, N_KV_HEADS, n_krows, seq), BF16))
    outs = [rows(512, BF16), rows(512, BF16), rows(256, F32), feat, feat, feat, feat, win, win, conv,
            kt_sel, ve, kt, ve]
    return pl.pallas_call(
        functools.partial(_inproj_kernel, tm=tm, tiles_per_seq=tps),
        grid=(m // tm,),
        in_specs=[pl.BlockSpec((tm, d), row), _const_spec((1, d)), _const_spec(wm.shape),
                  _const_spec(wcg.shape), _const_spec(cw.shape), _const_spec(gn.shape),
                  _const_spec(ws.shape), _const_spec(bs_tile.shape), _const_spec(qn.shape),
                  _const_spec(kn12.shape)],
        out_specs=[o[0] for o in outs],
        out_shape=[o[1] for o in outs],
        scratch_shapes=[pltpu.VMEM((tm + SUBLANES, D_CONV), F32)],
        compiler_params=_cparams("arbitrary"),
        name="in_proj_prompt",
    )(x, g, wm, wcg, cw, gn, ws, bs_tile, qn, kn12)


def _half_sums(pages, taps, seg):
    lhs = jnp.concatenate([(p * taps).astype(BF16) for p in pages], axis=1)
    return jnp.dot(lhs, seg, preferred_element_type=F32)


def _combine_halves(p0, p1):
    n = p0.shape[1]
    col = lax.broadcasted_iota(jnp.int32, p0.shape, 1)
    return jnp.where(col < n - 1, p0 + pltpu.roll(p1, n - 1, axis=1), 0.0)


def _head_rms_rows(x, g_col):
    frow = lax.broadcasted_iota(jnp.int32, x.shape, 0)
    sq = x * x
    ss0 = jnp.sum(jnp.where(frow < HEAD_DIM, sq, 0.0), axis=0, keepdims=True)
    ss1 = jnp.sum(jnp.where(frow >= HEAD_DIM, sq, 0.0), axis=0, keepdims=True)
    inv = lax.rsqrt(jnp.where(frow < HEAD_DIM, ss0, ss1) * (1.0 / HEAD_DIM) + EPS)
    return x * inv * g_col


def _compress_kernel(kc_ref, vc_ref, wt_ref, seg_ref, kn_ref, kct_ref, vce_ref, *, n_half):
    seg = seg_ref[...]
    rows_per_dot = seg.shape[0]

    def halves(src_ref, taps):
        parts = []
        for c in range(src_ref.shape[1] // rows_per_dot):
            pages = [src_ref[:, c * rows_per_dot + k * PAGE_SIZE:c * rows_per_dot + (k + 1) * PAGE_SIZE]
                     for k in range(rows_per_dot // PAGE_SIZE)]
            parts.append(_half_sums(pages, taps, seg))
        return jnp.concatenate(parts, axis=1)

    kc = _combine_halves(halves(kc_ref, wt_ref[0]), halves(kc_ref, wt_ref[1]))
    vc = _combine_halves(halves(vc_ref, wt_ref[2]), halves(vc_ref, wt_ref[3]))
    kc = _head_rms_rows(kc, kn_ref[...]).astype(BF16)
    vc_rows = vc.T
    cmp_end = lax.broadcasted_iota(jnp.int32, (1, n_half), 1) * CMP_STRIDE + (CMP_LEN - 1)
    prow = _pos_rows(cmp_end)
    for grp in range(N_KV_HEADS):
        kct_ref[grp] = jnp.concatenate([kc[grp * HEAD_DIM:(grp + 1) * HEAD_DIM], prow], axis=0)
        vce_ref[grp] = _value_ext(vc_rows, grp)


def _compress_prompt(kc_t, vc_t, nb, seq, wt, seg, kn0_col):
    n_half = seq // CMP_STRIDE
    return pl.pallas_call(
        functools.partial(_compress_kernel, n_half=n_half),
        grid=(nb,),
        in_specs=[pl.BlockSpec((None, D_KV, seq), lambda b: (b, 0, 0)),
                  pl.BlockSpec((None, D_KV, seq), lambda b: (b, 0, 0)),
                  _const_spec(wt.shape), _const_spec(seg.shape), _const_spec(kn0_col.shape)],
        out_specs=[pl.BlockSpec((None, N_KV_HEADS, 2 * HEAD_DIM, n_half), lambda b: (b, 0, 0, 0)),
                   pl.BlockSpec((None, N_KV_HEADS, n_half, LANES), lambda b: (b, 0, 0, 0))],
        out_shape=[jax.ShapeDtypeStruct((nb, N_KV_HEADS, 2 * HEAD_DIM, n_half), BF16),
                   jax.ShapeDtypeStruct((nb, N_KV_HEADS, n_half, LANES), BF16)],
        compiler_params=_cparams("arbitrary"),
        name="compress_prompt",
    )(kc_t, vc_t, wt, seg, kn0_col)


def _attn_kernel(q_ref, gate_ref, slope_ref, wb_ref, kct_ref, vce_ref, kst_ref, vse_ref, kwt_ref, vwe_ref, o_ref,
                 qx_ref, m_ref, acc_ref, oc_ref, ow_ref, list_ref, *, n_cmp_pad, n_blk_pad):
    qb = Q_BLOCK
    rows = HEADS_PER_KV * qb
    i = pl.program_id(2)
    p0 = i * qb

    q = q_ref[...].astype(F32)
    lane = lax.broadcasted_iota(jnp.int32, (qb, LANES), 1)
    parts = []
    for hp in range(HEADS_PER_KV):
        col = q[:, (hp // 2) * LANES:(hp // 2 + 1) * LANES]
        if hp % 2 == 1:
            col = _swap_halves(col)
        parts.append(jnp.where(lane < HEAD_DIM, col, 0.0))
    qx = (jnp.concatenate(parts, axis=0) + slope_ref[...]).astype(BF16)

    t_q = p0 + lax.broadcasted_iota(jnp.int32, (qb, 1), 0)
    t_rows = jnp.concatenate([t_q] * HEADS_PER_KV, axis=0)

    s_c = jnp.dot(qx, kct_ref[...], preferred_element_type=F32)
    n_win = WINDOW + qb
    wstart = pl.multiple_of(jnp.maximum(p0 - WINDOW, 0), qb)
    s_w = (jnp.dot(qx, kwt_ref[:, pl.ds(wstart, n_win)], preferred_element_type=F32)
           + jnp.concatenate([wb_ref[...]] * HEADS_PER_KV, axis=0))

    cmp_end = lax.broadcasted_iota(jnp.int32, (1, n_cmp_pad), 1) * CMP_STRIDE + (CMP_LEN - 1)
    vis = cmp_end <= t_rows
    s_c = jnp.where(vis, s_c, NEG)
    e_c = jnp.where(vis, jnp.exp(s_c - _rep(_row_max(s_c), s_c)), 0.0)
    p_c = e_c * _rep(1.0 / jnp.maximum(_row_sum(e_c), 1e-30), e_c)
    oc_ref[...] = jnp.dot(p_c.astype(BF16), vce_ref[...], preferred_element_type=F32)

    psum = p_c[0:qb]
    for hp in range(1, HEADS_PER_KV):
        psum = psum + p_c[hp * qb:(hp + 1) * qb]
    bidx = lax.broadcasted_iota(jnp.int32, (n_blk_pad, n_cmp_pad), 0)
    cidx = lax.broadcasted_iota(jnp.int32, (n_blk_pad, n_cmp_pad), 1)
    ratio = SEL_LEN // CMP_STRIDE
    band_t = jnp.where((cidx >= ratio * bidx - 1) & (cidx <= ratio * bidx + ratio - 1)
                       & (cidx < n_cmp_pad - 1), 1.0, 0.0).astype(BF16)
    nt = (((1,), (1,)), ((), ()))
    imp = sum(lax.dot_general(band_t, part, nt, preferred_element_type=F32)
              for part in _split3(psum))
    e_w = jnp.exp(s_w - _rep(_row_max(s_w), s_w))
    acc_w = jnp.dot(e_w.astype(BF16), vwe_ref[pl.ds(wstart, n_win), :], preferred_element_type=F32)
    ow_ref[...] = acc_w * (1.0 / _swap_halves(acc_w))

    blk = lax.broadcasted_iota(jnp.int32, (n_blk_pad, qb), 0)
    t_lane = p0 + lax.broadcasted_iota(jnp.int32, (1, qb), 1)
    cur = t_lane >> 6
    forced = (blk == 0) | (blk == cur) | (blk == cur - 1)
    future = blk * SEL_LEN > t_lane
    imp = jnp.where(forced, -jnp.inf, imp)
    imp = jnp.where(future, NEG, imp)
    sel_t = (forced | _topk_select_cols(imp, blk.astype(F32), TOP_N - 3)) & jnp.logical_not(future)
    selneg = jnp.where(sel_t, 0.0, -MASK_BIG).T.astype(BF16)

    last = p0 // KEY_CHUNK
    blocks_per_chunk = KEY_CHUNK // SEL_LEN
    sel_f = jnp.where(sel_t, 1.0, 0.0)
    n_list = jnp.int32(0)
    for c in range(list_ref.shape[0] - 1):
        used = jnp.max(sel_f[c * blocks_per_chunk:(c + 1) * blocks_per_chunk])
        active = jnp.logical_and(used > 0.0, c < last)
        list_ref[n_list] = jnp.where(active, c, last)
        n_list = n_list + active.astype(jnp.int32)
    list_ref[n_list] = last
    qx_ref[:, 0:n_blk_pad] = jnp.concatenate([selneg] * HEADS_PER_KV, axis=0)
    qx_ref[:, n_blk_pad:] = qx

    m_ref[...] = jnp.full((rows, LANES), NEG, F32)
    acc_ref[...] = jnp.zeros((rows, LANES), F32)

    def scores(c):
        start = pl.multiple_of(c * KEY_CHUNK, KEY_CHUNK)
        return jnp.dot(qx_ref[...], kst_ref[:, pl.ds(start, KEY_CHUNK)], preferred_element_type=F32)

    def softmax_update(s, c):
        start = pl.multiple_of(c * KEY_CHUNK, KEY_CHUNK)
        m_old = m_ref[...]
        m_new = jnp.maximum(m_old, _row_max(s))
        p = jnp.exp(s - _rep(m_new, s)).astype(BF16)
        acc_ref[...] = (jnp.exp(m_old - m_new) * acc_ref[...]
                        + jnp.dot(p, vse_ref[pl.ds(start, KEY_CHUNK), :], preferred_element_type=F32))
        m_ref[...] = m_new

    kpos = last * KEY_CHUNK + lax.broadcasted_iota(jnp.int32, (1, KEY_CHUNK), 1)
    causal = lambda s: jnp.where(kpos <= t_rows, s, NEG)

    def pair(ca, cb, last_is_diagonal):
        s_a, s_b = scores(ca), scores(cb)
        softmax_update(s_a, ca)
        softmax_update(causal(s_b) if last_is_diagonal else s_b, cb)

    def body(j, carry):
        pair(list_ref[2 * j], list_ref[2 * j + 1], False)
        return carry

    lax.fori_loop(0, n_list // 2, body, 0)

    @pl.when(n_list % 2 == 1)
    def _():
        pair(list_ref[n_list - 1], last, True)

    @pl.when(n_list % 2 == 0)
    def _():
        softmax_update(causal(scores(last)), last)

    acc_s = acc_ref[...]

    o_s = acc_s * (1.0 / _swap_halves(acc_s))
    o_c = oc_ref[...]
    o_w = ow_ref[...]
    gate = gate_ref[...]
    res = []
    for hp in range(HEADS_PER_KV):
        sl = slice(hp * qb, (hp + 1) * qb)
        gc = gate[:, N_BRANCH * hp + 0:N_BRANCH * hp + 1]
        gs = gate[:, N_BRANCH * hp + 1:N_BRANCH * hp + 2]
        gw = gate[:, N_BRANCH * hp + 2:N_BRANCH * hp + 3]
        res.append(gc * o_c[sl] + gs * o_s[sl] + gw * o_w[sl])
    cols = [jnp.where(lane < HEAD_DIM, res[2 * k], _swap_halves(res[2 * k + 1])) for k in range(2)]
    o_ref[...] = jnp.concatenate(cols, axis=1).astype(o_ref.dtype)


def _window_bias():
    r = np.arange(Q_BLOCK)[:, None]
    j = np.arange(WINDOW + Q_BLOCK)[None, :]
    early = [j <= Q_BLOCK * v + r for v in range(WINDOW // Q_BLOCK)]
    steady = (j >= r) & (j <= r + WINDOW)
    return jnp.asarray(np.where(np.stack(early + [steady]), 0.0, NEG), F32)


def _attention_prompt(q, gates, slope_rows, wbias, kct, vce, kst, vse, kwt, vwe, nb, seq):
    nq = seq // Q_BLOCK
    n_cmp_pad = kct.shape[-1]
    n_blk_pad = _n_blk_pad(seq)
    rows = HEADS_PER_KV * Q_BLOCK
    n_var = wbias.shape[0]
    qspec = pl.BlockSpec((Q_BLOCK, HEADS_PER_KV * HEAD_DIM), lambda b, g, i: (b * nq + i, g))
    per_bg = lambda shape: pl.BlockSpec((None, None) + shape, lambda b, g, i: (b, g, 0, 0))
    return pl.pallas_call(
        functools.partial(_attn_kernel, n_cmp_pad=n_cmp_pad, n_blk_pad=n_blk_pad),
        grid=(nb, N_KV_HEADS, nq),
        in_specs=[qspec,
                  pl.BlockSpec((Q_BLOCK, LANES), lambda b, g, i: (b * nq + i, g)),
                  pl.BlockSpec((None, rows, LANES), lambda b, g, i: (g, 0, 0)),
                  pl.BlockSpec((None,) + wbias.shape[1:], lambda b, g, i: (jnp.minimum(i, n_var - 1), 0, 0)),
                  per_bg((2 * HEAD_DIM, n_cmp_pad)), per_bg((n_cmp_pad, LANES)),
                  per_bg((n_blk_pad + 2 * HEAD_DIM, seq)), per_bg((seq, LANES)),
                  per_bg((2 * HEAD_DIM, seq)), per_bg((seq, LANES))],
        out_specs=qspec,
        out_shape=jax.ShapeDtypeStruct(q.shape, BF16),
        scratch_shapes=[pltpu.VMEM((rows, n_blk_pad + LANES), BF16), pltpu.VMEM((rows, LANES), F32),
                        pltpu.VMEM((rows, LANES), F32), pltpu.VMEM((rows, LANES), F32),
                        pltpu.VMEM((rows, LANES), F32), pltpu.SMEM((seq // KEY_CHUNK + 1,), jnp.int32)],
        compiler_params=_cparams("arbitrary", "arbitrary", "arbitrary"),
        name="attention_prompt",
    )(q, gates, slope_rows, wbias, kct, vce, kst, vse, kwt, vwe)


def _merge_kernel(x_ref, yab_ref, yc_ref, g_ref, wmg_ref, wb_ref, wo_ref, o_ref):
    x = x_ref[...]
    d = x.shape[1]
    h = _rms(x, g_ref[...]).astype(BF16)
    yab = yab_ref[...]
    branches = (jnp.dot(yab[:, 0:D_CONV], wb_ref[0:D_CONV, :], preferred_element_type=F32),
                jnp.dot(yab[:, D_CONV:], wb_ref[D_CONV:D_CONV + D_GMLP, :], preferred_element_type=F32),
                jnp.dot(yc_ref[...], wb_ref[D_CONV + D_GMLP:, :], preferred_element_type=F32))
    merged = None
    for k, y in enumerate(branches):
        gk = _sigmoid(jnp.dot(h, wmg_ref[:, k * d:(k + 1) * d], preferred_element_type=F32))
        merged = gk * y if merged is None else merged + gk * y
    o_ref[...] = x + jnp.dot(merged.astype(BF16), wo_ref[...], preferred_element_type=F32)


def _merge(x, yab, yc, g, wmg, wb, wo, tm):
    m, d = x.shape
    row = lambda i: (i, 0)
    return pl.pallas_call(
        _merge_kernel,
        grid=(m // tm,),
        in_specs=[pl.BlockSpec((tm, d), row), pl.BlockSpec((tm, yab.shape[1]), row),
                  pl.BlockSpec((tm, yc.shape[1]), row), _const_spec((1, d)), _const_spec(wmg.shape),
                  _const_spec(wb.shape), _const_spec(wo.shape)],
        out_specs=pl.BlockSpec((tm, d), row),
        out_shape=jax.ShapeDtypeStruct((m, d), F32),
        compiler_params=_cparams("arbitrary"),
        name="merge_out",
    )(x, yab, yc, g, wmg, wb, wo)


def _inproj_sample_kernel(x_ref, g_ref, wm_ref, wcg_ref, cw_ref, st0_ref, st1_ref, gn_ref, ws0_ref, bs0_ref,
                          qn_ref, kn_ref, yab_ref, q_ref, gate_ref, kc_ref, vc_ref, ks_ref, vs_ref, kw_ref,
                          vw_ref, zc_ref, vrow_ref):
    h = _rms(x_ref[...], g_ref[...]).astype(BF16)
    z = jnp.dot(h, wm_ref[...], preferred_element_type=F32)
    gate_ref[...] = _sigmoid(jnp.dot(h, wcg_ref[...], preferred_element_type=F32))
    a_b, a_c, a_x = z[:, 0:256], z[:, 256:512], z[:, 512:768]
    zc = a_c * a_x
    cw = cw_ref[...]
    y_a = a_b * (cw[0:1] * st0_ref[...] + cw[1:2] * st1_ref[...] + cw[2:3] * zc)
    zc_ref[...] = zc
    u = _gelu_tanh(z[:, 768:1024])
    v = _rms(_gelu_tanh(z[:, 1024:1280]), gn_ref[...])
    vrow_ref[...] = v
    y_b = u * (ws0_ref[...] * v + bs0_ref[...])
    yab_ref[...] = jnp.concatenate([y_a, y_b], axis=1).astype(BF16)
    ones_bd = _head_group_ones(256)
    qn = qn_ref[...]
    q = jnp.concatenate([_head_rms(z[:, 1280:1536], qn, ones_bd),
                         _head_rms(z[:, 1536:1792], qn, ones_bd)], axis=1)
    q_ref[...] = q * (HEAD_DIM ** -0.5)
    kc_ref[...] = z[:, 1792:1920]
    vc_ref[...] = z[:, 1920:2048]
    vs_ref[...] = z[:, 2176:2304]
    vw_ref[...] = z[:, 2432:2560]
    kn = _head_rms(jnp.concatenate([z[:, 2048:2176], z[:, 2304:2432]], axis=1), kn_ref[...], ones_bd)
    ks_ref[...] = kn[:, 0:128]
    kw_ref[...] = kn[:, 128:256]


def _in_proj_sample(x, g, wm, wcg, cw, st0, st1, gn, ws0, bs0, qn, kn12):
    m = x.shape[0]
    ins = (x, g, wm, wcg, cw, st0, st1, gn, ws0, bs0, qn, kn12)
    sd = lambda w, dt=F32: jax.ShapeDtypeStruct((m, w), dt)
    out_shape = [sd(512, BF16), sd(512), sd(256), sd(128), sd(128), sd(128), sd(128), sd(128), sd(128),
                 sd(256), sd(256)]
    return pl.pallas_call(
        _inproj_sample_kernel,
        grid=(1,),
        in_specs=[_const_spec(a.shape) for a in ins],
        out_specs=[_const_spec(s.shape) for s in out_shape],
        out_shape=out_shape,
        compiler_params=_cparams("arbitrary"),
        name="in_proj_sample",
    )(*ins)


def _head_slopes():
    hrow = lax.broadcasted_iota(jnp.int32, (N_HEADS, 1), 0)
    return lax.bitcast_convert_type((126 - hrow) << 23, F32)


def _sample_cmp_kernel(pt_ref, *refs, n_half, t_pos):
    del pt_ref
    pp = CMP_PAGES
    kpages, vpages = refs[0:pp], refs[pp:2 * pp]
    qz_ref, wt_ref, seg_ref, kn_ref = refs[2 * pp:2 * pp + 4]
    oc_ref, sel_ref = refs[2 * pp + 4:2 * pp + 6]
    p0k_ref, p1k_ref, p0v_ref, p1v_ref = refs[2 * pp + 6:]
    s = pl.program_id(1)
    halves = pp * PAGE_SIZE // CMP_STRIDE
    off = pl.multiple_of(s * halves, halves)
    seg = seg_ref[...]

    kp = [r[...] for r in kpages]
    vp = [r[...] for r in vpages]
    p0k_ref[:, pl.ds(off, halves)] = _half_sums(kp, wt_ref[0], seg)
    p1k_ref[:, pl.ds(off, halves)] = _half_sums(kp, wt_ref[1], seg)
    p0v_ref[:, pl.ds(off, halves)] = _half_sums(vp, wt_ref[2], seg)
    p1v_ref[:, pl.ds(off, halves)] = _half_sums(vp, wt_ref[3], seg)

    @pl.when(s == pl.num_programs(1) - 1)
    def _():
        kc = _head_rms_rows(_combine_halves(p0k_ref[...], p1k_ref[...]), kn_ref[...])
        vc = _combine_halves(p0v_ref[...], p1v_ref[...])
        qz = qz_ref[...].astype(BF16)
        s_c = jnp.dot(qz, kc.astype(BF16), preferred_element_type=F32)
        cmp_end = lax.broadcasted_iota(jnp.int32, (1, n_half), 1) * CMP_STRIDE + (CMP_LEN - 1)
        d_c = t_pos - cmp_end
        vis = d_c >= 0
        s_c = jnp.where(vis, s_c - _head_slopes() * d_c.astype(F32), NEG)
        e_c = jnp.where(vis, jnp.exp(s_c - jnp.max(s_c, axis=1, keepdims=True)), 0.0)
        p_c = e_c * (1.0 / jnp.maximum(jnp.sum(e_c, axis=1, keepdims=True), 1e-30))
        nt = (((1,), (1,)), ((), ()))
        oc_ref[...] = lax.dot_general(p_c.astype(BF16), vc.astype(BF16), nt, preferred_element_type=F32)

        hrow = lax.broadcasted_iota(jnp.int32, p_c.shape, 0)
        ps0 = jnp.sum(jnp.where(hrow < HEADS_PER_KV, p_c, 0.0), axis=0, keepdims=True)
        ps1 = jnp.sum(jnp.where(hrow >= HEADS_PER_KV, p_c, 0.0), axis=0, keepdims=True)
        prow = lax.broadcasted_iota(jnp.int32, (LANES, n_half), 0)
        psum = jnp.where(prow == 0, ps0, jnp.where(prow == 1, ps1, 0.0))
        n_blk_pad = sel_ref.shape[1]
        n_sel = t_pos // SEL_LEN + 1
        bidx = lax.broadcasted_iota(jnp.int32, (n_blk_pad, n_half), 0)
        cidx = lax.broadcasted_iota(jnp.int32, (n_blk_pad, n_half), 1)
        ratio = SEL_LEN // CMP_STRIDE
        band_t = jnp.where((cidx >= ratio * bidx - 1) & (cidx <= ratio * bidx + ratio - 1)
                           & (cidx < n_half - 1), 1.0, 0.0).astype(BF16)
        imp = sum(lax.dot_general(band_t, part, nt, preferred_element_type=F32) for part in _split3(psum))
        blk = lax.broadcasted_iota(jnp.int32, imp.shape, 0)
        cur = t_pos // SEL_LEN
        forced = (blk == 0) | (blk == cur) | (blk == cur - 1)
        future = blk * SEL_LEN > t_pos
        imp = jnp.where(forced, -NEG, imp)
        imp = jnp.where(future, NEG, imp)
        imp = jnp.where(blk < n_sel, imp, -jnp.inf)
        sel_t = (_topk_select_cols(imp, blk.astype(F32), min(TOP_N, n_sel))
                 & jnp.logical_not(future) & (blk < n_sel))
        sel_ref[...] = jnp.where(sel_t, 1.0, 0.0).T[0:SUBLANES]


def _page_specs(layer, n, table_col):
    def spec(k):
        return pl.BlockSpec((None, None, D_KV, PAGE_SIZE),
                            lambda b, s, *tabs: (layer, table_col(tabs, b, s * n + k), 0, 0))
    return [spec(k) for k in range(n)]


def _sample_cmp(page_table, cache_k, cache_v, layer, qz, wt, seg, kn0_col, t_pos):
    nb, n_pages = page_table.shape
    n_half = n_pages * PAGE_SIZE // CMP_STRIDE
    n_blk_pad = -(-(t_pos // SEL_LEN + 1) // LANES) * LANES
    pp = CMP_PAGES
    per_b = lambda shape: pl.BlockSpec((None,) + shape, lambda b, s, pt: (b, 0, 0))
    const = lambda shape: pl.BlockSpec(shape, lambda b, s, pt: (0,) * len(shape))
    pages = lambda: _page_specs(layer, pp, lambda tabs, b, j: tabs[0][b, j])
    grid_spec = pltpu.PrefetchScalarGridSpec(
        num_scalar_prefetch=1,
        grid=(nb, n_pages // pp),
        in_specs=pages() + pages()
        + [per_b((N_HEADS, LANES)), const(wt.shape), const(seg.shape), const(kn0_col.shape)],
        out_specs=[per_b((N_HEADS, LANES)), per_b((SUBLANES, n_blk_pad))],
        scratch_shapes=[pltpu.VMEM((D_KV, n_half), F32)] * 4,
    )
    return pl.pallas_call(
        functools.partial(_sample_cmp_kernel, n_half=n_half, t_pos=t_pos),
        grid_spec=grid_spec,
        out_shape=[jax.ShapeDtypeStruct((nb, N_HEADS, LANES), F32),
                   jax.ShapeDtypeStruct((nb, SUBLANES, n_blk_pad), F32)],
        compiler_params=_cparams("arbitrary", "arbitrary"),
        name="sample_cmp",
    )(page_table, *([cache_k] * pp), *([cache_v] * pp), qz, wt, seg, kn0_col)


def _sample_sel_kernel(phys_ref, lp_ref, *refs, t_pos):
    del phys_ref
    pp = SEL_PAGES
    kpages, vpages = refs[0:pp], refs[pp:2 * pp]
    qz_ref, sel_ref, kn_ref, vn_ref, o_ref, m_ref, l_ref, acc_ref = refs[2 * pp:]
    b, s = pl.program_id(0), pl.program_id(1)
    n_keys = pp * PAGE_SIZE

    @pl.when(s == 0)
    def _():
        m_ref[...] = jnp.full(m_ref.shape, NEG, F32)
        l_ref[...] = jnp.zeros(l_ref.shape, F32)
        acc_ref[...] = jnp.zeros(acc_ref.shape, F32)

    lane = lax.broadcasted_iota(jnp.int32, (1, PAGE_SIZE), 1)
    lps = [lp_ref[b, s * pp + k] for k in range(pp)]
    kpos = jnp.concatenate([lp * PAGE_SIZE + lane for lp in lps], axis=1)
    kblk = jnp.concatenate([lp * (PAGE_SIZE // SEL_LEN) + lane // SEL_LEN for lp in lps], axis=1)
    qz = qz_ref[...].astype(BF16)
    kk = jnp.concatenate([r[...] for r in kpages], axis=1).astype(BF16)
    vv = jnp.concatenate([r[...] for r in vpages], axis=1).astype(BF16)
    sc = jnp.dot(qz, kk, preferred_element_type=F32)
    sc = sc - _head_slopes() * (t_pos - kpos).astype(F32)
    n_blk_pad = sel_ref.shape[1]
    hrow = lax.broadcasted_iota(jnp.int32, (N_HEADS, n_blk_pad), 0)
    selv = sel_ref[...]
    sel_h = jnp.where(hrow < HEADS_PER_KV, selv[0:1], selv[1:2]).astype(BF16)
    erow = lax.broadcasted_iota(jnp.int32, (n_blk_pad, n_keys), 0)
    ok = jnp.dot(sel_h, jnp.where(erow == kblk, 1.0, 0.0).astype(BF16), preferred_element_type=F32) > 0.5
    sc = jnp.where(ok, sc, NEG)
    m_old = m_ref[...]
    m_new = jnp.maximum(m_old, jnp.max(sc, axis=1, keepdims=True))
    p = jnp.where(ok, jnp.exp(sc - m_new), 0.0)
    alpha = jnp.exp(m_old - m_new)
    l_ref[...] = alpha * l_ref[...] + jnp.sum(p, axis=1, keepdims=True)
    acc_ref[...] = alpha * acc_ref[...] + lax.dot_general(p.astype(BF16), vv, (((1,), (1,)), ((), ())),
                                                           preferred_element_type=F32)
    m_ref[...] = m_new

    @pl.when(s == pl.num_programs(1) - 1)
    def _():
        k_new = kn_ref[...].astype(BF16).astype(F32)
        s_new = jnp.sum(qz.astype(F32) * k_new, axis=1, keepdims=True)
        m_o = m_ref[...]
        m_n = jnp.maximum(m_o, s_new)
        a = jnp.exp(m_o - m_n)
        p_new = jnp.exp(s_new - m_n)
        l = a * l_ref[...] + p_new
        acc = a * acc_ref[...] + p_new.astype(BF16).astype(F32) * vn_ref[...].astype(BF16).astype(F32)
        o_ref[...] = acc * (1.0 / l)


def _needed_pages(sel, page_table):
    nb, n_pages = page_table.shape
    per_page = PAGE_SIZE // SEL_LEN
    flags = sel[:, :N_KV_HEADS, :n_pages * per_page] > 0.5
    need = flags.reshape(nb, N_KV_HEADS, n_pages, per_page).any(axis=(1, 3))
    n_slots = min(SEL_SLOTS, n_pages)
    order = jnp.argsort(jnp.logical_not(need), axis=1, stable=True)[:, :n_slots]
    count = need.sum(axis=1, keepdims=True)
    valid = jnp.arange(n_slots, dtype=jnp.int32)[None, :] < count
    logical = jnp.where(valid, order, -1).astype(jnp.int32)
    phys = jnp.take_along_axis(page_table, jnp.maximum(logical, 0), axis=1).astype(jnp.int32)
    return phys, logical


def _sample_sel(phys, logical, cache_k, cache_v, layer, qz, sel, ks_new, vs_new, t_pos):
    nb = phys.shape[0]
    pp = SEL_PAGES
    per_b = lambda shape: pl.BlockSpec((None,) + shape, lambda b, s, ph, lp: (b, 0, 0))
    pages = lambda: _page_specs(layer, pp, lambda tabs, b, j: tabs[0][b, j])
    grid_spec = pltpu.PrefetchScalarGridSpec(
        num_scalar_prefetch=2,
        grid=(nb, phys.shape[1] // pp),
        in_specs=pages() + pages()
        + [per_b((N_HEADS, LANES)), per_b(sel.shape[1:]), per_b((1, D_KV)), per_b((1, D_KV))],
        out_specs=per_b((N_HEADS, LANES)),
        scratch_shapes=[pltpu.VMEM((N_HEADS, 1), F32), pltpu.VMEM((N_HEADS, 1), F32),
                        pltpu.VMEM((N_HEADS, LANES), F32)],
    )
    return pl.pallas_call(
        functools.partial(_sample_sel_kernel, t_pos=t_pos),
        grid_spec=grid_spec,
        out_shape=jax.ShapeDtypeStruct((nb, N_HEADS, LANES), F32),
        compiler_params=_cparams("arbitrary", "arbitrary"),
        name="sample_sel",
    )(phys, logical, *([cache_k] * pp), *([cache_v] * pp), qz, sel, ks_new, vs_new)


def _sample_win_kernel(qz_ref, wk_ref, wv_ref, kn_ref, vn_ref, gate_ref, oc_ref, os_ref,
                       y_ref, wko_ref, wvo_ref):
    qz = qz_ref[...].astype(BF16)
    kwin, vwin = wk_ref[...], wv_ref[...]
    n_win = kwin.shape[1]
    nt = (((1,), (1,)), ((), ()))
    sc = jnp.dot(qz, kwin.astype(BF16), preferred_element_type=F32)
    dist = n_win - lax.broadcasted_iota(jnp.int32, (1, n_win), 1)
    sc = sc - _head_slopes() * dist.astype(F32)
    k_new, v_new = kn_ref[...], vn_ref[...]
    s_new = jnp.sum(qz.astype(F32) * k_new.astype(BF16).astype(F32), axis=1, keepdims=True)
    m = jnp.maximum(jnp.max(sc, axis=1, keepdims=True), s_new)
    p = jnp.exp(sc - m)
    p_new = jnp.exp(s_new - m)
    l = jnp.sum(p, axis=1, keepdims=True) + p_new
    acc = (lax.dot_general(p.astype(BF16), vwin.astype(BF16), nt, preferred_element_type=F32)
           + p_new.astype(BF16).astype(F32) * v_new.astype(BF16).astype(F32))
    o_w = acc * (1.0 / l)
    gate = gate_ref[...]
    y = gate[:, 0:1] * oc_ref[...] + gate[:, 1:2] * os_ref[...] + gate[:, 2:3] * o_w
    lane = lax.broadcasted_iota(jnp.int32, (1, LANES), 1)
    cols = []
    for k in range(N_HEADS // 2):
        grp = (2 * k) // HEADS_PER_KV
        even, odd = y[2 * k:2 * k + 1], y[2 * k + 1:2 * k + 2]
        low = even if grp == 0 else _swap_halves(even)
        high = odd if grp == 1 else _swap_halves(odd)
        cols.append(jnp.where(lane < HEAD_DIM, low, high))
    y_ref[...] = jnp.concatenate(cols, axis=1).astype(y_ref.dtype)
    eye = (lax.broadcasted_iota(jnp.int32, (D_KV, D_KV), 0) == lax.broadcasted_iota(jnp.int32, (D_KV, D_KV), 1))
    as_col = lambda r: jnp.sum(jnp.where(eye, r, 0.0), axis=1, keepdims=True)
    pos = lax.broadcasted_iota(jnp.int32, kwin.shape, 1)
    wko_ref[...] = jnp.where(pos == n_win - 1, as_col(k_new), pltpu.roll(kwin, n_win - 1, axis=1))
    wvo_ref[...] = jnp.where(pos == n_win - 1, as_col(v_new), pltpu.roll(vwin, n_win - 1, axis=1))


def _sample_win(qz, win_k, win_v, layer, kw_new, vw_new, gates_h, o_c, o_s):
    nb = qz.shape[0]
    n_win = win_k.shape[3]
    per_b = lambda shape: pl.BlockSpec((None,) + shape, lambda b: (b, 0, 0))
    cache = pl.BlockSpec((None, None, D_KV, n_win), lambda b: (layer, b, 0, 0))
    hl = (N_HEADS, LANES)
    return pl.pallas_call(
        _sample_win_kernel,
        grid=(nb,),
        in_specs=[per_b(hl), cache, cache, per_b((1, D_KV)), per_b((1, D_KV)), per_b(hl), per_b(hl), per_b(hl)],
        out_specs=[per_b((1, D_ATTN)), per_b((D_KV, n_win)), per_b((D_KV, n_win))],
        out_shape=[jax.ShapeDtypeStruct((nb, 1, D_ATTN), F32), jax.ShapeDtypeStruct((nb, D_KV, n_win), F32),
                   jax.ShapeDtypeStruct((nb, D_KV, n_win), F32)],
        compiler_params=_cparams("arbitrary"),
        name="sample_win",
    )(qz, win_k, win_v, kw_new, vw_new, gates_h, o_c, o_s)


def _slope_rows():
    out = np.zeros((N_KV_HEADS, HEADS_PER_KV * Q_BLOCK, LANES), np.float32)
    for g in range(N_KV_HEADS):
        for hp in range(HEADS_PER_KV):
            slope = 2.0 ** -(g * HEADS_PER_KV + hp + 1)
            out[g, hp * Q_BLOCK:(hp + 1) * Q_BLOCK, HEAD_DIM] = slope * 128.0
            out[g, hp * Q_BLOCK:(hp + 1) * Q_BLOCK, HEAD_DIM + 1] = slope
    return jnp.asarray(out)


def _tile_lanes(v, reps):
    return jnp.tile(v.reshape(1, -1), (1, reps))


def _heads_to_rows(q):
    n = q.shape[0]
    qh = q.reshape(n, N_HEADS, HEAD_DIM)
    z = jnp.zeros_like(qh[:, :HEADS_PER_KV])
    return jnp.concatenate([jnp.concatenate([qh[:, :HEADS_PER_KV], z], axis=-1),
                            jnp.concatenate([z, qh[:, HEADS_PER_KV:]], axis=-1)], axis=1)


def kernel(x_prompt, x_sample, cache_cmp_k, cache_cmp_v, cache_sel_k, cache_sel_v, cache_win_k, cache_win_v, state_conv, page_table, ffn1_norm, ffn1_w_gate, ffn1_w_up, ffn1_w_down, mix_norm, w_in, conv_w, gmlp_norm, gmlp_ws, gmlp_bs, q_norm, k_norm, cmp_wk, cmp_wv, w_branch, w_out, ffn2_norm, ffn2_w_gate, ffn2_w_up, ffn2_w_down):
    nb, seq, d = x_prompt.shape
    ns = x_sample.shape[0]
    depth = w_in.shape[0]
    t_pos = page_table.shape[1] * PAGE_SIZE
    assert x_sample.shape[1] == 1 and cache_win_k.shape[2] == WINDOW
    assert seq % (CMP_STRIDE * LANES) == 0 and t_pos % (CMP_STRIDE * LANES) == 0
    assert page_table.shape[1] % CMP_PAGES == 0
    tm = min(ROW_TILE, seq)

    xp = x_prompt.reshape(nb * seq, d)
    xs = x_sample.reshape(ns, d)
    feat_major = lambda c: jnp.transpose(c, (0, 1, 3, 4, 2)).reshape(depth, c.shape[1], D_KV, c.shape[2])
    ck, cv, sk, sv = (feat_major(c) for c in (cache_cmp_k, cache_cmp_v, cache_sel_k, cache_sel_v))
    wink, winv = feat_major(cache_win_k), feat_major(cache_win_v)
    slope_rows = _slope_rows()
    wbias = _window_bias()
    seg_rows = CMP_PAGES * PAGE_SIZE
    seg = jnp.asarray(np.arange(seg_rows)[:, None] // CMP_STRIDE == np.arange(seg_rows // CMP_STRIDE)[None, :], BF16)

    prompt_new = [[] for _ in range(7)]
    sample_new = [[] for _ in range(8)]
    for l in range(depth):
        bf = lambda w: w.astype(BF16)
        w_main = bf(w_in[l, :, :MAIN_COLS])
        cg = w_in[l, :, MAIN_COLS:MAIN_COLS + N_BRANCH * N_HEADS].reshape(d, N_KV_HEADS, HEADS_PER_KV * N_BRANCH)
        w_cg = bf(jnp.pad(cg, ((0, 0), (0, 0), (0, LANES - HEADS_PER_KV * N_BRANCH))).reshape(d, N_KV_HEADS * LANES))
        w_mg = bf(w_in[l, :, MAIN_COLS + N_BRANCH * N_HEADS:])
        wb, wo = bf(w_branch[l]), bf(w_out[l])
        f1 = (ffn1_norm[l].reshape(1, d), bf(ffn1_w_gate[l]), bf(ffn1_w_up[l]), bf(ffn1_w_down[l]))
        f2 = (ffn2_norm[l].reshape(1, d), bf(ffn2_w_gate[l]), bf(ffn2_w_up[l]), bf(ffn2_w_down[l]))
        mn = mix_norm[l].reshape(1, d)
        cw = conv_w[l]
        gn = gmlp_norm[l].reshape(1, D_GMLP)
        gdim = D_GMLP // GMLP_GROUPS
        bs_tile = jnp.repeat(gmlp_bs[l].T, gdim, axis=1)
        ws0 = jnp.repeat(gmlp_ws[l, :, 0, 0], gdim).reshape(1, D_GMLP)
        bs0 = jnp.repeat(gmlp_bs[l, :, 0], gdim).reshape(1, D_GMLP)
        qn = _tile_lanes(q_norm[l], 4)
        kn0 = _tile_lanes(k_norm[l, 0], 2)
        kn12 = jnp.concatenate([_tile_lanes(k_norm[l, 1], 2), _tile_lanes(k_norm[l, 2], 2)], axis=1)
        wk = cmp_wk[l].reshape(CMP_LEN, D_KV)
        wv = cmp_wv[l].reshape(CMP_LEN, D_KV)
        taps = lambda w: jnp.tile(w.T, (1, PAGE_SIZE // CMP_STRIDE))
        wt = jnp.stack([taps(wk[:CMP_STRIDE]), taps(wk[CMP_STRIDE:]), taps(wv[:CMP_STRIDE]), taps(wv[CMP_STRIDE:])])

        xp = _half_ffn(xp, *f1, tm)
        (yab, q, gates, kc, vc, ks, vs, kww, vww, conv_new, kst, vse, kwt, vwe) = _in_proj_prompt(
            xp, nb, seq, mn, w_main, w_cg, cw, gn, gmlp_ws[l], bs_tile, qn, kn12, tm)
        kct, vce = _compress_prompt(kc, vc, nb, seq, wt, seg, kn0.reshape(D_KV, 1))
        yc = _attention_prompt(q, gates, slope_rows, wbias, kct, vce, kst, vse, kwt, vwe, nb, seq)
        xp = _merge(xp, yab, yc, mn, w_mg, wb, wo, tm)
        xp = _half_ffn(xp, *f2, tm)
        for lst, a in zip(prompt_new, (kc, vc, ks, vs, kww, vww, conv_new)):
            lst.append(a)

        xs = _half_ffn(xs, *f1, ns)
        (yab_s, q_s, gates_s, kc_s, vc_s, ks_s, vs_s, kw_s, vw_s, zc_s, vrow_s) = _in_proj_sample(
            xs, mn, w_main, w_cg, cw, state_conv[l, :, 0], state_conv[l, :, 1], gn, ws0, bs0, qn, kn12)
        qz = _heads_to_rows(q_s)
        o_c, sel = _sample_cmp(page_table, ck, cv, l, qz, wt, seg, kn0.reshape(D_KV, 1), t_pos)
        phys, logical = _needed_pages(sel, page_table)
        o_s = _sample_sel(phys, logical, sk, sv, l, qz, sel, ks_s.reshape(ns, 1, D_KV),
                          vs_s.reshape(ns, 1, D_KV), t_pos)
        gh = gates_s.reshape(ns, N_KV_HEADS, LANES)[:, :, :HEADS_PER_KV * N_BRANCH].reshape(ns, N_HEADS, N_BRANCH)
        gh = jnp.pad(gh, ((0, 0), (0, 0), (0, LANES - N_BRANCH)))
        yc_s, wk_new, wv_new = _sample_win(qz, wink, winv, l, kw_s.reshape(ns, 1, D_KV),
                                           vw_s.reshape(ns, 1, D_KV), gh, o_c, o_s)
        yc_s = yc_s.reshape(ns, D_ATTN).astype(BF16)
        xs = _merge(xs, yab_s, yc_s, mn, w_mg, wb, wo, ns)
        xs = _half_ffn(xs, *f2, ns)
        kv5s = lambda a: a.reshape(ns, -1, N_KV_HEADS, HEAD_DIM)
        conv_s = jnp.stack([state_conv[l, :, 1], zc_s], axis=1)
        rows_major = lambda a: jnp.transpose(a.reshape(ns, N_KV_HEADS, HEAD_DIM, -1), (0, 3, 1, 2))
        for lst, a in zip(sample_new, (kv5s(kc_s), kv5s(vc_s), kv5s(ks_s), kv5s(vs_s), rows_major(wk_new),
                                       rows_major(wv_new), conv_s, vrow_s.reshape(ns, 1, D_GMLP))):
            lst.append(a)

    tokens_major = lambda a: jnp.transpose(a.reshape(depth, nb, N_KV_HEADS, HEAD_DIM, -1), (0, 1, 4, 2, 3))
    outs_p = [jnp.stack(a) for a in prompt_new]
    outs_p = [tokens_major(a) for a in outs_p[:6]] + outs_p[6:]
    outs_s = [jnp.stack(a) for a in sample_new]
    return (xp.reshape(nb, seq, d), xs.reshape(ns, 1, d), *outs_p, *outs_s)
```

```python
import functools

import numpy as np
import jax
import jax.numpy as jnp
from jax import lax
from jax.experimental import pallas as pl
from jax.experimental.pallas import tpu as pltpu

F32 = jnp.float32
BF16 = jnp.bfloat16

HEAD_DIM = 64
N_HEADS = 8
N_KV_HEADS = 2
HEADS_PER_KV = N_HEADS // N_KV_HEADS
D_CONV = 256
CONV_W = 3
D_GMLP = 256
GMLP_GROUPS = 4
CHUNK = 128
D_ATTN = N_HEADS * HEAD_DIM
D_KV = N_KV_HEADS * HEAD_DIM
CMP_LEN = 32
CMP_STRIDE = 16
SEL_LEN = 64
TOP_N = 16
WINDOW = 512
Q_BLOCK = 128
N_BRANCH = 3
PAGE_SIZE = 128
EPS = 1e-6
NEG = -1e30
MASK_BIG = 2.0 ** 100
MAIN_COLS = 5 * 256 + D_ATTN + 6 * D_KV

LANES = 128
SUBLANES = 8
ROW_TILE = 512
FF_CHUNK = 256
KEY_CHUNK = 512
CMP_PAGES = 16
SEL_PAGES = 8
SEL_SLOTS = 32
VMEM_LIMIT = 56 * 1024 * 1024


def _cparams(*sem):
    return pltpu.CompilerParams(dimension_semantics=sem, vmem_limit_bytes=VMEM_LIMIT)


def _const_spec(shape):
    nd = len(shape)
    return pl.BlockSpec(shape, lambda *_: (0,) * nd, pipeline_mode=pl.Buffered(1))


def _rms(x, g):
    return x * lax.rsqrt(jnp.mean(x * x, axis=-1, keepdims=True) + EPS) * g


def _sigmoid(x):
    return 1.0 / (1.0 + jnp.exp(-x))


def _gelu_tanh(x):
    return 0.5 * x * (1.0 + jnp.tanh(0.7978845608028654 * (x + 0.044715 * (x * x * x))))


def _split3(x):
    hi = x.astype(BF16)
    r = x - hi.astype(F32)
    mid = r.astype(BF16)
    lo = (r - mid.astype(F32)).astype(BF16)
    return hi, mid, lo


def _exact_dot01(x, m01):
    hi, mid, lo = _split3(x)
    return (jnp.dot(hi, m01, preferred_element_type=F32) + jnp.dot(mid, m01, preferred_element_type=F32)
            + jnp.dot(lo, m01, preferred_element_type=F32))


def _head_group_ones(n):
    r = lax.broadcasted_iota(jnp.int32, (n, n), 0) // HEAD_DIM
    c = lax.broadcasted_iota(jnp.int32, (n, n), 1) // HEAD_DIM
    return jnp.where(r == c, 1.0, 0.0).astype(BF16)


def _head_rms(x, g, ones_bd):
    ssq = _exact_dot01(x * x, ones_bd)
    return x * lax.rsqrt(ssq * (1.0 / HEAD_DIM) + EPS) * g


def _swap_halves(x):
    return pltpu.roll(x, HEAD_DIM, axis=1)


def _value_ext(v, grp):
    lane = lax.broadcasted_iota(jnp.int32, v.shape, 1)
    src = v if grp == 0 else _swap_halves(v)
    return jnp.where(lane < HEAD_DIM, src, 1.0).astype(BF16)


def _pos_rows(pos):
    n = pos.shape[1]
    row = lax.broadcasted_iota(jnp.int32, (HEAD_DIM, n), 0)
    hi = (pos >> 7).astype(F32)
    lo = (pos & 127).astype(F32)
    return jnp.where(row == 0, hi, jnp.where(row == 1, lo, 0.0)).astype(BF16)


def _n_blk_pad(seq):
    return -(-(seq // SEL_LEN) // LANES) * LANES


def _fold_lane_tiles(x, op):
    t = x[:, 0:LANES]
    for k in range(1, x.shape[1] // LANES):
        t = op(t, x[:, k * LANES:(k + 1) * LANES])
    return t


def _row_max(x):
    t = _fold_lane_tiles(x, jnp.maximum)
    return jnp.broadcast_to(jnp.max(t, axis=1, keepdims=True), t.shape)


def _row_sum(x):
    t = _fold_lane_tiles(x, jnp.add)
    return jnp.broadcast_to(jnp.sum(t, axis=1, keepdims=True), t.shape)


def _rep(m, like):
    return jnp.concatenate([m] * (like.shape[1] // LANES), axis=1)


def _topk_select_cols(imp, blk_f, n_iter):
    sel = jnp.zeros(imp.shape, dtype=jnp.bool_)
    for _ in range(n_iter):
        m = jnp.max(imp, axis=0, keepdims=True)
        idx = jnp.min(jnp.where(imp == m, blk_f, float(imp.shape[0])), axis=0, keepdims=True)
        pick = blk_f == idx
        sel = jnp.logical_or(sel, pick)
        imp = jnp.where(pick, -jnp.inf, imp)
    return sel


def _ffn_kernel(x_ref, g_ref, wg_ref, wu_ref, wd_ref, o_ref, acc_ref):
    x = x_ref[...]
    h = _rms(x, g_ref[...]).astype(BF16)
    d_ff = wg_ref.shape[1]
    for c in range(d_ff // FF_CHUNK):
        sl = slice(c * FF_CHUNK, (c + 1) * FF_CHUNK)
        gate = jnp.dot(h, wg_ref[:, sl], preferred_element_type=F32)
        up = jnp.dot(h, wu_ref[:, sl], preferred_element_type=F32)
        a = (gate * _sigmoid(gate) * up).astype(BF16)
        part = jnp.dot(a, wd_ref[sl, :], preferred_element_type=F32)
        if c == 0:
            acc_ref[...] = part
        else:
            acc_ref[...] += part
    o_ref[...] = x + 0.5 * acc_ref[...]


def _half_ffn(x, g, wg, wu, wd, tm):
    m, d = x.shape
    d_ff = wg.shape[1]
    return pl.pallas_call(
        _ffn_kernel,
        grid=(m // tm,),
        in_specs=[pl.BlockSpec((tm, d), lambda i: (i, 0)), _const_spec((1, d)),
                  _const_spec((d, d_ff)), _const_spec((d, d_ff)), _const_spec((d_ff, d))],
        out_specs=pl.BlockSpec((tm, d), lambda i: (i, 0)),
        out_shape=jax.ShapeDtypeStruct((m, d), F32),
        scratch_shapes=[pltpu.VMEM((tm, d), F32)],
        compiler_params=_cparams("arbitrary"),
        name="half_ffn",
    )(x, g, wg, wu, wd)


def _inproj_kernel(x_ref, g_ref, wm_ref, wcg_ref, cw_ref, gn_ref, ws_ref, bs_ref, qn_ref, kn_ref,
                   yab_ref, q_ref, gate_ref, kc_ref, vc_ref, ks_ref, vs_ref, kww_ref, vww_ref, conv_ref,
                   kst_ref, vse_ref, kwt_ref, vwe_ref, zbuf_ref, *, tm, tiles_per_seq):
    j = pl.program_id(0) % tiles_per_seq

    @pl.when(j == 0)
    def _():
        zbuf_ref[0:SUBLANES, :] = jnp.zeros((SUBLANES, D_CONV), F32)

    h = _rms(x_ref[...], g_ref[...]).astype(BF16)
    z_a = jnp.dot(h, wm_ref[:, 1280:MAIN_COLS], preferred_element_type=F32)
    za = lambda lo, hi: z_a[:, lo - 1280:hi - 1280]

    def gmlp_and_conv():
        z_g = jnp.dot(h, wm_ref[:, 768:1280], preferred_element_type=F32)
        z_c = jnp.dot(h, wm_ref[:, 0:768], preferred_element_type=F32)
        gate_ref[...] = _sigmoid(jnp.dot(h, wcg_ref[...], preferred_element_type=F32))

        u = _gelu_tanh(z_g[:, 0:256])
        v = _rms(_gelu_tanh(z_g[:, 256:512]), gn_ref[...]).astype(BF16)
        tri = (lax.broadcasted_iota(jnp.int32, (CHUNK, CHUNK), 0)
               >= lax.broadcasted_iota(jnp.int32, (CHUNK, CHUNK), 1))
        wt = [jnp.where(tri, ws_ref[gi], 0.0).astype(BF16) for gi in range(GMLP_GROUPS)]
        lane_grp = lax.broadcasted_iota(jnp.int32, (CHUNK, D_GMLP), 1) // (D_GMLP // GMLP_GROUPS)
        bias = bs_ref[...]
        yb = []
        for ci in range(tm // CHUNK):
            vch = v[ci * CHUNK:(ci + 1) * CHUNK]
            s = bias
            for gi in range(GMLP_GROUPS):
                s = s + jnp.where(lane_grp == gi, jnp.dot(wt[gi], vch, preferred_element_type=F32), 0.0)
            yb.append(u[ci * CHUNK:(ci + 1) * CHUNK] * s)
        y_b = jnp.concatenate(yb, axis=0)

        a_b, a_c, a_x = z_c[:, 0:256], z_c[:, 256:512], z_c[:, 512:768]
        zc = a_c * a_x

        zbuf_ref[SUBLANES:SUBLANES + tm, :] = zc
        z1 = zbuf_ref[pl.ds(SUBLANES - 1, tm), :]
        z2 = zbuf_ref[pl.ds(SUBLANES - 2, tm), :]
        cw = cw_ref[...]
        y_a = a_b * (cw[0:1] * z2 + cw[1:2] * z1 + cw[2:3] * zc)
        tail = zbuf_ref[tm:tm + SUBLANES, :]
        zbuf_ref[0:SUBLANES, :] = tail
        conv_ref[...] = tail[SUBLANES - (CONV_W - 1):, :]
        yab_ref[...] = jnp.concatenate([y_a, y_b], axis=1).astype(BF16)

    ones_bd = _head_group_ones(256)
    qn = qn_ref[...]
    scale = HEAD_DIM ** -0.5
    q = jnp.concatenate([_head_rms(za(1280, 1536), qn, ones_bd),
                         _head_rms(za(1536, 1792), qn, ones_bd)], axis=1)
    q_ref[...] = (q * scale).astype(BF16)
    vs = za(2176, 2304)
    vw = za(2432, 2560)
    kn = _head_rms(jnp.concatenate([za(2048, 2176), za(2304, 2432)], axis=1), kn_ref[...], ones_bd)
    ks, kw = kn[:, 0:128], kn[:, 128:256]
    ks_t, kw_t, vw_t = ks.T, kw.T, vw.T
    kc_ref[...] = za(1792, 1920).T
    vc_ref[...] = za(1920, 2048).T
    ks_ref[...] = ks_t
    vs_ref[...] = vs.T

    if tm >= WINDOW:
        kww_ref[...] = kw_t[:, tm - WINDOW:]
        vww_ref[...] = vw_t[:, tm - WINDOW:]
    else:
        first = tiles_per_seq - WINDOW // tm

        @pl.when(j >= first)
        def _():
            off = pl.multiple_of((j - first) * tm, tm)
            kww_ref[:, pl.ds(off, tm)] = kw_t
            vww_ref[:, pl.ds(off, tm)] = vw_t

    pos = j * tm + lax.broadcasted_iota(jnp.int32, (1, tm), 1)
    prow = _pos_rows(pos)
    kst = ks_t.astype(BF16)
    kwt = kw_t.astype(BF16)
    n_blk_pad = kst_ref.shape[1] - 2 * HEAD_DIM
    blk_row = lax.broadcasted_iota(jnp.int32, (n_blk_pad, tm), 0)
    erows = jnp.where(blk_row == (pos >> 6), 1.0, 0.0).astype(BF16)
    for grp in range(N_KV_HEADS):
        sl = slice(grp * HEAD_DIM, (grp + 1) * HEAD_DIM)
        kst_ref[grp] = jnp.concatenate([erows, kst[sl], prow], axis=0)
        kwt_ref[grp] = jnp.concatenate([kwt[sl], prow], axis=0)
        vse_ref[grp] = _value_ext(vs, grp)
        vwe_ref[grp] = _value_ext(vw, grp)

    gmlp_and_conv()


def _in_proj_prompt(x, nb, seq, g, wm, wcg, cw, gn, ws, bs_tile, qn, kn12, tm):
    m, d = x.shape
    tps = seq // tm
    row = lambda i: (i, 0)
    rows = lambda w, dt: (pl.BlockSpec((tm, w), row), jax.ShapeDtypeStruct((m, w), dt))
    win = (pl.BlockSpec((None, D_KV, WINDOW), lambda i: (i // tps, 0, 0)),
           jax.ShapeDtypeStruct((nb, D_KV, WINDOW), F32))
    feat = (pl.BlockSpec((None, D_KV, tm), lambda i: (i // tps, 0, i % tps)),
            jax.ShapeDtypeStruct((nb, D_KV, seq), F32))
    kt = (pl.BlockSpec((None, N_KV_HEADS, 2 * HEAD_DIM, tm), lambda i: (i // tps, 0, 0, i % tps)),
          jax.ShapeDtypeStruct((nb, N_KV_HEADS, 2 * HEAD_DIM, seq), BF16))
    ve = (pl.BlockSpec((None, N_KV_HEADS, tm, LANES), lambda i: (i // tps, 0, i % tps, 0)),
          jax.ShapeDtypeStruct((nb, N_KV_HEADS, seq, LANES), BF16))
    conv = (pl.BlockSpec((None, CONV_W - 1, D_CONV), lambda i: (i // tps, 0, 0)),
            jax.ShapeDtypeStruct((nb, CONV_W - 1, D_CONV), F32))
    n_krows = _n_blk_pad(seq) + 2 * HEAD_DIM
    kt_sel = (pl.BlockSpec((None, N_KV_HEADS, n_krows, tm), lambda i: (i // tps, 0, 0, i % tps)),
              jax.ShapeDtypeStruct((nb, N_KV_HEADS, n_krows, seq), BF16))
    outs = [rows(512, BF16), rows(512, BF16), rows(256, F32), feat, feat, feat, feat, win, win, conv,
            kt_sel, ve, kt, ve]
    return pl.pallas_call(
        functools.partial(_inproj_kernel, tm=tm, tiles_per_seq=tps),
        grid=(m // tm,),
        in_specs=[pl.BlockSpec((tm, d), row), _const_spec((1, d)), _const_spec(wm.shape),
                  _const_spec(wcg.shape), _const_spec(cw.shape), _const_spec(gn.shape),
                  _const_spec(ws.shape), _const_spec(bs_tile.shape), _const_spec(qn.shape),
                  _const_spec(kn12.shape)],
        out_specs=[o[0] for o in outs],
        out_shape=[o[1] for o in outs],
        scratch_shapes=[pltpu.VMEM((tm + SUBLANES, D_CONV), F32)],
        compiler_params=_cparams("arbitrary"),
        name="in_proj_prompt",
    )(x, g, wm, wcg, cw, gn, ws, bs_tile, qn, kn12)


def _half_sums(pages, taps, seg):
    lhs = jnp.concatenate([(p * taps).astype(BF16) for p in pages], axis=1)
    return jnp.dot(lhs, seg, preferred_element_type=F32)


def _combine_halves(p0, p1):
    n = p0.shape[1]
    col = lax.broadcasted_iota(jnp.int32, p0.shape, 1)
    return jnp.where(col < n - 1, p0 + pltpu.roll(p1, n - 1, axis=1), 0.0)


def _head_rms_rows(x, g_col):
    frow = lax.broadcasted_iota(jnp.int32, x.shape, 0)
    sq = x * x
    ss0 = jnp.sum(jnp.where(frow < HEAD_DIM, sq, 0.0), axis=0, keepdims=True)
    ss1 = jnp.sum(jnp.where(frow >= HEAD_DIM, sq, 0.0), axis=0, keepdims=True)
    inv = lax.rsqrt(jnp.where(frow < HEAD_DIM, ss0, ss1) * (1.0 / HEAD_DIM) + EPS)
    return x * inv * g_col


def _compress_kernel(kc_ref, vc_ref, wt_ref, seg_ref, kn_ref, kct_ref, vce_ref, *, n_half):
    seg = seg_ref[...]
    rows_per_dot = seg.shape[0]

    def halves(src_ref, taps):
        parts = []
        for c in range(src_ref.shape[1] // rows_per_dot):
            pages = [src_ref[:, c * rows_per_dot + k * PAGE_SIZE:c * rows_per_dot + (k + 1) * PAGE_SIZE]
                     for k in range(rows_per_dot // PAGE_SIZE)]
            parts.append(_half_sums(pages, taps, seg))
        return jnp.concatenate(parts, axis=1)

    kc = _combine_halves(halves(kc_ref, wt_ref[0]), halves(kc_ref, wt_ref[1]))
    vc = _combine_halves(halves(vc_ref, wt_ref[2]), halves(vc_ref, wt_ref[3]))
    kc = _head_rms_rows(kc, kn_ref[...]).astype(BF16)
    vc_rows = vc.T
    cmp_end = lax.broadcasted_iota(jnp.int32, (1, n_half), 1) * CMP_STRIDE + (CMP_LEN - 1)
    prow = _pos_rows(cmp_end)
    for grp in range(N_KV_HEADS):
        kct_ref[grp] = jnp.concatenate([kc[grp * HEAD_DIM:(grp + 1) * HEAD_DIM], prow], axis=0)
        vce_ref[grp] = _value_ext(vc_rows, grp)


def _compress_prompt(kc_t, vc_t, nb, seq, wt, seg, kn0_col):
    n_half = seq // CMP_STRIDE
    return pl.pallas_call(
        functools.partial(_compress_kernel, n_half=n_half),
        grid=(nb,),
        in_specs=[pl.BlockSpec((None, D_KV, seq), lambda b: (b, 0, 0)),
                  pl.BlockSpec((None, D_KV, seq), lambda b: (b, 0, 0)),
                  _const_spec(wt.shape), _const_spec(seg.shape), _const_spec(kn0_col.shape)],
        out_specs=[pl.BlockSpec((None, N_KV_HEADS, 2 * HEAD_DIM, n_half), lambda b: (b, 0, 0, 0)),
                   pl.BlockSpec((None, N_KV_HEADS, n_half, LANES), lambda b: (b, 0, 0, 0))],
        out_shape=[jax.ShapeDtypeStruct((nb, N_KV_HEADS, 2 * HEAD_DIM, n_half), BF16),
                   jax.ShapeDtypeStruct((nb, N_KV_HEADS, n_half, LANES), BF16)],
        compiler_params=_cparams("arbitrary"),
        name="compress_prompt",
    )(kc_t, vc_t, wt, seg, kn0_col)


def _attn_kernel(q_ref, gate_ref, slope_ref, wb_ref, kct_ref, vce_ref, kst_ref, vse_ref, kwt_ref, vwe_ref, o_ref,
                 qx_ref, m_ref, acc_ref, oc_ref, ow_ref, list_ref, *, n_cmp_pad, n_blk_pad):
    qb = Q_BLOCK
    rows = HEADS_PER_KV * qb
    i = pl.program_id(2)
    p0 = i * qb

    q = q_ref[...].astype(F32)
    lane = lax.broadcasted_iota(jnp.int32, (qb, LANES), 1)
    parts = []
    for hp in range(HEADS_PER_KV):
        col = q[:, (hp // 2) * LANES:(hp // 2 + 1) * LANES]
        if hp % 2 == 1:
            col = _swap_halves(col)
        parts.append(jnp.where(lane < HEAD_DIM, col, 0.0))
    qx = (jnp.concatenate(parts, axis=0) + slope_ref[...]).astype(BF16)

    t_q = p0 + lax.broadcasted_iota(jnp.int32, (qb, 1), 0)
    t_rows = jnp.concatenate([t_q] * HEADS_PER_KV, axis=0)

    n_win = WINDOW + qb
    wstart = pl.multiple_of(jnp.maximum(p0 - WINDOW, 0), qb)
    last = p0 // KEY_CHUNK
    n_chunks = list_ref.shape[0] - 2
    blocks_per_chunk = KEY_CHUNK // SEL_LEN

    def compressed_and_select(n_col):
        n_blk = n_col // (SEL_LEN // CMP_STRIDE)
        s_c = jnp.dot(qx, kct_ref[:, 0:n_col], preferred_element_type=F32)
        s_w = (jnp.dot(qx, kwt_ref[:, pl.ds(wstart, n_win)], preferred_element_type=F32)
               + jnp.concatenate([wb_ref[...]] * HEADS_PER_KV, axis=0))
        cmp_end = lax.broadcasted_iota(jnp.int32, (1, n_col), 1) * CMP_STRIDE + (CMP_LEN - 1)
        vis = cmp_end <= t_rows
        s_c = jnp.where(vis, s_c, NEG)
        e_c = jnp.where(vis, jnp.exp(s_c - _rep(_row_max(s_c), s_c)), 0.0)
        p_c = e_c * _rep(1.0 / jnp.maximum(_row_sum(e_c), 1e-30), e_c)
        oc_ref[...] = jnp.dot(p_c.astype(BF16), vce_ref[0:n_col, :], preferred_element_type=F32)

        psum = p_c[0:qb]
        for hp in range(1, HEADS_PER_KV):
            psum = psum + p_c[hp * qb:(hp + 1) * qb]
        bidx = lax.broadcasted_iota(jnp.int32, (n_blk, n_col), 0)
        cidx = lax.broadcasted_iota(jnp.int32, (n_blk, n_col), 1)
        ratio = SEL_LEN // CMP_STRIDE
        band_t = jnp.where((cidx >= ratio * bidx - 1) & (cidx <= ratio * bidx + ratio - 1)
                           & (cidx < n_cmp_pad - 1), 1.0, 0.0).astype(BF16)
        nt = (((1,), (1,)), ((), ()))
        imp = sum(lax.dot_general(band_t, part, nt, preferred_element_type=F32)
                  for part in _split3(psum))
        e_w = jnp.exp(s_w - _rep(_row_max(s_w), s_w))
        acc_w = jnp.dot(e_w.astype(BF16), vwe_ref[pl.ds(wstart, n_win), :], preferred_element_type=F32)
        ow_ref[...] = acc_w * (1.0 / _swap_halves(acc_w))

        blk = lax.broadcasted_iota(jnp.int32, (n_blk, qb), 0)
        t_lane = p0 + lax.broadcasted_iota(jnp.int32, (1, qb), 1)
        cur = t_lane >> 6
        forced = (blk == 0) | (blk == cur) | (blk == cur - 1)
        future = blk * SEL_LEN > t_lane
        imp = jnp.where(forced, -jnp.inf, imp)
        imp = jnp.where(future, NEG, imp)
        sel_t = (forced | _topk_select_cols(imp, blk.astype(F32), TOP_N - 3)) & jnp.logical_not(future)
        selneg = jnp.where(sel_t, 0.0, -MASK_BIG)
        if n_blk < n_blk_pad:
            selneg = jnp.concatenate([selneg, jnp.full((n_blk_pad - n_blk, qb), -MASK_BIG, F32)], axis=0)
        selneg = selneg.T.astype(BF16)
        qx_ref[:, 0:n_blk_pad] = jnp.concatenate([selneg] * HEADS_PER_KV, axis=0)

        sel_f = jnp.where(sel_t, 1.0, 0.0)
        n_list = jnp.int32(0)
        for c in range(min(n_chunks, n_blk // blocks_per_chunk)):
            used = jnp.max(sel_f[c * blocks_per_chunk:(c + 1) * blocks_per_chunk])
            active = jnp.logical_and(used > 0.0, c < last)
            list_ref[n_list] = jnp.where(active, c, last)
            n_list = n_list + active.astype(jnp.int32)
        list_ref[n_list] = last
        list_ref[n_chunks + 1] = n_list

    n_tiers = n_cmp_pad // LANES
    tier = jnp.minimum(i // (LANES * CMP_STRIDE // qb), n_tiers - 1)
    for k in range(n_tiers):
        @pl.when(tier == k)
        def _():
            compressed_and_select((k + 1) * LANES)

    n_list = list_ref[n_chunks + 1]
    qx_ref[:, n_blk_pad:] = qx


    m_ref[...] = jnp.full((rows, LANES), NEG, F32)
    acc_ref[...] = jnp.zeros((rows, LANES), F32)

    def scores(c):
        start = pl.multiple_of(c * KEY_CHUNK, KEY_CHUNK)
        return jnp.dot(qx_ref[...], kst_ref[:, pl.ds(start, KEY_CHUNK)], preferred_element_type=F32)

    def softmax_update(s, c):
        start = pl.multiple_of(c * KEY_CHUNK, KEY_CHUNK)
        m_old = m_ref[...]
        m_new = jnp.maximum(m_old, _row_max(s))
        p = jnp.exp(s - _rep(m_new, s)).astype(BF16)
        acc_ref[...] = (jnp.exp(m_old - m_new) * acc_ref[...]
                        + jnp.dot(p, vse_ref[pl.ds(start, KEY_CHUNK), :], preferred_element_type=F32))
        m_ref[...] = m_new

    kpos = last * KEY_CHUNK + lax.broadcasted_iota(jnp.int32, (1, KEY_CHUNK), 1)
    causal = lambda s: jnp.where(kpos <= t_rows, s, NEG)

    def pair(ca, cb, last_is_diagonal):
        s_a, s_b = scores(ca), scores(cb)
        softmax_update(s_a, ca)
        softmax_update(causal(s_b) if last_is_diagonal else s_b, cb)

    def body(j, carry):
        pair(list_ref[2 * j], list_ref[2 * j + 1], False)
        return carry

    lax.fori_loop(0, n_list // 2, body, 0)

    @pl.when(n_list % 2 == 1)
    def _():
        pair(list_ref[n_list - 1], last, True)

    @pl.when(n_list % 2 == 0)
    def _():
        softmax_update(causal(scores(last)), last)

    acc_s = acc_ref[...]

    o_s = acc_s * (1.0 / _swap_halves(acc_s))
    o_c = oc_ref[...]
    o_w = ow_ref[...]
    gate = gate_ref[...]
    res = []
    for hp in range(HEADS_PER_KV):
        sl = slice(hp * qb, (hp + 1) * qb)
        gc = gate[:, N_BRANCH * hp + 0:N_BRANCH * hp + 1]
        gs = gate[:, N_BRANCH * hp + 1:N_BRANCH * hp + 2]
        gw = gate[:, N_BRANCH * hp + 2:N_BRANCH * hp + 3]
        res.append(gc * o_c[sl] + gs * o_s[sl] + gw * o_w[sl])
    cols = [jnp.where(lane < HEAD_DIM, res[2 * k], _swap_halves(res[2 * k + 1])) for k in range(2)]
    o_ref[...] = jnp.concatenate(cols, axis=1).astype(o_ref.dtype)


def _window_bias():
    r = np.arange(Q_BLOCK)[:, None]
    j = np.arange(WINDOW + Q_BLOCK)[None, :]
    early = [j <= Q_BLOCK * v + r for v in range(WINDOW // Q_BLOCK)]
    steady = (j >= r) & (j <= r + WINDOW)
    return jnp.asarray(np.where(np.stack(early + [steady]), 0.0, NEG), F32)


def _attention_prompt(q, gates, slope_rows, wbias, kct, vce, kst, vse, kwt, vwe, nb, seq):
    nq = seq // Q_BLOCK
    n_cmp_pad = kct.shape[-1]
    n_blk_pad = _n_blk_pad(seq)
    rows = HEADS_PER_KV * Q_BLOCK
    n_var = wbias.shape[0]
    qspec = pl.BlockSpec((Q_BLOCK, HEADS_PER_KV * HEAD_DIM), lambda b, g, i: (b * nq + i, g))
    per_bg = lambda shape: pl.BlockSpec((None, None) + shape, lambda b, g, i: (b, g, 0, 0))
    return pl.pallas_call(
        functools.partial(_attn_kernel, n_cmp_pad=n_cmp_pad, n_blk_pad=n_blk_pad),
        grid=(nb, N_KV_HEADS, nq),
        in_specs=[qspec,
                  pl.BlockSpec((Q_BLOCK, LANES), lambda b, g, i: (b * nq + i, g)),
                  pl.BlockSpec((None, rows, LANES), lambda b, g, i: (g, 0, 0)),
                  pl.BlockSpec((None,) + wbias.shape[1:], lambda b, g, i: (jnp.minimum(i, n_var - 1), 0, 0)),
                  per_bg((2 * HEAD_DIM, n_cmp_pad)), per_bg((n_cmp_pad, LANES)),
                  per_bg((n_blk_pad + 2 * HEAD_DIM, seq)), per_bg((seq, LANES)),
                  per_bg((2 * HEAD_DIM, seq)), per_bg((seq, LANES))],
        out_specs=qspec,
        out_shape=jax.ShapeDtypeStruct(q.shape, BF16),
        scratch_shapes=[pltpu.VMEM((rows, n_blk_pad + LANES), BF16), pltpu.VMEM((rows, LANES), F32),
                        pltpu.VMEM((rows, LANES), F32), pltpu.VMEM((rows, LANES), F32),
                        pltpu.VMEM((rows, LANES), F32), pltpu.SMEM((seq // KEY_CHUNK + 2,), jnp.int32)],
        compiler_params=_cparams("arbitrary", "arbitrary", "arbitrary"),
        name="attention_prompt",
    )(q, gates, slope_rows, wbias, kct, vce, kst, vse, kwt, vwe)


def _merge_kernel(x_ref, yab_ref, yc_ref, g_ref, wmg_ref, wb_ref, wo_ref, o_ref):
    x = x_ref[...]
    d = x.shape[1]
    h = _rms(x, g_ref[...]).astype(BF16)
    yab = yab_ref[...]
    branches = (jnp.dot(yab[:, 0:D_CONV], wb_ref[0:D_CONV, :], preferred_element_type=F32),
                jnp.dot(yab[:, D_CONV:], wb_ref[D_CONV:D_CONV + D_GMLP, :], preferred_element_type=F32),
                jnp.dot(yc_ref[...], wb_ref[D_CONV + D_GMLP:, :], preferred_element_type=F32))
    merged = None
    for k, y in enumerate(branches):
        gk = _sigmoid(jnp.dot(h, wmg_ref[:, k * d:(k + 1) * d], preferred_element_type=F32))
        merged = gk * y if merged is None else merged + gk * y
    o_ref[...] = x + jnp.dot(merged.astype(BF16), wo_ref[...], preferred_element_type=F32)


def _merge(x, yab, yc, g, wmg, wb, wo, tm):
    m, d = x.shape
    row = lambda i: (i, 0)
    return pl.pallas_call(
        _merge_kernel,
        grid=(m // tm,),
        in_specs=[pl.BlockSpec((tm, d), row), pl.BlockSpec((tm, yab.shape[1]), row),
                  pl.BlockSpec((tm, yc.shape[1]), row), _const_spec((1, d)), _const_spec(wmg.shape),
                  _const_spec(wb.shape), _const_spec(wo.shape)],
        out_specs=pl.BlockSpec((tm, d), row),
        out_shape=jax.ShapeDtypeStruct((m, d), F32),
        compiler_params=_cparams("arbitrary"),
        name="merge_out",
    )(x, yab, yc, g, wmg, wb, wo)


def _inproj_sample_kernel(x_ref, g_ref, wm_ref, wcg_ref, cw_ref, st0_ref, st1_ref, gn_ref, ws0_ref, bs0_ref,
                          qn_ref, kn_ref, yab_ref, q_ref, gate_ref, kc_ref, vc_ref, ks_ref, vs_ref, kw_ref,
                          vw_ref, zc_ref, vrow_ref):
    h = _rms(x_ref[...], g_ref[...]).astype(BF16)
    z = jnp.dot(h, wm_ref[...], preferred_element_type=F32)
    gate_ref[...] = _sigmoid(jnp.dot(h, wcg_ref[...], preferred_element_type=F32))
    a_b, a_c, a_x = z[:, 0:256], z[:, 256:512], z[:, 512:768]
    zc = a_c * a_x
    cw = cw_ref[...]
    y_a = a_b * (cw[0:1] * st0_ref[...] + cw[1:2] * st1_ref[...] + cw[2:3] * zc)
    zc_ref[...] = zc
    u = _gelu_tanh(z[:, 768:1024])
    v = _rms(_gelu_tanh(z[:, 1024:1280]), gn_ref[...])
    vrow_ref[...] = v
    y_b = u * (ws0_ref[...] * v + bs0_ref[...])
    yab_ref[...] = jnp.concatenate([y_a, y_b], axis=1).astype(BF16)
    ones_bd = _head_group_ones(256)
    qn = qn_ref[...]
    q = jnp.concatenate([_head_rms(z[:, 1280:1536], qn, ones_bd),
                         _head_rms(z[:, 1536:1792], qn, ones_bd)], axis=1)
    q_ref[...] = q * (HEAD_DIM ** -0.5)
    kc_ref[...] = z[:, 1792:1920]
    vc_ref[...] = z[:, 1920:2048]
    vs_ref[...] = z[:, 2176:2304]
    vw_ref[...] = z[:, 2432:2560]
    kn = _head_rms(jnp.concatenate([z[:, 2048:2176], z[:, 2304:2432]], axis=1), kn_ref[...], ones_bd)
    ks_ref[...] = kn[:, 0:128]
    kw_ref[...] = kn[:, 128:256]


def _in_proj_sample(x, g, wm, wcg, cw, st0, st1, gn, ws0, bs0, qn, kn12):
    m = x.shape[0]
    ins = (x, g, wm, wcg, cw, st0, st1, gn, ws0, bs0, qn, kn12)
    sd = lambda w, dt=F32: jax.ShapeDtypeStruct((m, w), dt)
    out_shape = [sd(512, BF16), sd(512), sd(256), sd(128), sd(128), sd(128), sd(128), sd(128), sd(128),
                 sd(256), sd(256)]
    return pl.pallas_call(
        _inproj_sample_kernel,
        grid=(1,),
        in_specs=[_const_spec(a.shape) for a in ins],
        out_specs=[_const_spec(s.shape) for s in out_shape],
        out_shape=out_shape,
        compiler_params=_cparams("arbitrary"),
        name="in_proj_sample",
    )(*ins)


def _head_slopes():
    hrow = lax.broadcasted_iota(jnp.int32, (N_HEADS, 1), 0)
    return lax.bitcast_convert_type((126 - hrow) << 23, F32)


def _sample_cmp_kernel(pt_ref, *refs, n_half, t_pos):
    del pt_ref
    pp = CMP_PAGES
    kpages, vpages = refs[0:pp], refs[pp:2 * pp]
    qz_ref, wt_ref, seg_ref, kn_ref = refs[2 * pp:2 * pp + 4]
    oc_ref, sel_ref = refs[2 * pp + 4:2 * pp + 6]
    p0k_ref, p1k_ref, p0v_ref, p1v_ref = refs[2 * pp + 6:]
    s = pl.program_id(1)
    halves = pp * PAGE_SIZE // CMP_STRIDE
    off = pl.multiple_of(s * halves, halves)
    seg = seg_ref[...]

    kp = [r[...] for r in kpages]
    vp = [r[...] for r in vpages]
    p0k_ref[:, pl.ds(off, halves)] = _half_sums(kp, wt_ref[0], seg)
    p1k_ref[:, pl.ds(off, halves)] = _half_sums(kp, wt_ref[1], seg)
    p0v_ref[:, pl.ds(off, halves)] = _half_sums(vp, wt_ref[2], seg)
    p1v_ref[:, pl.ds(off, halves)] = _half_sums(vp, wt_ref[3], seg)

    @pl.when(s == pl.num_programs(1) - 1)
    def _():
        kc = _head_rms_rows(_combine_halves(p0k_ref[...], p1k_ref[...]), kn_ref[...])
        vc = _combine_halves(p0v_ref[...], p1v_ref[...])
        qz = qz_ref[...].astype(BF16)
        s_c = jnp.dot(qz, kc.astype(BF16), preferred_element_type=F32)
        cmp_end = lax.broadcasted_iota(jnp.int32, (1, n_half), 1) * CMP_STRIDE + (CMP_LEN - 1)
        d_c = t_pos - cmp_end
        vis = d_c >= 0
        s_c = jnp.where(vis, s_c - _head_slopes() * d_c.astype(F32), NEG)
        e_c = jnp.where(vis, jnp.exp(s_c - jnp.max(s_c, axis=1, keepdims=True)), 0.0)
        p_c = e_c * (1.0 / jnp.maximum(jnp.sum(e_c, axis=1, keepdims=True), 1e-30))
        nt = (((1,), (1,)), ((), ()))
        oc_ref[...] = lax.dot_general(p_c.astype(BF16), vc.astype(BF16), nt, preferred_element_type=F32)

        hrow = lax.broadcasted_iota(jnp.int32, p_c.shape, 0)
        ps0 = jnp.sum(jnp.where(hrow < HEADS_PER_KV, p_c, 0.0), axis=0, keepdims=True)
        ps1 = jnp.sum(jnp.where(hrow >= HEADS_PER_KV, p_c, 0.0), axis=0, keepdims=True)
        prow = lax.broadcasted_iota(jnp.int32, (LANES, n_half), 0)
        psum = jnp.where(prow == 0, ps0, jnp.where(prow == 1, ps1, 0.0))
        n_blk_pad = sel_ref.shape[1]
        n_sel = t_pos // SEL_LEN + 1
        bidx = lax.broadcasted_iota(jnp.int32, (n_blk_pad, n_half), 0)
        cidx = lax.broadcasted_iota(jnp.int32, (n_blk_pad, n_half), 1)
        ratio = SEL_LEN // CMP_STRIDE
        band_t = jnp.where((cidx >= ratio * bidx - 1) & (cidx <= ratio * bidx + ratio - 1)
                           & (cidx < n_half - 1), 1.0, 0.0).astype(BF16)
        imp = sum(lax.dot_general(band_t, part, nt, preferred_element_type=F32) for part in _split3(psum))
        blk = lax.broadcasted_iota(jnp.int32, imp.shape, 0)
        cur = t_pos // SEL_LEN
        forced = (blk == 0) | (blk == cur) | (blk == cur - 1)
        future = blk * SEL_LEN > t_pos
        imp = jnp.where(forced, -NEG, imp)
        imp = jnp.where(future, NEG, imp)
        imp = jnp.where(blk < n_sel, imp, -jnp.inf)
        sel_t = (_topk_select_cols(imp, blk.astype(F32), min(TOP_N, n_sel))
                 & jnp.logical_not(future) & (blk < n_sel))
        sel_ref[...] = jnp.where(sel_t, 1.0, 0.0).T[0:SUBLANES]


def _page_specs(layer, n, table_col):
    def spec(k):
        return pl.BlockSpec((None, None, D_KV, PAGE_SIZE),
                            lambda b, s, *tabs: (layer, table_col(tabs, b, s * n + k), 0, 0))
    return [spec(k) for k in range(n)]


def _sample_cmp(page_table, cache_k, cache_v, layer, qz, wt, seg, kn0_col, t_pos):
    nb, n_pages = page_table.shape
    n_half = n_pages * PAGE_SIZE // CMP_STRIDE
    n_blk_pad = -(-(t_pos // SEL_LEN + 1) // LANES) * LANES
    pp = CMP_PAGES
    per_b = lambda shape: pl.BlockSpec((None,) + shape, lambda b, s, pt: (b, 0, 0))
    const = lambda shape: pl.BlockSpec(shape, lambda b, s, pt: (0,) * len(shape))
    pages = lambda: _page_specs(layer, pp, lambda tabs, b, j: tabs[0][b, j])
    grid_spec = pltpu.PrefetchScalarGridSpec(
        num_scalar_prefetch=1,
        grid=(nb, n_pages // pp),
        in_specs=pages() + pages()
        + [per_b((N_HEADS, LANES)), const(wt.shape), const(seg.shape), const(kn0_col.shape)],
        out_specs=[per_b((N_HEADS, LANES)), per_b((SUBLANES, n_blk_pad))],
        scratch_shapes=[pltpu.VMEM((D_KV, n_half), F32)] * 4,
    )
    return pl.pallas_call(
        functools.partial(_sample_cmp_kernel, n_half=n_half, t_pos=t_pos),
        grid_spec=grid_spec,
        out_shape=[jax.ShapeDtypeStruct((nb, N_HEADS, LANES), F32),
                   jax.ShapeDtypeStruct((nb, SUBLANES, n_blk_pad), F32)],
        compiler_params=_cparams("arbitrary", "arbitrary"),
        name="sample_cmp",
    )(page_table, *([cache_k] * pp), *([cache_v] * pp), qz, wt, seg, kn0_col)


def _sample_sel_kernel(phys_ref, lp_ref, *refs, t_pos):
    del phys_ref
    pp = SEL_PAGES
    kpages, vpages = refs[0:pp], refs[pp:2 * pp]
    qz_ref, sel_ref, kn_ref, vn_ref, o_ref, m_ref, l_ref, acc_ref = refs[2 * pp:]
    b, s = pl.program_id(0), pl.program_id(1)
    n_keys = pp * PAGE_SIZE

    @pl.when(s == 0)
    def _():
        m_ref[...] = jnp.full(m_ref.shape, NEG, F32)
        l_ref[...] = jnp.zeros(l_ref.shape, F32)
        acc_ref[...] = jnp.zeros(acc_ref.shape, F32)

    lane = lax.broadcasted_iota(jnp.int32, (1, PAGE_SIZE), 1)
    lps = [lp_ref[b, s * pp + k] for k in range(pp)]
    kpos = jnp.concatenate([lp * PAGE_SIZE + lane for lp in lps], axis=1)
    kblk = jnp.concatenate([lp * (PAGE_SIZE // SEL_LEN) + lane // SEL_LEN for lp in lps], axis=1)
    qz = qz_ref[...].astype(BF16)
    kk = jnp.concatenate([r[...] for r in kpages], axis=1).astype(BF16)
    vv = jnp.concatenate([r[...] for r in vpages], axis=1).astype(BF16)
    sc = jnp.dot(qz, kk, preferred_element_type=F32)
    sc = sc - _head_slopes() * (t_pos - kpos).astype(F32)
    n_blk_pad = sel_ref.shape[1]
    hrow = lax.broadcasted_iota(jnp.int32, (N_HEADS, n_blk_pad), 0)
    selv = sel_ref[...]
    sel_h = jnp.where(hrow < HEADS_PER_KV, selv[0:1], selv[1:2]).astype(BF16)
    erow = lax.broadcasted_iota(jnp.int32, (n_blk_pad, n_keys), 0)
    ok = jnp.dot(sel_h, jnp.where(erow == kblk, 1.0, 0.0).astype(BF16), preferred_element_type=F32) > 0.5
    sc = jnp.where(ok, sc, NEG)
    m_old = m_ref[...]
    m_new = jnp.maximum(m_old, jnp.max(sc, axis=1, keepdims=True))
    p = jnp.where(ok, jnp.exp(sc - m_new), 0.0)
    alpha = jnp.exp(m_old - m_new)
    l_ref[...] = alpha * l_ref[...] + jnp.sum(p, axis=1, keepdims=True)
    acc_ref[...] = alpha * acc_ref[...] + lax.dot_general(p.astype(BF16), vv, (((1,), (1,)), ((), ())),
                                                           preferred_element_type=F32)
    m_ref[...] = m_new

    @pl.when(s == pl.num_programs(1) - 1)
    def _():
        k_new = kn_ref[...].astype(BF16).astype(F32)
        s_new = jnp.sum(qz.astype(F32) * k_new, axis=1, keepdims=True)
        m_o = m_ref[...]
        m_n = jnp.maximum(m_o, s_new)
        a = jnp.exp(m_o - m_n)
        p_new = jnp.exp(s_new - m_n)
        l = a * l_ref[...] + p_new
        acc = a * acc_ref[...] + p_new.astype(BF16).astype(F32) * vn_ref[...].astype(BF16).astype(F32)
        o_ref[...] = acc * (1.0 / l)


def _needed_pages(sel, page_table):
    nb, n_pages = page_table.shape
    per_page = PAGE_SIZE // SEL_LEN
    flags = sel[:, :N_KV_HEADS, :n_pages * per_page] > 0.5
    need = flags.reshape(nb, N_KV_HEADS, n_pages, per_page).any(axis=(1, 3))
    n_slots = min(SEL_SLOTS, n_pages)
    order = jnp.argsort(jnp.logical_not(need), axis=1, stable=True)[:, :n_slots]
    count = need.sum(axis=1, keepdims=True)
    valid = jnp.arange(n_slots, dtype=jnp.int32)[None, :] < count
    logical = jnp.where(valid, order, -1).astype(jnp.int32)
    phys = jnp.take_along_axis(page_table, jnp.maximum(logical, 0), axis=1).astype(jnp.int32)
    return phys, logical


def _sample_sel(phys, logical, cache_k, cache_v, layer, qz, sel, ks_new, vs_new, t_pos):
    nb = phys.shape[0]
    pp = SEL_PAGES
    per_b = lambda shape: pl.BlockSpec((None,) + shape, lambda b, s, ph, lp: (b, 0, 0))
    pages = lambda: _page_specs(layer, pp, lambda tabs, b, j: tabs[0][b, j])
    grid_spec = pltpu.PrefetchScalarGridSpec(
        num_scalar_prefetch=2,
        grid=(nb, phys.shape[1] // pp),
        in_specs=pages() + pages()
        + [per_b((N_HEADS, LANES)), per_b(sel.shape[1:]), per_b((1, D_KV)), per_b((1, D_KV))],
        out_specs=per_b((N_HEADS, LANES)),
        scratch_shapes=[pltpu.VMEM((N_HEADS, 1), F32), pltpu.VMEM((N_HEADS, 1), F32),
                        pltpu.VMEM((N_HEADS, LANES), F32)],
    )
    return pl.pallas_call(
        functools.partial(_sample_sel_kernel, t_pos=t_pos),
        grid_spec=grid_spec,
        out_shape=jax.ShapeDtypeStruct((nb, N_HEADS, LANES), F32),
        compiler_params=_cparams("arbitrary", "arbitrary"),
        name="sample_sel",
    )(phys, logical, *([cache_k] * pp), *([cache_v] * pp), qz, sel, ks_new, vs_new)


def _sample_win_kernel(qz_ref, wk_ref, wv_ref, kn_ref, vn_ref, gate_ref, oc_ref, os_ref,
                       y_ref, wko_ref, wvo_ref):
    qz = qz_ref[...].astype(BF16)
    kwin, vwin = wk_ref[...], wv_ref[...]
    n_win = kwin.shape[1]
    nt = (((1,), (1,)), ((), ()))
    sc = jnp.dot(qz, kwin.astype(BF16), preferred_element_type=F32)
    dist = n_win - lax.broadcasted_iota(jnp.int32, (1, n_win), 1)
    sc = sc - _head_slopes() * dist.astype(F32)
    k_new, v_new = kn_ref[...], vn_ref[...]
    s_new = jnp.sum(qz.astype(F32) * k_new.astype(BF16).astype(F32), axis=1, keepdims=True)
    m = jnp.maximum(jnp.max(sc, axis=1, keepdims=True), s_new)
    p = jnp.exp(sc - m)
    p_new = jnp.exp(s_new - m)
    l = jnp.sum(p, axis=1, keepdims=True) + p_new
    acc = (lax.dot_general(p.astype(BF16), vwin.astype(BF16), nt, preferred_element_type=F32)
           + p_new.astype(BF16).astype(F32) * v_new.astype(BF16).astype(F32))
    o_w = acc * (1.0 / l)
    gate = gate_ref[...]
    y = gate[:, 0:1] * oc_ref[...] + gate[:, 1:2] * os_ref[...] + gate[:, 2:3] * o_w
    lane = lax.broadcasted_iota(jnp.int32, (1, LANES), 1)
    cols = []
    for k in range(N_HEADS // 2):
        grp = (2 * k) // HEADS_PER_KV
        even, odd = y[2 * k:2 * k + 1], y[2 * k + 1:2 * k + 2]
        low = even if grp == 0 else _swap_halves(even)
        high = odd if grp == 1 else _swap_halves(odd)
        cols.append(jnp.where(lane < HEAD_DIM, low, high))
    y_ref[...] = jnp.concatenate(cols, axis=1).astype(y_ref.dtype)
    eye = (lax.broadcasted_iota(jnp.int32, (D_KV, D_KV), 0) == lax.broadcasted_iota(jnp.int32, (D_KV, D_KV), 1))
    as_col = lambda r: jnp.sum(jnp.where(eye, r, 0.0), axis=1, keepdims=True)
    pos = lax.broadcasted_iota(jnp.int32, kwin.shape, 1)
    wko_ref[...] = jnp.where(pos == n_win - 1, as_col(k_new), pltpu.roll(kwin, n_win - 1, axis=1))
    wvo_ref[...] = jnp.where(pos == n_win - 1, as_col(v_new), pltpu.roll(vwin, n_win - 1, axis=1))


def _sample_win(qz, win_k, win_v, layer, kw_new, vw_new, gates_h, o_c, o_s):
    nb = qz.shape[0]
    n_win = win_k.shape[3]
    per_b = lambda shape: pl.BlockSpec((None,) + shape, lambda b: (b, 0, 0))
    cache = pl.BlockSpec((None, None, D_KV, n_win), lambda b: (layer, b, 0, 0))
    hl = (N_HEADS, LANES)
    return pl.pallas_call(
        _sample_win_kernel,
        grid=(nb,),
        in_specs=[per_b(hl), cache, cache, per_b((1, D_KV)), per_b((1, D_KV)), per_b(hl), per_b(hl), per_b(hl)],
        out_specs=[per_b((1, D_ATTN)), per_b((D_KV, n_win)), per_b((D_KV, n_win))],
        out_shape=[jax.ShapeDtypeStruct((nb, 1, D_ATTN), F32), jax.ShapeDtypeStruct((nb, D_KV, n_win), F32),
                   jax.ShapeDtypeStruct((nb, D_KV, n_win), F32)],
        compiler_params=_cparams("arbitrary"),
        name="sample_win",
    )(qz, win_k, win_v, kw_new, vw_new, gates_h, o_c, o_s)


def _slope_rows():
    out = np.zeros((N_KV_HEADS, HEADS_PER_KV * Q_BLOCK, LANES), np.float32)
    for g in range(N_KV_HEADS):
        for hp in range(HEADS_PER_KV):
            slope = 2.0 ** -(g * HEADS_PER_KV + hp + 1)
            out[g, hp * Q_BLOCK:(hp + 1) * Q_BLOCK, HEAD_DIM] = slope * 128.0
            out[g, hp * Q_BLOCK:(hp + 1) * Q_BLOCK, HEAD_DIM + 1] = slope
    return jnp.asarray(out)


def _tile_lanes(v, reps):
    return jnp.tile(v.reshape(1, -1), (1, reps))


def _heads_to_rows(q):
    n = q.shape[0]
    qh = q.reshape(n, N_HEADS, HEAD_DIM)
    z = jnp.zeros_like(qh[:, :HEADS_PER_KV])
    return jnp.concatenate([jnp.concatenate([qh[:, :HEADS_PER_KV], z], axis=-1),
                            jnp.concatenate([z, qh[:, HEADS_PER_KV:]], axis=-1)], axis=1)


def kernel(x_prompt, x_sample, cache_cmp_k, cache_cmp_v, cache_sel_k, cache_sel_v, cache_win_k, cache_win_v, state_conv, page_table, ffn1_norm, ffn1_w_gate, ffn1_w_up, ffn1_w_down, mix_norm, w_in, conv_w, gmlp_norm, gmlp_ws, gmlp_bs, q_norm, k_norm, cmp_wk, cmp_wv, w_branch, w_out, ffn2_norm, ffn2_w_gate, ffn2_w_up, ffn2_w_down):
    nb, seq, d = x_prompt.shape
    ns = x_sample.shape[0]
    depth = w_in.shape[0]
    t_pos = page_table.shape[1] * PAGE_SIZE
    assert x_sample.shape[1] == 1 and cache_win_k.shape[2] == WINDOW
    assert seq % (CMP_STRIDE * LANES) == 0 and t_pos % (CMP_STRIDE * LANES) == 0
    assert page_table.shape[1] % CMP_PAGES == 0
    tm = min(ROW_TILE, seq)

    xp = x_prompt.reshape(nb * seq, d)
    xs = x_sample.reshape(ns, d)
    feat_major = lambda c: jnp.transpose(c, (0, 1, 3, 4, 2)).reshape(depth, c.shape[1], D_KV, c.shape[2])
    ck, cv, sk, sv = (feat_major(c) for c in (cache_cmp_k, cache_cmp_v, cache_sel_k, cache_sel_v))
    wink, winv = feat_major(cache_win_k), feat_major(cache_win_v)
    slope_rows = _slope_rows()
    wbias = _window_bias()
    seg_rows = CMP_PAGES * PAGE_SIZE
    seg = jnp.asarray(np.arange(seg_rows)[:, None] // CMP_STRIDE == np.arange(seg_rows // CMP_STRIDE)[None, :], BF16)

    prompt_new = [[] for _ in range(7)]
    sample_new = [[] for _ in range(8)]
    for l in range(depth):
        bf = lambda w: w.astype(BF16)
        w_main = bf(w_in[l, :, :MAIN_COLS])
        cg = w_in[l, :, MAIN_COLS:MAIN_COLS + N_BRANCH * N_HEADS].reshape(d, N_KV_HEADS, HEADS_PER_KV * N_BRANCH)
        w_cg = bf(jnp.pad(cg, ((0, 0), (0, 0), (0, LANES - HEADS_PER_KV * N_BRANCH))).reshape(d, N_KV_HEADS * LANES))
        w_mg = bf(w_in[l, :, MAIN_COLS + N_BRANCH * N_HEADS:])
        wb, wo = bf(w_branch[l]), bf(w_out[l])
        f1 = (ffn1_norm[l].reshape(1, d), bf(ffn1_w_gate[l]), bf(ffn1_w_up[l]), bf(ffn1_w_down[l]))
        f2 = (ffn2_norm[l].reshape(1, d), bf(ffn2_w_gate[l]), bf(ffn2_w_up[l]), bf(ffn2_w_down[l]))
        mn = mix_norm[l].reshape(1, d)
        cw = conv_w[l]
        gn = gmlp_norm[l].reshape(1, D_GMLP)
        gdim = D_GMLP // GMLP_GROUPS
        bs_tile = jnp.repeat(gmlp_bs[l].T, gdim, axis=1)
        ws0 = jnp.repeat(gmlp_ws[l, :, 0, 0], gdim).reshape(1, D_GMLP)
        bs0 = jnp.repeat(gmlp_bs[l, :, 0], gdim).reshape(1, D_GMLP)
        qn = _tile_lanes(q_norm[l], 4)
        kn0 = _tile_lanes(k_norm[l, 0], 2)
        kn12 = jnp.concatenate([_tile_lanes(k_norm[l, 1], 2), _tile_lanes(k_norm[l, 2], 2)], axis=1)
        wk = cmp_wk[l].reshape(CMP_LEN, D_KV)
        wv = cmp_wv[l].reshape(CMP_LEN, D_KV)
        taps = lambda w: jnp.tile(w.T, (1, PAGE_SIZE // CMP_STRIDE))
        wt = jnp.stack([taps(wk[:CMP_STRIDE]), taps(wk[CMP_STRIDE:]), taps(wv[:CMP_STRIDE]), taps(wv[CMP_STRIDE:])])

        xp = _half_ffn(xp, *f1, tm)
        (yab, q, gates, kc, vc, ks, vs, kww, vww, conv_new, kst, vse, kwt, vwe) = _in_proj_prompt(
            xp, nb, seq, mn, w_main, w_cg, cw, gn, gmlp_ws[l], bs_tile, qn, kn12, tm)
        kct, vce = _compress_prompt(kc, vc, nb, seq, wt, seg, kn0.reshape(D_KV, 1))
        yc = _attention_prompt(q, gates, slope_rows, wbias, kct, vce, kst, vse, kwt, vwe, nb, seq)
        xp = _merge(xp, yab, yc, mn, w_mg, wb, wo, tm)
        xp = _half_ffn(xp, *f2, tm)
        for lst, a in zip(prompt_new, (kc, vc, ks, vs, kww, vww, conv_new)):
            lst.append(a)

        xs = _half_ffn(xs, *f1, ns)
        (yab_s, q_s, gates_s, kc_s, vc_s, ks_s, vs_s, kw_s, vw_s, zc_s, vrow_s) = _in_proj_sample(
            xs, mn, w_main, w_cg, cw, state_conv[l, :, 0], state_conv[l, :, 1], gn, ws0, bs0, qn, kn12)
        qz = _heads_to_rows(q_s)
        o_c, sel = _sample_cmp(page_table, ck, cv, l, qz, wt, seg, kn0.reshape(D_KV, 1), t_pos)
        phys, logical = _needed_pages(sel, page_table)
        o_s = _sample_sel(phys, logical, sk, sv, l, qz, sel, ks_s.reshape(ns, 1, D_KV),
                          vs_s.reshape(ns, 1, D_KV), t_pos)
        gh = gates_s.reshape(ns, N_KV_HEADS, LANES)[:, :, :HEADS_PER_KV * N_BRANCH].reshape(ns, N_HEADS, N_BRANCH)
        gh = jnp.pad(gh, ((0, 0), (0, 0), (0, LANES - N_BRANCH)))
        yc_s, wk_new, wv_new = _sample_win(qz, wink, winv, l, kw_s.reshape(ns, 1, D_KV),
                                           vw_s.reshape(ns, 1, D_KV), gh, o_c, o_s)
        yc_s = yc_s.reshape(ns, D_ATTN).astype(BF16)
        xs = _merge(xs, yab_s, yc_s, mn, w_mg, wb, wo, ns)
        xs = _half_ffn(xs, *f2, ns)
        kv5s = lambda a: a.reshape(ns, -1, N_KV_HEADS, HEAD_DIM)
        conv_s = jnp.stack([state_conv[l, :, 1], zc_s], axis=1)
        rows_major = lambda a: jnp.transpose(a.reshape(ns, N_KV_HEADS, HEAD_DIM, -1), (0, 3, 1, 2))
        for lst, a in zip(sample_new, (kv5s(kc_s), kv5s(vc_s), kv5s(ks_s), kv5s(vs_s), rows_major(wk_new),
                                       rows_major(wv_new), conv_s, vrow_s.reshape(ns, 1, D_GMLP))):
            lst.append(a)

    tokens_major = lambda a: jnp.transpose(a.reshape(depth, nb, N_KV_HEADS, HEAD_DIM, -1), (0, 1, 4, 2, 3))
    outs_p = [jnp.stack(a) for a in prompt_new]
    outs_p = [tokens_major(a) for a in outs_p[:6]] + outs_p[6:]
    outs_s = [jnp.stack(a) for a in sample_new]
    return (xp.reshape(nb, seq, d), xs.reshape(ns, 1, d), *outs_p, *outs_s)
```

```python
import functools

import numpy as np
import jax
import jax.numpy as jnp
from jax import lax
from jax.experimental import pallas as pl
from jax.experimental.pallas import tpu as pltpu

F32 = jnp.float32
BF16 = jnp.bfloat16

HEAD_DIM = 64
N_HEADS = 8
N_KV_HEADS = 2
HEADS_PER_KV = N_HEADS // N_KV_HEADS
D_CONV = 256
CONV_W = 3
D_GMLP = 256
GMLP_GROUPS = 4
CHUNK = 128
D_ATTN = N_HEADS * HEAD_DIM
D_KV = N_KV_HEADS * HEAD_DIM
CMP_LEN = 32
CMP_STRIDE = 16
SEL_LEN = 64
TOP_N = 16
WINDOW = 512
Q_BLOCK = 128
N_BRANCH = 3
PAGE_SIZE = 128
EPS = 1e-6
NEG = -1e30
MASK_BIG = 2.0 ** 100
MAIN_COLS = 5 * 256 + D_ATTN + 6 * D_KV

LANES = 128
SUBLANES = 8
ROW_TILE = 1024
FF_CHUNK = 256
KEY_CHUNK = 512
CMP_PAGES = 32
SEL_PAGES = 16
SEL_SLOTS = 32
VMEM_LIMIT = 56 * 1024 * 1024


def _cparams(*sem):
    return pltpu.CompilerParams(dimension_semantics=sem, vmem_limit_bytes=VMEM_LIMIT)


def _const_spec(shape):
    nd = len(shape)
    return pl.BlockSpec(shape, lambda *_: (0,) * nd, pipeline_mode=pl.Buffered(1))


def _rms(x, g):
    return x * lax.rsqrt(jnp.mean(x * x, axis=-1, keepdims=True) + EPS) * g


def _sigmoid(x):
    return 1.0 / (1.0 + jnp.exp(-x))


def _gelu_tanh(x):
    return 0.5 * x * (1.0 + jnp.tanh(0.7978845608028654 * (x + 0.044715 * (x * x * x))))


def _split3(x):
    hi = x.astype(BF16)
    r = x - hi.astype(F32)
    mid = r.astype(BF16)
    lo = (r - mid.astype(F32)).astype(BF16)
    return hi, mid, lo


def _exact_dot01(x, m01):
    hi, mid, lo = _split3(x)
    return (jnp.dot(hi, m01, preferred_element_type=F32) + jnp.dot(mid, m01, preferred_element_type=F32)
            + jnp.dot(lo, m01, preferred_element_type=F32))


def _head_group_ones(n):
    r = lax.broadcasted_iota(jnp.int32, (n, n), 0) // HEAD_DIM
    c = lax.broadcasted_iota(jnp.int32, (n, n), 1) // HEAD_DIM
    return jnp.where(r == c, 1.0, 0.0).astype(BF16)


def _head_rms(x, g, ones_bd):
    ssq = _exact_dot01(x * x, ones_bd)
    return x * lax.rsqrt(ssq * (1.0 / HEAD_DIM) + EPS) * g


def _swap_halves(x):
    return pltpu.roll(x, HEAD_DIM, axis=1)


def _value_ext(v, grp):
    lane = lax.broadcasted_iota(jnp.int32, v.shape, 1)
    src = v if grp == 0 else _swap_halves(v)
    return jnp.where(lane < HEAD_DIM, src, 1.0).astype(BF16)


def _pos_rows(pos):
    n = pos.shape[1]
    row = lax.broadcasted_iota(jnp.int32, (HEAD_DIM, n), 0)
    hi = (pos >> 7).astype(F32)
    lo = (pos & 127).astype(F32)
    return jnp.where(row == 0, hi, jnp.where(row == 1, lo, 0.0)).astype(BF16)


def _n_blk_pad(seq):
    return -(-(seq // SEL_LEN) // LANES) * LANES


def _fold_lane_tiles(x, op):
    t = x[:, 0:LANES]
    for k in range(1, x.shape[1] // LANES):
        t = op(t, x[:, k * LANES:(k + 1) * LANES])
    return t


def _row_max(x):
    t = _fold_lane_tiles(x, jnp.maximum)
    return jnp.broadcast_to(jnp.max(t, axis=1, keepdims=True), t.shape)


def _row_sum(x):
    t = _fold_lane_tiles(x, jnp.add)
    return jnp.broadcast_to(jnp.sum(t, axis=1, keepdims=True), t.shape)


def _rep(m, like):
    return jnp.concatenate([m] * (like.shape[1] // LANES), axis=1)


def _topk_select_cols(imp, blk_f, n_iter):
    sel = jnp.zeros(imp.shape, dtype=jnp.bool_)
    for _ in range(n_iter):
        m = jnp.max(imp, axis=0, keepdims=True)
        idx = jnp.min(jnp.where(imp == m, blk_f, float(imp.shape[0])), axis=0, keepdims=True)
        pick = blk_f == idx
        sel = jnp.logical_or(sel, pick)
        imp = jnp.where(pick, -jnp.inf, imp)
    return sel


def _ffn_kernel(x_ref, g_ref, wg_ref, wu_ref, wd_ref, o_ref, acc_ref):
    x = x_ref[...]
    h = _rms(x, g_ref[...]).astype(BF16)
    d_ff = wg_ref.shape[1]
    for c in range(d_ff // FF_CHUNK):
        sl = slice(c * FF_CHUNK, (c + 1) * FF_CHUNK)
        gate = jnp.dot(h, wg_ref[:, sl], preferred_element_type=F32)
        up = jnp.dot(h, wu_ref[:, sl], preferred_element_type=F32)
        a = (gate * _sigmoid(gate) * up).astype(BF16)
        part = jnp.dot(a, wd_ref[sl, :], preferred_element_type=F32)
        if c == 0:
            acc_ref[...] = part
        else:
            acc_ref[...] += part
    o_ref[...] = x + 0.5 * acc_ref[...]


def _half_ffn(x, g, wg, wu, wd, tm):
    m, d = x.shape
    d_ff = wg.shape[1]
    return pl.pallas_call(
        _ffn_kernel,
        grid=(m // tm,),
        in_specs=[pl.BlockSpec((tm, d), lambda i: (i, 0)), _const_spec((1, d)),
                  _const_spec((d, d_ff)), _const_spec((d, d_ff)), _const_spec((d_ff, d))],
        out_specs=pl.BlockSpec((tm, d), lambda i: (i, 0)),
        out_shape=jax.ShapeDtypeStruct((m, d), F32),
        scratch_shapes=[pltpu.VMEM((tm, d), F32)],
        compiler_params=_cparams("arbitrary"),
        name="half_ffn",
    )(x, g, wg, wu, wd)


def _inproj_kernel(x_ref, g_ref, wm_ref, wcg_ref, cw_ref, gn_ref, ws_ref, bs_ref, qn_ref, kn_ref,
                   yab_ref, q_ref, gate_ref, kc_ref, vc_ref, ks_ref, vs_ref, kww_ref, vww_ref, conv_ref,
                   kst_ref, vse_ref, kwt_ref, vwe_ref, zbuf_ref, *, tm, tiles_per_seq):
    j = pl.program_id(0) % tiles_per_seq

    @pl.when(j == 0)
    def _():
        zbuf_ref[0:SUBLANES, :] = jnp.zeros((SUBLANES, D_CONV), F32)

    h = _rms(x_ref[...], g_ref[...]).astype(BF16)
    z_a = jnp.dot(h, wm_ref[:, 1280:MAIN_COLS], preferred_element_type=F32)
    za = lambda lo, hi: z_a[:, lo - 1280:hi - 1280]

    def gmlp_and_conv():
        z_g = jnp.dot(h, wm_ref[:, 768:1280], preferred_element_type=F32)
        z_c = jnp.dot(h, wm_ref[:, 0:768], preferred_element_type=F32)
        gate_ref[...] = _sigmoid(jnp.dot(h, wcg_ref[...], preferred_element_type=F32))

        u = _gelu_tanh(z_g[:, 0:256])
        v = _rms(_gelu_tanh(z_g[:, 256:512]), gn_ref[...]).astype(BF16)
        tri = (lax.broadcasted_iota(jnp.int32, (CHUNK, CHUNK), 0)
               >= lax.broadcasted_iota(jnp.int32, (CHUNK, CHUNK), 1))
        wt = [jnp.where(tri, ws_ref[gi], 0.0).astype(BF16) for gi in range(GMLP_GROUPS)]
        lane_grp = lax.broadcasted_iota(jnp.int32, (CHUNK, D_GMLP), 1) // (D_GMLP // GMLP_GROUPS)
        bias = bs_ref[...]
        yb = []
        for ci in range(tm // CHUNK):
            vch = v[ci * CHUNK:(ci + 1) * CHUNK]
            s = bias
            for gi in range(GMLP_GROUPS):
                s = s + jnp.where(lane_grp == gi, jnp.dot(wt[gi], vch, preferred_element_type=F32), 0.0)
            yb.append(u[ci * CHUNK:(ci + 1) * CHUNK] * s)
        y_b = jnp.concatenate(yb, axis=0)

        a_b, a_c, a_x = z_c[:, 0:256], z_c[:, 256:512], z_c[:, 512:768]
        zc = a_c * a_x

        zbuf_ref[SUBLANES:SUBLANES + tm, :] = zc
        z1 = zbuf_ref[pl.ds(SUBLANES - 1, tm), :]
        z2 = zbuf_ref[pl.ds(SUBLANES - 2, tm), :]
        cw = cw_ref[...]
        y_a = a_b * (cw[0:1] * z2 + cw[1:2] * z1 + cw[2:3] * zc)
        tail = zbuf_ref[tm:tm + SUBLANES, :]
        zbuf_ref[0:SUBLANES, :] = tail
        conv_ref[...] = tail[SUBLANES - (CONV_W - 1):, :]
        yab_ref[...] = jnp.concatenate([y_a, y_b], axis=1).astype(BF16)

    ones_bd = _head_group_ones(256)
    qn = qn_ref[...]
    scale = HEAD_DIM ** -0.5
    q = jnp.concatenate([_head_rms(za(1280, 1536), qn, ones_bd),
                         _head_rms(za(1536, 1792), qn, ones_bd)], axis=1)
    q_ref[...] = (q * scale).astype(BF16)
    vs = za(2176, 2304)
    vw = za(2432, 2560)
    kn = _head_rms(jnp.concatenate([za(2048, 2176), za(2304, 2432)], axis=1), kn_ref[...], ones_bd)
    ks, kw = kn[:, 0:128], kn[:, 128:256]
    ks_t, kw_t, vw_t = ks.T, kw.T, vw.T
    kc_ref[...] = za(1792, 1920).T
    vc_ref[...] = za(1920, 2048).T
    ks_ref[...] = ks_t
    vs_ref[...] = vs.T

    if tm >= WINDOW:
        kww_ref[...] = kw_t[:, tm - WINDOW:]
        vww_ref[...] = vw_t[:, tm - WINDOW:]
    else:
        first = tiles_per_seq - WINDOW // tm

        @pl.when(j >= first)
        def _():
            off = pl.multiple_of((j - first) * tm, tm)
            kww_ref[:, pl.ds(off, tm)] = kw_t
            vww_ref[:, pl.ds(off, tm)] = vw_t

    pos = j * tm + lax.broadcasted_iota(jnp.int32, (1, tm), 1)
    prow = _pos_rows(pos)
    kst = ks_t.astype(BF16)
    kwt = kw_t.astype(BF16)
    n_blk_pad = kst_ref.shape[1] - 2 * HEAD_DIM
    blk_row = lax.broadcasted_iota(jnp.int32, (n_blk_pad, tm), 0)
    erows = jnp.where(blk_row == (pos >> 6), 1.0, 0.0).astype(BF16)
    for grp in range(N_KV_HEADS):
        sl = slice(grp * HEAD_DIM, (grp + 1) * HEAD_DIM)
        kst_ref[grp] = jnp.concatenate([erows, kst[sl], prow], axis=0)
        kwt_ref[grp] = jnp.concatenate([kwt[sl], prow], axis=0)
        vse_ref[grp] = _value_ext(vs, grp)
        vwe_ref[grp] = _value_ext(vw, grp)

    gmlp_and_conv()


def _in_proj_prompt(x, nb, seq, g, wm, wcg, cw, gn, ws, bs_tile, qn, kn12, tm):
    m, d = x.shape
    tps = seq // tm
    row = lambda i: (i, 0)
    rows = lambda w, dt: (pl.BlockSpec((tm, w), row), jax.ShapeDtypeStruct((m, w), dt))
    win = (pl.BlockSpec((None, D_KV, WINDOW), lambda i: (i // tps, 0, 0)),
           jax.ShapeDtypeStruct((nb, D_KV, WINDOW), F32))
    feat = (pl.BlockSpec((None, D_KV, tm), lambda i: (i // tps, 0, i % tps)),
            jax.ShapeDtypeStruct((nb, D_KV, seq), F32))
    kt = (pl.BlockSpec((None, N_KV_HEADS, 2 * HEAD_DIM, tm), lambda i: (i // tps, 0, 0, i % tps)),
          jax.ShapeDtypeStruct((nb, N_KV_HEADS, 2 * HEAD_DIM, seq), BF16))
    ve = (pl.BlockSpec((None, N_KV_HEADS, tm, LANES), lambda i: (i // tps, 0, i % tps, 0)),
          jax.ShapeDtypeStruct((nb, N_KV_HEADS, seq, LANES), BF16))
    conv = (pl.BlockSpec((None, CONV_W - 1, D_CONV), lambda i: (i // tps, 0, 0)),
            jax.ShapeDtypeStruct((nb, CONV_W - 1, D_CONV), F32))
    n_krows = _n_blk_pad(seq) + 2 * HEAD_DIM
    kt_sel = (pl.BlockSpec((None, N_KV_HEADS, n_krows, tm), lambda i: (i // tps, 0, 0, i % tps)),
              jax.ShapeDtypeStruct((nb, N_KV_HEADS, n_krows, seq), BF16))
    outs = [rows(512, BF16), rows(512, BF16), rows(256, F32), feat, feat, feat, feat, win, win, conv,
            kt_sel, ve, kt, ve]
    return pl.pallas_call(
        functools.partial(_inproj_kernel, tm=tm, tiles_per_seq=tps),
        grid=(m // tm,),
        in_specs=[pl.BlockSpec((tm, d), row), _const_spec((1, d)), _const_spec(wm.shape),
                  _const_spec(wcg.shape), _const_spec(cw.shape), _const_spec(gn.shape),
                  _const_spec(ws.shape), _const_spec(bs_tile.shape), _const_spec(qn.shape),
                  _const_spec(kn12.shape)],
        out_specs=[o[0] for o in outs],
        out_shape=[o[1] for o in outs],
        scratch_shapes=[pltpu.VMEM((tm + SUBLANES, D_CONV), F32)],
        compiler_params=_cparams("arbitrary"),
        name="in_proj_prompt",
    )(x, g, wm, wcg, cw, gn, ws, bs_tile, qn, kn12)


def _half_sums(pages, taps, seg):
    lhs = jnp.concatenate([(p * taps).astype(BF16) for p in pages], axis=1)
    return jnp.dot(lhs, seg, preferred_element_type=F32)


def _combine_halves(p0, p1):
    n = p0.shape[1]
    col = lax.broadcasted_iota(jnp.int32, p0.shape, 1)
    return jnp.where(col < n - 1, p0 + pltpu.roll(p1, n - 1, axis=1), 0.0)


def _head_rms_rows(x, g_col):
    frow = lax.broadcasted_iota(jnp.int32, x.shape, 0)
    sq = x * x
    ss0 = jnp.sum(jnp.where(frow < HEAD_DIM, sq, 0.0), axis=0, keepdims=True)
    ss1 = jnp.sum(jnp.where(frow >= HEAD_DIM, sq, 0.0), axis=0, keepdims=True)
    inv = lax.rsqrt(jnp.where(frow < HEAD_DIM, ss0, ss1) * (1.0 / HEAD_DIM) + EPS)
    return x * inv * g_col


def _compress_kernel(kc_ref, vc_ref, wt_ref, seg_ref, kn_ref, kct_ref, vce_ref, *, n_half):
    seg = seg_ref[...]
    rows_per_dot = seg.shape[0]

    def halves(src_ref, taps):
        parts = []
        for c in range(src_ref.shape[1] // rows_per_dot):
            pages = [src_ref[:, c * rows_per_dot + k * PAGE_SIZE:c * rows_per_dot + (k + 1) * PAGE_SIZE]
                     for k in range(rows_per_dot // PAGE_SIZE)]
            parts.append(_half_sums(pages, taps, seg))
        return jnp.concatenate(parts, axis=1)

    kc = _combine_halves(halves(kc_ref, wt_ref[0]), halves(kc_ref, wt_ref[1]))
    vc = _combine_halves(halves(vc_ref, wt_ref[2]), halves(vc_ref, wt_ref[3]))
    kc = _head_rms_rows(kc, kn_ref[...]).astype(BF16)
    vc_rows = vc.T
    cmp_end = lax.broadcasted_iota(jnp.int32, (1, n_half), 1) * CMP_STRIDE + (CMP_LEN - 1)
    prow = _pos_rows(cmp_end)
    for grp in range(N_KV_HEADS):
        kct_ref[grp] = jnp.concatenate([kc[grp * HEAD_DIM:(grp + 1) * HEAD_DIM], prow], axis=0)
        vce_ref[grp] = _value_ext(vc_rows, grp)


def _compress_prompt(kc_t, vc_t, nb, seq, wt, seg, kn0_col):
    n_half = seq // CMP_STRIDE
    return pl.pallas_call(
        functools.partial(_compress_kernel, n_half=n_half),
        grid=(nb,),
        in_specs=[pl.BlockSpec((None, D_KV, seq), lambda b: (b, 0, 0)),
                  pl.BlockSpec((None, D_KV, seq), lambda b: (b, 0, 0)),
                  _const_spec(wt.shape), _const_spec(seg.shape), _const_spec(kn0_col.shape)],
        out_specs=[pl.BlockSpec((None, N_KV_HEADS, 2 * HEAD_DIM, n_half), lambda b: (b, 0, 0, 0)),
                   pl.BlockSpec((None, N_KV_HEADS, n_half, LANES), lambda b: (b, 0, 0, 0))],
        out_shape=[jax.ShapeDtypeStruct((nb, N_KV_HEADS, 2 * HEAD_DIM, n_half), BF16),
                   jax.ShapeDtypeStruct((nb, N_KV_HEADS, n_half, LANES), BF16)],
        compiler_params=_cparams("arbitrary"),
        name="compress_prompt",
    )(kc_t, vc_t, wt, seg, kn0_col)


def _attn_kernel(q_ref, gate_ref, slope_ref, wb_ref, kct_ref, vce_ref, kst_ref, vse_ref, kwt_ref, vwe_ref, o_ref,
                 qx_ref, m_ref, acc_ref, oc_ref, ow_ref, list_ref, *, n_cmp_pad, n_blk_pad):
    qb = Q_BLOCK
    rows = HEADS_PER_KV * qb
    i = pl.program_id(2)
    p0 = i * qb

    q = q_ref[...].astype(F32)
    lane = lax.broadcasted_iota(jnp.int32, (qb, LANES), 1)
    parts = []
    for hp in range(HEADS_PER_KV):
        col = q[:, (hp // 2) * LANES:(hp // 2 + 1) * LANES]
        if hp % 2 == 1:
            col = _swap_halves(col)
        parts.append(jnp.where(lane < HEAD_DIM, col, 0.0))
    qx = (jnp.concatenate(parts, axis=0) + slope_ref[...]).astype(BF16)

    t_q = p0 + lax.broadcasted_iota(jnp.int32, (qb, 1), 0)
    t_rows = jnp.concatenate([t_q] * HEADS_PER_KV, axis=0)

    n_win = WINDOW + qb
    wstart = pl.multiple_of(jnp.maximum(p0 - WINDOW, 0), qb)
    last = p0 // KEY_CHUNK
    n_chunks = list_ref.shape[0] - 2
    blocks_per_chunk = KEY_CHUNK // SEL_LEN

    def compressed_and_select(n_col):
        n_blk = n_col // (SEL_LEN // CMP_STRIDE)
        s_c = jnp.dot(qx, kct_ref[:, 0:n_col], preferred_element_type=F32)
        s_w = (jnp.dot(qx, kwt_ref[:, pl.ds(wstart, n_win)], preferred_element_type=F32)
               + jnp.concatenate([wb_ref[...]] * HEADS_PER_KV, axis=0))
        cmp_end = lax.broadcasted_iota(jnp.int32, (1, n_col), 1) * CMP_STRIDE + (CMP_LEN - 1)
        vis = cmp_end <= t_rows
        s_c = jnp.where(vis, s_c, NEG)
        e_c = jnp.where(vis, jnp.exp(s_c - _rep(_row_max(s_c), s_c)), 0.0)
        p_c = e_c * _rep(1.0 / jnp.maximum(_row_sum(e_c), 1e-30), e_c)
        oc_ref[...] = jnp.dot(p_c.astype(BF16), vce_ref[0:n_col, :], preferred_element_type=F32)

        psum = p_c[0:qb]
        for hp in range(1, HEADS_PER_KV):
            psum = psum + p_c[hp * qb:(hp + 1) * qb]
        bidx = lax.broadcasted_iota(jnp.int32, (n_blk, n_col), 0)
        cidx = lax.broadcasted_iota(jnp.int32, (n_blk, n_col), 1)
        ratio = SEL_LEN // CMP_STRIDE
        band_t = jnp.where((cidx >= ratio * bidx - 1) & (cidx <= ratio * bidx + ratio - 1)
                           & (cidx < n_cmp_pad - 1), 1.0, 0.0).astype(BF16)
        nt = (((1,), (1,)), ((), ()))
        imp = sum(lax.dot_general(band_t, part, nt, preferred_element_type=F32)
                  for part in _split3(psum))
        e_w = jnp.exp(s_w - _rep(_row_max(s_w), s_w))
        acc_w = jnp.dot(e_w.astype(BF16), vwe_ref[pl.ds(wstart, n_win), :], preferred_element_type=F32)
        ow_ref[...] = acc_w * (1.0 / _swap_halves(acc_w))

        blk = lax.broadcasted_iota(jnp.int32, (n_blk, qb), 0)
        t_lane = p0 + lax.broadcasted_iota(jnp.int32, (1, qb), 1)
        cur = t_lane >> 6
        forced = (blk == 0) | (blk == cur) | (blk == cur - 1)
        future = blk * SEL_LEN > t_lane
        imp = jnp.where(forced, -jnp.inf, imp)
        imp = jnp.where(future, NEG, imp)
        sel_t = (forced | _topk_select_cols(imp, blk.astype(F32), TOP_N - 3)) & jnp.logical_not(future)
        selneg = jnp.where(sel_t, 0.0, -MASK_BIG)
        if n_blk < n_blk_pad:
            selneg = jnp.concatenate([selneg, jnp.full((n_blk_pad - n_blk, qb), -MASK_BIG, F32)], axis=0)
        selneg = selneg.T.astype(BF16)
        qx_ref[:, 0:n_blk_pad] = jnp.concatenate([selneg] * HEADS_PER_KV, axis=0)

        sel_f = jnp.where(sel_t, 1.0, 0.0)
        n_list = jnp.int32(0)
        for c in range(min(n_chunks, n_blk // blocks_per_chunk)):
            used = jnp.max(sel_f[c * blocks_per_chunk:(c + 1) * blocks_per_chunk])
            active = jnp.logical_and(used > 0.0, c < last)
            list_ref[n_list] = jnp.where(active, c, last)
            n_list = n_list + active.astype(jnp.int32)
        list_ref[n_list] = last
        list_ref[n_chunks + 1] = n_list

    n_tiers = n_cmp_pad // LANES
    tier = jnp.minimum(i // (LANES * CMP_STRIDE // qb), n_tiers - 1)
    for k in range(n_tiers):
        @pl.when(tier == k)
        def _():
            compressed_and_select((k + 1) * LANES)

    n_list = list_ref[n_chunks + 1]
    qx_ref[:, n_blk_pad:] = qx


    m_ref[...] = jnp.full((rows, LANES), NEG, F32)
    acc_ref[...] = jnp.zeros((rows, LANES), F32)

    def scores(c):
        start = pl.multiple_of(c * KEY_CHUNK, KEY_CHUNK)
        return jnp.dot(qx_ref[...], kst_ref[:, pl.ds(start, KEY_CHUNK)], preferred_element_type=F32)

    def softmax_update(s, c):
        start = pl.multiple_of(c * KEY_CHUNK, KEY_CHUNK)
        m_old = m_ref[...]
        m_new = jnp.maximum(m_old, _row_max(s))
        p = jnp.exp(s - _rep(m_new, s)).astype(BF16)
        acc_ref[...] = (jnp.exp(m_old - m_new) * acc_ref[...]
                        + jnp.dot(p, vse_ref[pl.ds(start, KEY_CHUNK), :], preferred_element_type=F32))
        m_ref[...] = m_new

    kpos = last * KEY_CHUNK + lax.broadcasted_iota(jnp.int32, (1, KEY_CHUNK), 1)
    causal = lambda s: jnp.where(kpos <= t_rows, s, NEG)

    def pair(ca, cb, last_is_diagonal):
        s_a, s_b = scores(ca), scores(cb)
        softmax_update(s_a, ca)
        softmax_update(causal(s_b) if last_is_diagonal else s_b, cb)

    def body(j, carry):
        pair(list_ref[2 * j], list_ref[2 * j + 1], False)
        return carry

    lax.fori_loop(0, n_list // 2, body, 0)

    @pl.when(n_list % 2 == 1)
    def _():
        pair(list_ref[n_list - 1], last, True)

    @pl.when(n_list % 2 == 0)
    def _():
        softmax_update(causal(scores(last)), last)

    acc_s = acc_ref[...]

    o_s = acc_s * (1.0 / _swap_halves(acc_s))
    o_c = oc_ref[...]
    o_w = ow_ref[...]
    gate = gate_ref[...]
    res = []
    for hp in range(HEADS_PER_KV):
        sl = slice(hp * qb, (hp + 1) * qb)
        gc = gate[:, N_BRANCH * hp + 0:N_BRANCH * hp + 1]
        gs = gate[:, N_BRANCH * hp + 1:N_BRANCH * hp + 2]
        gw = gate[:, N_BRANCH * hp + 2:N_BRANCH * hp + 3]
        res.append(gc * o_c[sl] + gs * o_s[sl] + gw * o_w[sl])
    cols = [jnp.where(lane < HEAD_DIM, res[2 * k], _swap_halves(res[2 * k + 1])) for k in range(2)]
    o_ref[...] = jnp.concatenate(cols, axis=1).astype(o_ref.dtype)


def _window_bias():
    r = np.arange(Q_BLOCK)[:, None]
    j = np.arange(WINDOW + Q_BLOCK)[None, :]
    early = [j <= Q_BLOCK * v + r for v in range(WINDOW // Q_BLOCK)]
    steady = (j >= r) & (j <= r + WINDOW)
    return jnp.asarray(np.where(np.stack(early + [steady]), 0.0, NEG), F32)


def _attention_prompt(q, gates, slope_rows, wbias, kct, vce, kst, vse, kwt, vwe, nb, seq):
    nq = seq // Q_BLOCK
    n_cmp_pad = kct.shape[-1]
    n_blk_pad = _n_blk_pad(seq)
    rows = HEADS_PER_KV * Q_BLOCK
    n_var = wbias.shape[0]
    qspec = pl.BlockSpec((Q_BLOCK, HEADS_PER_KV * HEAD_DIM), lambda b, g, i: (b * nq + i, g))
    per_bg = lambda shape: pl.BlockSpec((None, None) + shape, lambda b, g, i: (b, g, 0, 0))
    return pl.pallas_call(
        functools.partial(_attn_kernel, n_cmp_pad=n_cmp_pad, n_blk_pad=n_blk_pad),
        grid=(nb, N_KV_HEADS, nq),
        in_specs=[qspec,
                  pl.BlockSpec((Q_BLOCK, LANES), lambda b, g, i: (b * nq + i, g)),
                  pl.BlockSpec((None, rows, LANES), lambda b, g, i: (g, 0, 0)),
                  pl.BlockSpec((None,) + wbias.shape[1:], lambda b, g, i: (jnp.minimum(i, n_var - 1), 0, 0)),
                  per_bg((2 * HEAD_DIM, n_cmp_pad)), per_bg((n_cmp_pad, LANES)),
                  per_bg((n_blk_pad + 2 * HEAD_DIM, seq)), per_bg((seq, LANES)),
                  per_bg((2 * HEAD_DIM, seq)), per_bg((seq, LANES))],
        out_specs=qspec,
        out_shape=jax.ShapeDtypeStruct(q.shape, BF16),
        scratch_shapes=[pltpu.VMEM((rows, n_blk_pad + LANES), BF16), pltpu.VMEM((rows, LANES), F32),
                        pltpu.VMEM((rows, LANES), F32), pltpu.VMEM((rows, LANES), F32),
                        pltpu.VMEM((rows, LANES), F32), pltpu.SMEM((seq // KEY_CHUNK + 2,), jnp.int32)],
        compiler_params=_cparams("arbitrary", "arbitrary", "arbitrary"),
        name="attention_prompt",
    )(q, gates, slope_rows, wbias, kct, vce, kst, vse, kwt, vwe)


def _merge_kernel(x_ref, yab_ref, yc_ref, g_ref, wmg_ref, wb_ref, wo_ref, o_ref):
    x = x_ref[...]
    d = x.shape[1]
    h = _rms(x, g_ref[...]).astype(BF16)
    yab = yab_ref[...]
    branches = (jnp.dot(yab[:, 0:D_CONV], wb_ref[0:D_CONV, :], preferred_element_type=F32),
                jnp.dot(yab[:, D_CONV:], wb_ref[D_CONV:D_CONV + D_GMLP, :], preferred_element_type=F32),
                jnp.dot(yc_ref[...], wb_ref[D_CONV + D_GMLP:, :], preferred_element_type=F32))
    merged = None
    for k, y in enumerate(branches):
        gk = _sigmoid(jnp.dot(h, wmg_ref[:, k * d:(k + 1) * d], preferred_element_type=F32))
        merged = gk * y if merged is None else merged + gk * y
    o_ref[...] = x + jnp.dot(merged.astype(BF16), wo_ref[...], preferred_element_type=F32)


def _merge(x, yab, yc, g, wmg, wb, wo, tm):
    m, d = x.shape
    row = lambda i: (i, 0)
    return pl.pallas_call(
        _merge_kernel,
        grid=(m // tm,),
        in_specs=[pl.BlockSpec((tm, d), row), pl.BlockSpec((tm, yab.shape[1]), row),
                  pl.BlockSpec((tm, yc.shape[1]), row), _const_spec((1, d)), _const_spec(wmg.shape),
                  _const_spec(wb.shape), _const_spec(wo.shape)],
        out_specs=pl.BlockSpec((tm, d), row),
        out_shape=jax.ShapeDtypeStruct((m, d), F32),
        compiler_params=_cparams("arbitrary"),
        name="merge_out",
    )(x, yab, yc, g, wmg, wb, wo)


def _inproj_sample_kernel(x_ref, g_ref, wm_ref, wcg_ref, cw_ref, st0_ref, st1_ref, gn_ref, ws0_ref, bs0_ref,
                          qn_ref, kn_ref, yab_ref, q_ref, gate_ref, kc_ref, vc_ref, ks_ref, vs_ref, kw_ref,
                          vw_ref, zc_ref, vrow_ref):
    h = _rms(x_ref[...], g_ref[...]).astype(BF16)
    z = jnp.dot(h, wm_ref[...], preferred_element_type=F32)
    gate_ref[...] = _sigmoid(jnp.dot(h, wcg_ref[...], preferred_element_type=F32))
    a_b, a_c, a_x = z[:, 0:256], z[:, 256:512], z[:, 512:768]
    zc = a_c * a_x
    cw = cw_ref[...]
    y_a = a_b * (cw[0:1] * st0_ref[...] + cw[1:2] * st1_ref[...] + cw[2:3] * zc)
    zc_ref[...] = zc
    u = _gelu_tanh(z[:, 768:1024])
    v = _rms(_gelu_tanh(z[:, 1024:1280]), gn_ref[...])
    vrow_ref[...] = v
    y_b = u * (ws0_ref[...] * v + bs0_ref[...])
    yab_ref[...] = jnp.concatenate([y_a, y_b], axis=1).astype(BF16)
    ones_bd = _head_group_ones(256)
    qn = qn_ref[...]
    q = jnp.concatenate([_head_rms(z[:, 1280:1536], qn, ones_bd),
                         _head_rms(z[:, 1536:1792], qn, ones_bd)], axis=1)
    q_ref[...] = q * (HEAD_DIM ** -0.5)
    kc_ref[...] = z[:, 1792:1920]
    vc_ref[...] = z[:, 1920:2048]
    vs_ref[...] = z[:, 2176:2304]
    vw_ref[...] = z[:, 2432:2560]
    kn = _head_rms(jnp.concatenate([z[:, 2048:2176], z[:, 2304:2432]], axis=1), kn_ref[...], ones_bd)
    ks_ref[...] = kn[:, 0:128]
    kw_ref[...] = kn[:, 128:256]


def _in_proj_sample(x, g, wm, wcg, cw, st0, st1, gn, ws0, bs0, qn, kn12):
    m = x.shape[0]
    ins = (x, g, wm, wcg, cw, st0, st1, gn, ws0, bs0, qn, kn12)
    sd = lambda w, dt=F32: jax.ShapeDtypeStruct((m, w), dt)
    out_shape = [sd(512, BF16), sd(512), sd(256), sd(128), sd(128), sd(128), sd(128), sd(128), sd(128),
                 sd(256), sd(256)]
    return pl.pallas_call(
        _inproj_sample_kernel,
        grid=(1,),
        in_specs=[_const_spec(a.shape) for a in ins],
        out_specs=[_const_spec(s.shape) for s in out_shape],
        out_shape=out_shape,
        compiler_params=_cparams("arbitrary"),
        name="in_proj_sample",
    )(*ins)


def _head_slopes():
    hrow = lax.broadcasted_iota(jnp.int32, (N_HEADS, 1), 0)
    return lax.bitcast_convert_type((126 - hrow) << 23, F32)


def _sample_cmp_kernel(pt_ref, *refs, n_half, t_pos):
    del pt_ref
    pp = CMP_PAGES
    kpages, vpages = refs[0:pp], refs[pp:2 * pp]
    qz_ref, wt_ref, seg_ref, kn_ref = refs[2 * pp:2 * pp + 4]
    oc_ref, sel_ref = refs[2 * pp + 4:2 * pp + 6]
    p0k_ref, p1k_ref, p0v_ref, p1v_ref = refs[2 * pp + 6:]
    s = pl.program_id(1)
    halves = pp * PAGE_SIZE // CMP_STRIDE
    off = pl.multiple_of(s * halves, halves)
    seg = seg_ref[...]

    kp = [r[...] for r in kpages]
    vp = [r[...] for r in vpages]
    p0k_ref[:, pl.ds(off, halves)] = _half_sums(kp, wt_ref[0], seg)
    p1k_ref[:, pl.ds(off, halves)] = _half_sums(kp, wt_ref[1], seg)
    p0v_ref[:, pl.ds(off, halves)] = _half_sums(vp, wt_ref[2], seg)
    p1v_ref[:, pl.ds(off, halves)] = _half_sums(vp, wt_ref[3], seg)

    @pl.when(s == pl.num_programs(1) - 1)
    def _():
        kc = _head_rms_rows(_combine_halves(p0k_ref[...], p1k_ref[...]), kn_ref[...])
        vc = _combine_halves(p0v_ref[...], p1v_ref[...])
        qz = qz_ref[...].astype(BF16)
        s_c = jnp.dot(qz, kc.astype(BF16), preferred_element_type=F32)
        cmp_end = lax.broadcasted_iota(jnp.int32, (1, n_half), 1) * CMP_STRIDE + (CMP_LEN - 1)
        d_c = t_pos - cmp_end
        vis = d_c >= 0
        s_c = jnp.where(vis, s_c - _head_slopes() * d_c.astype(F32), NEG)
        e_c = jnp.where(vis, jnp.exp(s_c - jnp.max(s_c, axis=1, keepdims=True)), 0.0)
        p_c = e_c * (1.0 / jnp.maximum(jnp.sum(e_c, axis=1, keepdims=True), 1e-30))
        nt = (((1,), (1,)), ((), ()))
        oc_ref[...] = lax.dot_general(p_c.astype(BF16), vc.astype(BF16), nt, preferred_element_type=F32)

        hrow = lax.broadcasted_iota(jnp.int32, p_c.shape, 0)
        ps0 = jnp.sum(jnp.where(hrow < HEADS_PER_KV, p_c, 0.0), axis=0, keepdims=True)
        ps1 = jnp.sum(jnp.where(hrow >= HEADS_PER_KV, p_c, 0.0), axis=0, keepdims=True)
        prow = lax.broadcasted_iota(jnp.int32, (LANES, n_half), 0)
        psum = jnp.where(prow == 0, ps0, jnp.where(prow == 1, ps1, 0.0))
        n_blk_pad = sel_ref.shape[1]
        n_sel = t_pos // SEL_LEN + 1
        bidx = lax.broadcasted_iota(jnp.int32, (n_blk_pad, n_half), 0)
        cidx = lax.broadcasted_iota(jnp.int32, (n_blk_pad, n_half), 1)
        ratio = SEL_LEN // CMP_STRIDE
        band_t = jnp.where((cidx >= ratio * bidx - 1) & (cidx <= ratio * bidx + ratio - 1)
                           & (cidx < n_half - 1), 1.0, 0.0).astype(BF16)
        imp = sum(lax.dot_general(band_t, part, nt, preferred_element_type=F32) for part in _split3(psum))
        blk = lax.broadcasted_iota(jnp.int32, imp.shape, 0)
        cur = t_pos // SEL_LEN
        forced = (blk == 0) | (blk == cur) | (blk == cur - 1)
        future = blk * SEL_LEN > t_pos
        imp = jnp.where(forced, -NEG, imp)
        imp = jnp.where(future, NEG, imp)
        imp = jnp.where(blk < n_sel, imp, -jnp.inf)
        sel_t = (_topk_select_cols(imp, blk.astype(F32), min(TOP_N, n_sel))
                 & jnp.logical_not(future) & (blk < n_sel))
        sel_ref[...] = jnp.where(sel_t, 1.0, 0.0).T[0:SUBLANES]


def _page_specs(layer, n, table_col):
    def spec(k):
        return pl.BlockSpec((None, None, D_KV, PAGE_SIZE),
                            lambda b, s, *tabs: (layer, table_col(tabs, b, s * n + k), 0, 0))
    return [spec(k) for k in range(n)]


def _sample_cmp(page_table, cache_k, cache_v, layer, qz, wt, seg, kn0_col, t_pos):
    nb, n_pages = page_table.shape
    n_half = n_pages * PAGE_SIZE // CMP_STRIDE
    n_blk_pad = -(-(t_pos // SEL_LEN + 1) // LANES) * LANES
    pp = CMP_PAGES
    per_b = lambda shape: pl.BlockSpec((None,) + shape, lambda b, s, pt: (b, 0, 0))
    const = lambda shape: pl.BlockSpec(shape, lambda b, s, pt: (0,) * len(shape))
    pages = lambda: _page_specs(layer, pp, lambda tabs, b, j: tabs[0][b, j])
    grid_spec = pltpu.PrefetchScalarGridSpec(
        num_scalar_prefetch=1,
        grid=(nb, n_pages // pp),
        in_specs=pages() + pages()
        + [per_b((N_HEADS, LANES)), const(wt.shape), const(seg.shape), const(kn0_col.shape)],
        out_specs=[per_b((N_HEADS, LANES)), per_b((SUBLANES, n_blk_pad))],
        scratch_shapes=[pltpu.VMEM((D_KV, n_half), F32)] * 4,
    )
    return pl.pallas_call(
        functools.partial(_sample_cmp_kernel, n_half=n_half, t_pos=t_pos),
        grid_spec=grid_spec,
        out_shape=[jax.ShapeDtypeStruct((nb, N_HEADS, LANES), F32),
                   jax.ShapeDtypeStruct((nb, SUBLANES, n_blk_pad), F32)],
        compiler_params=_cparams("arbitrary", "arbitrary"),
        name="sample_cmp",
    )(page_table, *([cache_k] * pp), *([cache_v] * pp), qz, wt, seg, kn0_col)


def _sample_sel_kernel(phys_ref, lp_ref, *refs, t_pos):
    del phys_ref
    pp = SEL_PAGES
    kpages, vpages = refs[0:pp], refs[pp:2 * pp]
    qz_ref, sel_ref, kn_ref, vn_ref, o_ref, m_ref, l_ref, acc_ref = refs[2 * pp:]
    b, s = pl.program_id(0), pl.program_id(1)
    n_keys = pp * PAGE_SIZE

    @pl.when(s == 0)
    def _():
        m_ref[...] = jnp.full(m_ref.shape, NEG, F32)
        l_ref[...] = jnp.zeros(l_ref.shape, F32)
        acc_ref[...] = jnp.zeros(acc_ref.shape, F32)

    lane = lax.broadcasted_iota(jnp.int32, (1, PAGE_SIZE), 1)
    lps = [lp_ref[b, s * pp + k] for k in range(pp)]
    kpos = jnp.concatenate([lp * PAGE_SIZE + lane for lp in lps], axis=1)
    kblk = jnp.concatenate([lp * (PAGE_SIZE // SEL_LEN) + lane // SEL_LEN for lp in lps], axis=1)
    qz = qz_ref[...].astype(BF16)
    kk = jnp.concatenate([r[...] for r in kpages], axis=1).astype(BF16)
    vv = jnp.concatenate([r[...] for r in vpages], axis=1).astype(BF16)
    sc = jnp.dot(qz, kk, preferred_element_type=F32)
    sc = sc - _head_slopes() * (t_pos - kpos).astype(F32)
    n_blk_pad = sel_ref.shape[1]
    hrow = lax.broadcasted_iota(jnp.int32, (N_HEADS, n_blk_pad), 0)
    selv = sel_ref[...]
    sel_h = jnp.where(hrow < HEADS_PER_KV, selv[0:1], selv[1:2]).astype(BF16)
    erow = lax.broadcasted_iota(jnp.int32, (n_blk_pad, n_keys), 0)
    ok = jnp.dot(sel_h, jnp.where(erow == kblk, 1.0, 0.0).astype(BF16), preferred_element_type=F32) > 0.5
    sc = jnp.where(ok, sc, NEG)
    m_old = m_ref[...]
    m_new = jnp.maximum(m_old, jnp.max(sc, axis=1, keepdims=True))
    p = jnp.where(ok, jnp.exp(sc - m_new), 0.0)
    alpha = jnp.exp(m_old - m_new)
    l_ref[...] = alpha * l_ref[...] + jnp.sum(p, axis=1, keepdims=True)
    acc_ref[...] = alpha * acc_ref[...] + lax.dot_general(p.astype(BF16), vv, (((1,), (1,)), ((), ())),
                                                           preferred_element_type=F32)
    m_ref[...] = m_new

    @pl.when(s == pl.num_programs(1) - 1)
    def _():
        k_new = kn_ref[...].astype(BF16).astype(F32)
        s_new = jnp.sum(qz.astype(F32) * k_new, axis=1, keepdims=True)
        m_o = m_ref[...]
        m_n = jnp.maximum(m_o, s_new)
        a = jnp.exp(m_o - m_n)
        p_new = jnp.exp(s_new - m_n)
        l = a * l_ref[...] + p_new
        acc = a * acc_ref[...] + p_new.astype(BF16).astype(F32) * vn_ref[...].astype(BF16).astype(F32)
        o_ref[...] = acc * (1.0 / l)


def _needed_pages(sel, page_table):
    nb, n_pages = page_table.shape
    per_page = PAGE_SIZE // SEL_LEN
    flags = sel[:, :N_KV_HEADS, :n_pages * per_page] > 0.5
    need = flags.reshape(nb, N_KV_HEADS, n_pages, per_page).any(axis=(1, 3))
    n_slots = min(SEL_SLOTS, n_pages)
    order = jnp.argsort(jnp.logical_not(need), axis=1, stable=True)[:, :n_slots]
    count = need.sum(axis=1, keepdims=True)
    valid = jnp.arange(n_slots, dtype=jnp.int32)[None, :] < count
    logical = jnp.where(valid, order, -1).astype(jnp.int32)
    phys = jnp.take_along_axis(page_table, jnp.maximum(logical, 0), axis=1).astype(jnp.int32)
    return phys, logical


def _sample_sel(phys, logical, cache_k, cache_v, layer, qz, sel, ks_new, vs_new, t_pos):
    nb = phys.shape[0]
    pp = SEL_PAGES
    per_b = lambda shape: pl.BlockSpec((None,) + shape, lambda b, s, ph, lp: (b, 0, 0))
    pages = lambda: _page_specs(layer, pp, lambda tabs, b, j: tabs[0][b, j])
    grid_spec = pltpu.PrefetchScalarGridSpec(
        num_scalar_prefetch=2,
        grid=(nb, phys.shape[1] // pp),
        in_specs=pages() + pages()
        + [per_b((N_HEADS, LANES)), per_b(sel.shape[1:]), per_b((1, D_KV)), per_b((1, D_KV))],
        out_specs=per_b((N_HEADS, LANES)),
        scratch_shapes=[pltpu.VMEM((N_HEADS, 1), F32), pltpu.VMEM((N_HEADS, 1), F32),
                        pltpu.VMEM((N_HEADS, LANES), F32)],
    )
    return pl.pallas_call(
        functools.partial(_sample_sel_kernel, t_pos=t_pos),
        grid_spec=grid_spec,
        out_shape=jax.ShapeDtypeStruct((nb, N_HEADS, LANES), F32),
        compiler_params=_cparams("arbitrary", "arbitrary"),
        name="sample_sel",
    )(phys, logical, *([cache_k] * pp), *([cache_v] * pp), qz, sel, ks_new, vs_new)


def _sample_win_kernel(qz_ref, wk_ref, wv_ref, kn_ref, vn_ref, gate_ref, oc_ref, os_ref,
                       y_ref, wko_ref, wvo_ref):
    qz = qz_ref[...].astype(BF16)
    kwin, vwin = wk_ref[...], wv_ref[...]
    n_win = kwin.shape[1]
    nt = (((1,), (1,)), ((), ()))
    sc = jnp.dot(qz, kwin.astype(BF16), preferred_element_type=F32)
    dist = n_win - lax.broadcasted_iota(jnp.int32, (1, n_win), 1)
    sc = sc - _head_slopes() * dist.astype(F32)
    k_new, v_new = kn_ref[...], vn_ref[...]
    s_new = jnp.sum(qz.astype(F32) * k_new.astype(BF16).astype(F32), axis=1, keepdims=True)
    m = jnp.maximum(jnp.max(sc, axis=1, keepdims=True), s_new)
    p = jnp.exp(sc - m)
    p_new = jnp.exp(s_new - m)
    l = jnp.sum(p, axis=1, keepdims=True) + p_new
    acc = (lax.dot_general(p.astype(BF16), vwin.astype(BF16), nt, preferred_element_type=F32)
           + p_new.astype(BF16).astype(F32) * v_new.astype(BF16).astype(F32))
    o_w = acc * (1.0 / l)
    gate = gate_ref[...]
    y = gate[:, 0:1] * oc_ref[...] + gate[:, 1:2] * os_ref[...] + gate[:, 2:3] * o_w
    lane = lax.broadcasted_iota(jnp.int32, (1, LANES), 1)
    cols = []
    for k in range(N_HEADS // 2):
        grp = (2 * k) // HEADS_PER_KV
        even, odd = y[2 * k:2 * k + 1], y[2 * k + 1:2 * k + 2]
        low = even if grp == 0 else _swap_halves(even)
        high = odd if grp == 1 else _swap_halves(odd)
        cols.append(jnp.where(lane < HEAD_DIM, low, high))
    y_ref[...] = jnp.concatenate(cols, axis=1).astype(y_ref.dtype)
    eye = (lax.broadcasted_iota(jnp.int32, (D_KV, D_KV), 0) == lax.broadcasted_iota(jnp.int32, (D_KV, D_KV), 1))
    as_col = lambda r: jnp.sum(jnp.where(eye, r, 0.0), axis=1, keepdims=True)
    pos = lax.broadcasted_iota(jnp.int32, kwin.shape, 1)
    wko_ref[...] = jnp.where(pos == n_win - 1, as_col(k_new), pltpu.roll(kwin, n_win - 1, axis=1))
    wvo_ref[...] = jnp.where(pos == n_win - 1, as_col(v_new), pltpu.roll(vwin, n_win - 1, axis=1))


def _sample_win(qz, win_k, win_v, layer, kw_new, vw_new, gates_h, o_c, o_s):
    nb = qz.shape[0]
    n_win = win_k.shape[3]
    per_b = lambda shape: pl.BlockSpec((None,) + shape, lambda b: (b, 0, 0))
    cache = pl.BlockSpec((None, None, D_KV, n_win), lambda b: (layer, b, 0, 0))
    hl = (N_HEADS, LANES)
    return pl.pallas_call(
        _sample_win_kernel,
        grid=(nb,),
        in_specs=[per_b(hl), cache, cache, per_b((1, D_KV)), per_b((1, D_KV)), per_b(hl), per_b(hl), per_b(hl)],
        out_specs=[per_b((1, D_ATTN)), per_b((D_KV, n_win)), per_b((D_KV, n_win))],
        out_shape=[jax.ShapeDtypeStruct((nb, 1, D_ATTN), F32), jax.ShapeDtypeStruct((nb, D_KV, n_win), F32),
                   jax.ShapeDtypeStruct((nb, D_KV, n_win), F32)],
        compiler_params=_cparams("arbitrary"),
        name="sample_win",
    )(qz, win_k, win_v, kw_new, vw_new, gates_h, o_c, o_s)


def _slope_rows():
    out = np.zeros((N_KV_HEADS, HEADS_PER_KV * Q_BLOCK, LANES), np.float32)
    for g in range(N_KV_HEADS):
        for hp in range(HEADS_PER_KV):
            slope = 2.0 ** -(g * HEADS_PER_KV + hp + 1)
            out[g, hp * Q_BLOCK:(hp + 1) * Q_BLOCK, HEAD_DIM] = slope * 128.0
            out[g, hp * Q_BLOCK:(hp + 1) * Q_BLOCK, HEAD_DIM + 1] = slope
    return jnp.asarray(out)


def _tile_lanes(v, reps):
    return jnp.tile(v.reshape(1, -1), (1, reps))


def _heads_to_rows(q):
    n = q.shape[0]
    qh = q.reshape(n, N_HEADS, HEAD_DIM)
    z = jnp.zeros_like(qh[:, :HEADS_PER_KV])
    return jnp.concatenate([jnp.concatenate([qh[:, :HEADS_PER_KV], z], axis=-1),
                            jnp.concatenate([z, qh[:, HEADS_PER_KV:]], axis=-1)], axis=1)


def kernel(x_prompt, x_sample, cache_cmp_k, cache_cmp_v, cache_sel_k, cache_sel_v, cache_win_k, cache_win_v, state_conv, page_table, ffn1_norm, ffn1_w_gate, ffn1_w_up, ffn1_w_down, mix_norm, w_in, conv_w, gmlp_norm, gmlp_ws, gmlp_bs, q_norm, k_norm, cmp_wk, cmp_wv, w_branch, w_out, ffn2_norm, ffn2_w_gate, ffn2_w_up, ffn2_w_down):
    nb, seq, d = x_prompt.shape
    ns = x_sample.shape[0]
    depth = w_in.shape[0]
    t_pos = page_table.shape[1] * PAGE_SIZE
    assert x_sample.shape[1] == 1 and cache_win_k.shape[2] == WINDOW
    assert seq % (CMP_STRIDE * LANES) == 0 and t_pos % (CMP_STRIDE * LANES) == 0
    assert page_table.shape[1] % CMP_PAGES == 0 and seq % (CMP_PAGES * PAGE_SIZE) == 0
    tm = min(ROW_TILE, seq)

    xp = x_prompt.reshape(nb * seq, d)
    xs = x_sample.reshape(ns, d)
    feat_major = lambda c: jnp.transpose(c, (0, 1, 3, 4, 2)).reshape(depth, c.shape[1], D_KV, c.shape[2])
    ck, cv, sk, sv = (feat_major(c) for c in (cache_cmp_k, cache_cmp_v, cache_sel_k, cache_sel_v))
    wink, winv = feat_major(cache_win_k), feat_major(cache_win_v)
    slope_rows = _slope_rows()
    wbias = _window_bias()
    seg_rows = CMP_PAGES * PAGE_SIZE
    seg = jnp.asarray(np.arange(seg_rows)[:, None] // CMP_STRIDE == np.arange(seg_rows // CMP_STRIDE)[None, :], BF16)

    prompt_new = [[] for _ in range(7)]
    sample_new = [[] for _ in range(8)]
    for l in range(depth):
        bf = lambda w: w.astype(BF16)
        w_main = bf(w_in[l, :, :MAIN_COLS])
        cg = w_in[l, :, MAIN_COLS:MAIN_COLS + N_BRANCH * N_HEADS].reshape(d, N_KV_HEADS, HEADS_PER_KV * N_BRANCH)
        w_cg = bf(jnp.pad(cg, ((0, 0), (0, 0), (0, LANES - HEADS_PER_KV * N_BRANCH))).reshape(d, N_KV_HEADS * LANES))
        w_mg = bf(w_in[l, :, MAIN_COLS + N_BRANCH * N_HEADS:])
        wb, wo = bf(w_branch[l]), bf(w_out[l])
        f1 = (ffn1_norm[l].reshape(1, d), bf(ffn1_w_gate[l]), bf(ffn1_w_up[l]), bf(ffn1_w_down[l]))
        f2 = (ffn2_norm[l].reshape(1, d), bf(ffn2_w_gate[l]), bf(ffn2_w_up[l]), bf(ffn2_w_down[l]))
        mn = mix_norm[l].reshape(1, d)
        cw = conv_w[l]
        gn = gmlp_norm[l].reshape(1, D_GMLP)
        gdim = D_GMLP // GMLP_GROUPS
        bs_tile = jnp.repeat(gmlp_bs[l].T, gdim, axis=1)
        ws0 = jnp.repeat(gmlp_ws[l, :, 0, 0], gdim).reshape(1, D_GMLP)
        bs0 = jnp.repeat(gmlp_bs[l, :, 0], gdim).reshape(1, D_GMLP)
        qn = _tile_lanes(q_norm[l], 4)
        kn0 = _tile_lanes(k_norm[l, 0], 2)
        kn12 = jnp.concatenate([_tile_lanes(k_norm[l, 1], 2), _tile_lanes(k_norm[l, 2], 2)], axis=1)
        wk = cmp_wk[l].reshape(CMP_LEN, D_KV)
        wv = cmp_wv[l].reshape(CMP_LEN, D_KV)
        taps = lambda w: jnp.tile(w.T, (1, PAGE_SIZE // CMP_STRIDE))
        wt = jnp.stack([taps(wk[:CMP_STRIDE]), taps(wk[CMP_STRIDE:]), taps(wv[:CMP_STRIDE]), taps(wv[CMP_STRIDE:])])

        xp = _half_ffn(xp, *f1, tm)
        (yab, q, gates, kc, vc, ks, vs, kww, vww, conv_new, kst, vse, kwt, vwe) = _in_proj_prompt(
            xp, nb, seq, mn, w_main, w_cg, cw, gn, gmlp_ws[l], bs_tile, qn, kn12, tm)
        kct, vce = _compress_prompt(kc, vc, nb, seq, wt, seg, kn0.reshape(D_KV, 1))
        yc = _attention_prompt(q, gates, slope_rows, wbias, kct, vce, kst, vse, kwt, vwe, nb, seq)
        xp = _merge(xp, yab, yc, mn, w_mg, wb, wo, tm)
        xp = _half_ffn(xp, *f2, tm)
        for lst, a in zip(prompt_new, (kc, vc, ks, vs, kww, vww, conv_new)):
            lst.append(a)

        xs = _half_ffn(xs, *f1, ns)
        (yab_s, q_s, gates_s, kc_s, vc_s, ks_s, vs_s, kw_s, vw_s, zc_s, vrow_s) = _in_proj_sample(
            xs, mn, w_main, w_cg, cw, state_conv[l, :, 0], state_conv[l, :, 1], gn, ws0, bs0, qn, kn12)
        qz = _heads_to_rows(q_s)
        o_c, sel = _sample_cmp(page_table, ck, cv, l, qz, wt, seg, kn0.reshape(D_KV, 1), t_pos)
        phys, logical = _needed_pages(sel, page_table)
        o_s = _sample_sel(phys, logical, sk, sv, l, qz, sel, ks_s.reshape(ns, 1, D_KV),
                          vs_s.reshape(ns, 1, D_KV), t_pos)
        gh = gates_s.reshape(ns, N_KV_HEADS, LANES)[:, :, :HEADS_PER_KV * N_BRANCH].reshape(ns, N_HEADS, N_BRANCH)
        gh = jnp.pad(gh, ((0, 0), (0, 0), (0, LANES - N_BRANCH)))
        yc_s, wk_new, wv_new = _sample_win(qz, wink, winv, l, kw_s.reshape(ns, 1, D_KV),
                                           vw_s.reshape(ns, 1, D_KV), gh, o_c, o_s)
        yc_s = yc_s.reshape(ns, D_ATTN).astype(BF16)
        xs = _merge(xs, yab_s, yc_s, mn, w_mg, wb, wo, ns)
        xs = _half_ffn(xs, *f2, ns)
        kv5s = lambda a: a.reshape(ns, -1, N_KV_HEADS, HEAD_DIM)
        conv_s = jnp.stack([state_conv[l, :, 1], zc_s], axis=1)
        rows_major = lambda a: jnp.transpose(a.reshape(ns, N_KV_HEADS, HEAD_DIM, -1), (0, 3, 1, 2))
        for lst, a in zip(sample_new, (kv5s(kc_s), kv5s(vc_s), kv5s(ks_s), kv5s(vs_s), rows_major(wk_new),
                                       rows_major(wv_new), conv_s, vrow_s.reshape(ns, 1, D_GMLP))):
            lst.append(a)

    tokens_major = lambda a: jnp.transpose(a.reshape(depth, nb, N_KV_HEADS, HEAD_DIM, -1), (0, 1, 4, 2, 3))
    outs_p = [jnp.stack(a) for a in prompt_new]
    outs_p = [tokens_major(a) for a in outs_p[:6]] + outs_p[6:]
    outs_s = [jnp.stack(a) for a in sample_new]
    return (xp.reshape(nb, seq, d), xs.reshape(ns, 1, d), *outs_p, *outs_s)
```
